```python
import math, functools
import jax, jax.numpy as jnp
from jax import lax
import numpy as np

D_MODEL = 1024
BATCH = 8
SEQ = 2048
DEPTH = 1
DEC_BATCH = 32
DEC_SEQ = 8
PAST_LEN = 8192
PAGE_SIZE = 128

D_MIX = D_MODEL
A_HEADS = 4
A_DK = 128
A_DV = 128
A_WIDTH = A_HEADS * A_DV
B_HEADS = 8
B_DH = 64
B_WIDTH = B_HEADS * B_DH
IDX_HEADS = 16
IDX_DIM = 64
TOPK_MAX = 256
ROPE_THETA = 500000.0
B_ROT = B_DH // 4
IDX_ROT = IDX_DIM // 4
HGRN_CHUNK = 64
Q_BLOCK = 128
LN_EPS = 1e-5
ALPHA = (2.0 * DEPTH) ** 0.25
BETA = (8.0 * DEPTH) ** -0.25
IN_WIDTHS = (A_HEADS * A_DK, A_HEADS * A_DK, A_WIDTH, A_WIDTH,
             B_WIDTH, B_WIDTH, B_WIDTH, B_WIDTH,
             IDX_HEADS * IDX_DIM, IDX_DIM, IDX_HEADS)
IN_OFFSETS = tuple(sum(IN_WIDTHS[: i + 1]) for i in range(len(IN_WIDTHS) - 1))
D_IN = sum(IN_WIDTHS)

kernel_name = "hymba_hgrn2_dsa_deepnorm_step"


def layer_norm(x, g, b):
    xf = x.astype(jnp.float32)
    mu = jnp.mean(xf, -1, keepdims=True)
    var = jnp.mean(jnp.square(xf - mu), -1, keepdims=True)
    return ((xf - mu) * lax.rsqrt(var + LN_EPS) * g + b).astype(x.dtype)


def rms_norm(x, g):
    xf = x.astype(jnp.float32)
    return xf * lax.rsqrt(jnp.mean(xf * xf, -1, keepdims=True) + LN_EPS) * g


def partial_rope(x, pos, rot):
    half = rot // 2
    inv = ROPE_THETA ** (-jnp.arange(half, dtype=jnp.float32) / half)
    ang = pos.astype(jnp.float32)[:, None] * inv[None, :]
    cos = jnp.cos(ang)[:, None, :]
    sin = jnp.sin(ang)[:, None, :]
    xf = x.astype(jnp.float32)
    x1, x2 = xf[..., :half], xf[..., half:rot]
    out = jnp.concatenate([x1 * cos - x2 * sin, x1 * sin + x2 * cos, xf[..., rot:]], -1)
    return out.astype(x.dtype)


def hgrn_lower_bounds(lb_logits):
    p = jax.nn.softmax(lb_logits.astype(jnp.float32), axis=0)
    return jnp.cumsum(p, axis=0)[:DEPTH]


def hgrn2_recurrence(q, k, v, logf, s0):
    B, L, H, _ = q.shape
    DV = v.shape[-1]
    c = math.gcd(L, HGRN_CHUNK)
    n = L // c

    def chunks(a):
        return a.astype(jnp.float32).reshape(B, n, c, H, a.shape[-1]).transpose(1, 0, 3, 2, 4)

    causal = jnp.tril(jnp.ones((c, c), dtype=bool))[:, :, None]

    def step(S, inp):
        qc, kc, vc, gc = inp
        cum = jnp.cumsum(gc, axis=2)
        o_inter = jnp.einsum('bhtk,bhkv->bhtv', qc * jnp.exp(cum), S)
        rel = jnp.where(causal, cum[:, :, :, None, :] - cum[:, :, None, :, :], -jnp.inf)
        scores = jnp.einsum('bhtk,bhsk,bhtsk->bhts', qc, kc, jnp.exp(rel))
        o_intra = jnp.einsum('bhts,bhsv->bhtv', scores, vc)
        last = cum[:, :, -1:, :]
        S_new = (jnp.exp(last[:, :, 0, :])[..., None] * S
                 + jnp.einsum('bhsk,bhsv->bhkv', kc * jnp.exp(last - cum), vc))
        return S_new, o_inter + o_intra

    s_final, o = lax.scan(step, s0.astype(jnp.float32),
                          (chunks(q), chunks(k), chunks(v), chunks(logf)))
    o = o.transpose(1, 0, 3, 2, 4).reshape(B, L, H, DV)
    return o, s_final


def hgrn2_branch(qa, fa, ia, ga, lb, norm_g, s0):
    B, L, _ = qa.shape
    hd = lambda a, d: a.reshape(B, L, A_HEADS, d)
    fa32 = fa.astype(jnp.float32)
    logf = jnp.log(lb + (1.0 - lb) * jax.nn.sigmoid(fa32))
    key = (1.0 - lb) * jax.nn.sigmoid(-fa32)
    o, s_new = hgrn2_recurrence(hd(jax.nn.silu(qa), A_DK), hd(key, A_DK),
                                hd(ia, A_DV), hd(logf, A_DK), s0)
    o = rms_norm(o, norm_g).reshape(B, L, A_WIDTH)
    return o * jax.nn.silu(ga.astype(jnp.float32)), s_new


def dsa_inputs(qb, kb, vb, qi, ki, wi, pos, kn_g, kn_b):
    B, L, _ = qb.shape
    q = partial_rope(qb.reshape(B, L, B_HEADS, B_DH), pos, B_ROT)
    k = partial_rope(kb.reshape(B, L, B_HEADS, B_DH), pos, B_ROT)
    v = vb.reshape(B, L, B_HEADS, B_DH)
    qi = partial_rope(qi.reshape(B, L, IDX_HEADS, IDX_DIM), pos, IDX_ROT)
    ki = partial_rope(layer_norm(ki, kn_g, kn_b)[:, :, None, :], pos, IDX_ROT)[:, :, 0, :]
    wi = wi * (IDX_HEADS ** -0.5 * IDX_DIM ** -0.5)
    return q, k, v, qi, ki, wi


def indexer_scores(qi, wi, ki, qpos, kpos):
    rel = jax.nn.relu(jnp.einsum('bthd,bsd->bths', qi.astype(jnp.float32), ki.astype(jnp.float32)))
    s = jnp.einsum('bths,bth->bts', rel, wi.astype(jnp.float32))
    return jnp.where(kpos[None, None, :] <= qpos[None, :, None], s, -jnp.inf)


def attend_selected(q, kg, vg, valid):
    logits = jnp.einsum('bthd,btkhd->bhtk', q.astype(jnp.float32), kg.astype(jnp.float32)) * (B_DH ** -0.5)
    logits = jnp.where(valid[:, None], logits, -jnp.inf)
    p = jax.nn.softmax(logits, axis=-1)
    return jnp.einsum('bhtk,btkhd->bthd', p, vg.astype(jnp.float32))


def dsa_prompt(q, k, v, qi, ki, wi):
    B, L = q.shape[:2]
    topk = min(TOPK_MAX, L // 4)
    blk = math.gcd(L, Q_BLOCK)
    kpos = jnp.arange(L, dtype=jnp.int32)
    gather = jax.vmap(lambda rows, idx: rows[idx])

    def one_block(i):
        start = i * blk
        qpos = start + jnp.arange(blk, dtype=jnp.int32)
        sl = lambda a: lax.dynamic_slice_in_dim(a, start, blk, axis=1)
        scores = indexer_scores(sl(qi), sl(wi), ki, qpos, kpos)
        _, sel = lax.top_k(scores, topk)
        valid = sel <= qpos[None, :, None]
        return attend_selected(sl(q), gather(k, sel), gather(v, sel), valid)

    o = lax.map(one_block, jnp.arange(L // blk))
    return o.transpose(1, 0, 2, 3, 4).reshape(B, L, B_HEADS, B_DH)


def dsa_sample(q, k, v, qi, ki, wi, cache_k, cache_v, cache_kidx, page_table, layer):
    Bd, T = q.shape[:2]
    n_pages = page_table.shape[1]
    page = cache_k.shape[2]
    past = n_pages * page
    L = past + T
    topk = min(TOPK_MAX, L // 4)
    ki_past = cache_kidx[page_table, layer].reshape(Bd, past, IDX_DIM)
    ki_all = jnp.concatenate([ki_past.astype(jnp.float32), ki.astype(jnp.float32)], axis=1)
    qpos = past + jnp.arange(T, dtype=jnp.int32)
    kpos = jnp.arange(L, dtype=jnp.int32)
    scores = indexer_scores(qi, wi, ki_all, qpos, kpos)
    _, sel = lax.top_k(scores, topk)
    valid = sel <= qpos[None, :, None]
    from_past = (sel < past)[..., None, None]
    p_sel = jnp.minimum(sel, past - 1)
    phys = jnp.take_along_axis(page_table, (p_sel // page).reshape(Bd, -1), axis=1).reshape(p_sel.shape)
    off = p_sel % page
    n_sel = jnp.clip(sel - past, 0, T - 1)
    gather = jax.vmap(lambda rows, idx: rows[idx])
    kg = jnp.where(from_past, cache_k[phys, layer, off].astype(jnp.float32), gather(k, n_sel).astype(jnp.float32))
    vg = jnp.where(from_past, cache_v[phys, layer, off].astype(jnp.float32), gather(v, n_sel).astype(jnp.float32))
    return attend_selected(q, kg, vg, valid)


def merge_and_norm(x, o_a, o_b, g_b, w_out_l, ln_g_l, ln_b_l):
    B, L, _ = x.shape
    o_b = o_b.reshape(B, L, B_WIDTH) * jax.nn.silu(g_b.astype(jnp.float32))
    mix = jnp.concatenate([o_a, o_b], axis=-1).astype(x.dtype)
    y = jnp.einsum('ble,ed->bld', mix, w_out_l)
    return layer_norm(ALPHA * x + y, ln_g_l, ln_b_l)


def layer_forward(x, pos, s0, attend, w_in_l, lb_l, norm_g_l, kn_g_l, kn_b_l, w_out_l, ln_g_l, ln_b_l):
    h = jnp.einsum('bld,de->ble', x, w_in_l)
    qa, fa, ia, ga, qb, kb, vb, gb, qi, ki, wi = jnp.split(h, IN_OFFSETS, axis=-1)
    o_a, s_new = hgrn2_branch(qa, fa, ia, ga, lb_l, norm_g_l, s0)
    q, k, v, qi, ki, wi = dsa_inputs(qb, kb, vb, qi, ki, wi, pos, kn_g_l, kn_b_l)
    o_b = attend(q, k, v, qi, ki, wi)
    x_new = merge_and_norm(x, o_a, o_b, gb, w_out_l, ln_g_l, ln_b_l)
    return x_new, k, v, ki, s_new


def setup_inputs(seed: int = 0) -> dict:
    key = jax.random.key(seed)
    ks = jax.random.split(key, 16)
    n_pages = PAST_LEN // PAGE_SIZE
    n_used = DEC_BATCH * n_pages
    n_pool = (n_used * 5) // 4
    f32 = jnp.float32
    nrm = lambda k, shape, s: s * jax.random.normal(k, shape, f32)
    page_table = jax.random.permutation(ks[0], n_pool)[:n_used].reshape(DEC_BATCH, n_pages).astype(jnp.int32)
    return {
        "x_prompt": nrm(ks[1], (BATCH, SEQ, D_MODEL), 1.0),
        "x_sample": nrm(ks[2], (DEC_BATCH, DEC_SEQ, D_MODEL), 1.0),
        "cache_k": nrm(ks[3], (n_pool, DEPTH, PAGE_SIZE, B_HEADS, B_DH), 1.0),
        "cache_v": nrm(ks[4], (n_pool, DEPTH, PAGE_SIZE, B_HEADS, B_DH), 1.0),
        "cache_kidx": nrm(ks[5], (n_pool, DEPTH, PAGE_SIZE, IDX_DIM), 1.0),
        "state_hgrn": nrm(ks[6], (DEPTH, DEC_BATCH, A_HEADS, A_DK, A_DV), 0.5),
        "page_table": page_table,
        "w_in": nrm(ks[7], (DEPTH, D_MODEL, D_IN), D_MODEL ** -0.5),
        "hgrn_lb_logits": nrm(ks[8], (DEPTH + 1, A_HEADS * A_DK), 0.5),
        "hgrn_norm_g": 1.0 + nrm(ks[9], (DEPTH, A_DV), 0.02),
        "idx_norm_g": 1.0 + nrm(ks[10], (DEPTH, IDX_DIM), 0.02),
        "idx_norm_b": nrm(ks[11], (DEPTH, IDX_DIM), 0.02),
        "w_out": nrm(ks[12], (DEPTH, D_MIX, D_MODEL), BETA * D_MIX ** -0.5),
        "ln_g": 1.0 + nrm(ks[13], (DEPTH, D_MODEL), 0.02),
        "ln_b": nrm(ks[14], (DEPTH, D_MODEL), 0.02),
    }


def reference(x_prompt, x_sample, cache_k, cache_v, cache_kidx, state_hgrn, page_table,
              w_in, hgrn_lb_logits, hgrn_norm_g, idx_norm_g, idx_norm_b, w_out, ln_g, ln_b):
    lbs = hgrn_lower_bounds(hgrn_lb_logits)
    B, L, _ = x_prompt.shape
    Bd, T, _ = x_sample.shape
    page = cache_k.shape[2]
    past = page_table.shape[1] * page
    pos_p = jnp.arange(L, dtype=jnp.int32)
    pos_s = past + jnp.arange(T, dtype=jnp.int32)
    s0_prompt = jnp.zeros((B, A_HEADS, A_DK, A_DV), jnp.float32)
    xp, xs = x_prompt, x_sample
    kp, vp, kip, sp = [], [], [], []
    ksm, vsm, kism, ssm = [], [], [], []
    for l in range(DEPTH):
        w_l = (w_in[l], lbs[l], hgrn_norm_g[l], idx_norm_g[l], idx_norm_b[l], w_out[l], ln_g[l], ln_b[l])
        xp, k, v, ki, s = layer_forward(xp, pos_p, s0_prompt, dsa_prompt, *w_l)
        kp.append(k.reshape(B, L // page, page, B_HEADS, B_DH))
        vp.append(v.reshape(B, L // page, page, B_HEADS, B_DH))
        kip.append(ki.reshape(B, L // page, page, IDX_DIM))
        sp.append(s)
        attend_s = functools.partial(dsa_sample, cache_k=cache_k, cache_v=cache_v,
                                     cache_kidx=cache_kidx, page_table=page_table, layer=l)
        xs, k, v, ki, s = layer_forward(xs, pos_s, state_hgrn[l], attend_s, *w_l)
        ksm.append(k)
        vsm.append(v)
        kism.append(ki)
        ssm.append(s)
    new_k_prompt = jnp.stack(kp, axis=2)
    new_v_prompt = jnp.stack(vp, axis=2)
    new_kidx_prompt = jnp.stack(kip, axis=2)
    new_state_prompt = jnp.stack(sp, axis=0)
    new_k_sample = jnp.stack(ksm, axis=1)
    new_v_sample = jnp.stack(vsm, axis=1)
    new_kidx_sample = jnp.stack(kism, axis=1)
    new_state_sample = jnp.stack(ssm, axis=0)
    return (xp, xs, new_k_prompt, new_v_prompt, new_kidx_prompt, new_state_prompt,
            new_k_sample, new_v_sample, new_kidx_sample, new_state_sample)
```

```python
import functools
import math

import jax
import jax.numpy as jnp
import numpy as np
from jax import lax
from jax.experimental import pallas as pl
from jax.experimental.pallas import tpu as pltpu

F32 = jnp.float32
BF16 = jnp.bfloat16
I32 = jnp.int32

A_HEADS = 4
A_DK = 128
A_DV = 128
B_HEADS = 8
B_DH = 64
IDX_HEADS = 16
IDX_DIM = 64
TOPK_MAX = 256
ROPE_THETA = 500000.0
ROT = 16
ROT_HALF = ROT // 2
LN_EPS = 1e-5
Q_BLOCK = 128
K_TILE = 256
HGRN_CHUNK = 128
HGRN_SUB = 16
LANES = 128
VMEM_LIMIT = 56 * 1024 * 1024

A_WIDTH = A_HEADS * A_DV
B_WIDTH = B_HEADS * B_DH
NAT_WIDTH = 7 * 512 + LANES
TR_ROWS = 512 + 512 + IDX_HEADS * IDX_DIM + IDX_HEADS
INT_MIN = -(2 ** 31)
KEY_NEG_INF = -0x7F800000


def _dot(a, b):
    return jnp.dot(a, b, preferred_element_type=F32)


def _dot_nt(a, b):
    return lax.dot_general(a, b, (((1,), (1,)), ((), ())), preferred_element_type=F32)


def _dot_tn(a, b):
    return lax.dot_general(a, b, (((0,), (0,)), ((), ())), preferred_element_type=F32)


def _silu(x):
    return x * jax.nn.sigmoid(x)


def _sort_key(x):
    b = pltpu.bitcast(x, I32)
    m = b >> 31
    return ((b & 0x7FFFFFFF) ^ m) - m


def _proj_kernel(x_ref, wn_ref, wt_ref, lb_ref, kng_ref, knb_ref, cn_ref, sa_ref, sb_ref, ct_ref, st_ref,
                 hq_ref, hk_ref, hg_ref, hv_ref, hgate_ref, k_ref, kbf_ref, v_ref, gb_ref, ki_ref, kibf_ref,
                 qT_ref, vT_ref, qiT_ref, wT_ref):
    tm = x_ref.shape[0]
    xb = x_ref[...].astype(BF16)

    def nat(col, width=512):
        return _dot(xb, wn_ref[:, col:col + width])

    lb = lb_ref[...]
    qa = nat(0)
    hq = _silu(qa)
    fa = nat(512)
    hg = jnp.log(lb + (1.0 - lb) * jax.nn.sigmoid(fa))
    hk = (1.0 - lb) * jax.nn.sigmoid(-fa)
    ia = nat(1024)
    hgate = _silu(nat(1536))
    for h in range(A_HEADS):
        sl = slice(h * A_DK, (h + 1) * A_DK)
        hq_ref[h] = hq[:, sl].astype(BF16)
        hk_ref[h] = hk[:, sl]
        hg_ref[h] = hg[:, sl]
        hv_ref[h] = ia[:, sl].astype(BF16)
        hgate_ref[h] = hgate[:, sl].astype(BF16)

    cn, sa, sb = cn_ref[...], sa_ref[...], sb_ref[...]

    def rope_nat(xc):
        return xc * cn + pltpu.roll(xc, LANES - ROT_HALF, 1) * sa + pltpu.roll(xc, ROT_HALF, 1) * sb

    kb = nat(2048)
    kr = jnp.concatenate([rope_nat(kb[:, c * LANES:(c + 1) * LANES]) for c in range(B_WIDTH // LANES)], axis=1)
    k_ref[...] = kr
    kbf_ref[...] = kr.astype(BF16)
    v_ref[...] = nat(2560)
    gb_ref[...] = _silu(nat(3072)).astype(BF16)

    kic = nat(3584, LANES)
    lane = lax.broadcasted_iota(I32, (tm, LANES), 1)
    inb = lane < IDX_DIM
    mu = jnp.sum(jnp.where(inb, kic, 0.0), axis=-1, keepdims=True) * (1.0 / IDX_DIM)
    d = jnp.where(inb, kic - mu, 0.0)
    var = jnp.sum(d * d, axis=-1, keepdims=True) * (1.0 / IDX_DIM)
    kin = d * lax.rsqrt(var + LN_EPS) * kng_ref[...] + knb_ref[...]
    kin = rope_nat(kin)[:, :IDX_DIM]
    ki_ref[...] = kin
    kibf_ref[...] = kin.astype(BF16)

    ct, st = ct_ref[...], st_ref[...]

    def tr(row, height):
        return _dot_nt(wt_ref[row:row + height, :], xb)

    def rope_tr(blk):
        x1, x2 = blk[0:ROT_HALF], blk[ROT_HALF:ROT]
        return jnp.concatenate([x1 * ct - x2 * st, x1 * st + x2 * ct, blk[ROT:]], axis=0)

    qbT = tr(0, B_WIDTH)
    for h in range(B_HEADS):
        blk = rope_tr(qbT[h * B_DH:(h + 1) * B_DH])
        qT_ref[h * B_DH:(h + 1) * B_DH, :] = (blk * (B_DH ** -0.5)).astype(BF16)
    vT_ref[...] = tr(B_WIDTH, B_WIDTH).astype(BF16)
    qiT = tr(2 * B_WIDTH, IDX_HEADS * IDX_DIM)
    for h in range(IDX_HEADS):
        blk = rope_tr(qiT[h * IDX_DIM:(h + 1) * IDX_DIM]).astype(BF16)
        for qb in range(tm // Q_BLOCK):
            qiT_ref[qb, :, h * Q_BLOCK:(h + 1) * Q_BLOCK] = blk[:, qb * Q_BLOCK:(qb + 1) * Q_BLOCK]
    wT_ref[...] = tr(2 * B_WIDTH + IDX_HEADS * IDX_DIM, IDX_HEADS) * (IDX_HEADS ** -0.5 * IDX_DIM ** -0.5)


def _rope_tables(pos):
    inv = ROPE_THETA ** (-jnp.arange(ROT_HALF, dtype=F32) / ROT_HALF)
    ang = pos.astype(F32)[:, None] * inv[None, :]
    cos, sin = jnp.cos(ang), jnp.sin(ang)
    p = pos.shape[0]
    one = jnp.ones((p, B_DH - ROT), F32)
    zero8 = jnp.zeros((p, ROT_HALF), F32)
    zero = jnp.zeros((p, B_DH - ROT), F32)
    cn = jnp.concatenate([cos, cos, one], axis=1)
    sa = jnp.concatenate([-sin, zero8, zero], axis=1)
    sb = jnp.concatenate([zero8, sin, zero], axis=1)
    tile2 = lambda a: jnp.concatenate([a, a], axis=1)
    return tile2(cn), tile2(sa), tile2(sb), cos.T, sin.T


def _project(x, pos, wn, wt, lb, kng, knb, tm):
    n, dm = x.shape
    p = pos.shape[0]
    nper = p // tm
    cn, sa, sb, ct, st = _rope_tables(pos)
    grid = (n // tm,)
    row = lambda i: (i, 0)
    full = lambda i: (0, 0)
    head = lambda i: (0, i, 0)
    per = lambda i: (i % nper, 0)
    perT = lambda i: (0, i % nper)
    colT = lambda i: (0, i)
    in_specs = [
        pl.BlockSpec((tm, dm), row),
        pl.BlockSpec((dm, NAT_WIDTH), full, pipeline_mode=pl.Buffered(1)),
        pl.BlockSpec((TR_ROWS, dm), full, pipeline_mode=pl.Buffered(1)),
        pl.BlockSpec((1, A_WIDTH), full),
        pl.BlockSpec((1, LANES), full),
        pl.BlockSpec((1, LANES), full),
        pl.BlockSpec((tm, LANES), per),
        pl.BlockSpec((tm, LANES), per),
        pl.BlockSpec((tm, LANES), per),
        pl.BlockSpec((ROT_HALF, tm), perT),
        pl.BlockSpec((ROT_HALF, tm), perT),
    ]
    hshape = lambda dt: jax.ShapeDtypeStruct((A_HEADS, n, A_DK), dt)
    out_shape = [
        hshape(BF16), hshape(F32), hshape(F32), hshape(BF16), hshape(BF16),
        jax.ShapeDtypeStruct((n, B_WIDTH), F32), jax.ShapeDtypeStruct((n, B_WIDTH), BF16),
        jax.ShapeDtypeStruct((n, B_WIDTH), F32), jax.ShapeDtypeStruct((n, B_WIDTH), BF16),
        jax.ShapeDtypeStruct((n, IDX_DIM), F32), jax.ShapeDtypeStruct((n, IDX_DIM), BF16),
        jax.ShapeDtypeStruct((B_WIDTH, n), BF16), jax.ShapeDtypeStruct((B_WIDTH, n), BF16),
        jax.ShapeDtypeStruct((n // Q_BLOCK, IDX_DIM, IDX_HEADS * Q_BLOCK), BF16),
        jax.ShapeDtypeStruct((IDX_HEADS, n), F32),
    ]
    hspec = pl.BlockSpec((A_HEADS, tm, A_DK), head)
    out_specs = [
        hspec, hspec, hspec, hspec, hspec,
        pl.BlockSpec((tm, B_WIDTH), row), pl.BlockSpec((tm, B_WIDTH), row),
        pl.BlockSpec((tm, B_WIDTH), row), pl.BlockSpec((tm, B_WIDTH), row),
        pl.BlockSpec((tm, IDX_DIM), row), pl.BlockSpec((tm, IDX_DIM), row),
        pl.BlockSpec((B_WIDTH, tm), colT), pl.BlockSpec((B_WIDTH, tm), colT),
        pl.BlockSpec((tm // Q_BLOCK, IDX_DIM, IDX_HEADS * Q_BLOCK), lambda i: (i, 0, 0)),
        pl.BlockSpec((IDX_HEADS, tm), colT),
    ]
    return pl.pallas_call(
        _proj_kernel, grid=grid, in_specs=in_specs, out_specs=out_specs, out_shape=out_shape,
        compiler_params=pltpu.CompilerParams(dimension_semantics=("parallel",), vmem_limit_bytes=VMEM_LIMIT),
        name="proj",
    )(x, wn, wt, lb, kng, knb, cn, sa, sb, ct, st)


def _hgrn_chunk(q, k, g, v, gate, ng, st, c):
    r = HGRN_SUB
    nsub = c // r
    if c == LANES:
        ri = lax.broadcasted_iota(I32, (c, c), 0)
        ci = lax.broadcasted_iota(I32, (c, c), 1)
        tri = jnp.where(ci <= ri, 1.0, 0.0).astype(BF16)
        g1 = g.astype(BF16)
        e1 = g - g1.astype(F32)
        g2 = e1.astype(BF16)
        g3 = (e1 - g2.astype(F32)).astype(BF16)
        cum = _dot(tri, g1) + _dot(tri, g2) + _dot(tri, g3)
    else:
        ri = lax.broadcasted_iota(I32, (c, A_DK), 0)
        cum = jnp.zeros((c, A_DK), F32)
        for s in range(c):
            cum = cum + jnp.where(ri >= s, g[s:s + 1], 0.0)
    stb = st.astype(BF16)
    o = _dot_nt((q * jnp.exp(cum)).astype(BF16), stb)

    rowi = lax.broadcasted_iota(I32, (r, A_DK), 0)
    lane = lax.broadcasted_iota(I32, (r, LANES), 1)
    prods = []
    for i in range(nsub):
        qs, cs = q[i * r:(i + 1) * r], cum[i * r:(i + 1) * r]
        for s in range(r):
            row = i * r + s
            p = (qs * k[row:row + 1]) * jnp.exp(cs - cum[row:row + 1])
            prods.append(jnp.where(rowi >= s, p, 0.0).astype(BF16))
    rsum = _dot(jnp.concatenate(prods, axis=0), jnp.ones((A_DK, LANES), BF16))
    sc_rows = []
    for i in range(nsub):
        sci = jnp.zeros((r, LANES), F32)
        for s in range(r):
            row = i * r + s
            sci = jnp.where(lane == row, rsum[row * r:(row + 1) * r], sci)
        if i > 0:
            ref = cum[i * r - 1:i * r]
            qt = (q[i * r:(i + 1) * r] * jnp.exp(cum[i * r:(i + 1) * r] - ref)).astype(BF16)
            kt = (k[:i * r] * jnp.exp(ref - cum[:i * r])).astype(BF16)
            kt = jnp.concatenate([kt, jnp.zeros((LANES - i * r, A_DK), BF16)], axis=0)
            sci = sci + _dot_nt(qt, kt)
        sc_rows.append(sci)
    sc = jnp.concatenate(sc_rows, axis=0).astype(BF16)
    vb = v.astype(BF16)
    last = cum[c - 1:c]
    kh = (k * jnp.exp(last - cum)).astype(BF16)
    if c < LANES:
        zpad = jnp.zeros((LANES - c, A_DK), BF16)
        vb = jnp.concatenate([vb, zpad], axis=0)
        kh = jnp.concatenate([kh, zpad], axis=0)
    o = o + _dot(sc, vb)
    st_new = st * jnp.exp(last) + _dot_tn(vb, kh)
    ms = jnp.mean(o * o, axis=-1, keepdims=True)
    return o * lax.rsqrt(ms + LN_EPS) * ng * gate, st_new


def _hgrn_prompt_kernel(q_ref, k_ref, g_ref, v_ref, gate_ref, ng_ref, o_ref, sf_ref, st_scr):
    c = HGRN_CHUNK
    nchunk = q_ref.shape[1] // c
    tb = pl.program_id(1)

    @pl.when(tb == 0)
    def _():
        st_scr[...] = jnp.zeros_like(st_scr)

    ng = ng_ref[...]

    def body(j, carry):
        h = j % A_HEADS
        sl = pl.ds(pl.multiple_of((j // A_HEADS) * c, c), c)
        o, st_new = _hgrn_chunk(q_ref[h, sl, :].astype(F32), k_ref[h, sl, :], g_ref[h, sl, :],
                                v_ref[h, sl, :].astype(F32), gate_ref[h, sl, :].astype(F32), ng, st_scr[h], c)
        st_scr[h] = st_new
        o_ref[h, sl, :] = o.astype(BF16)
        return carry

    lax.fori_loop(0, nchunk * A_HEADS, body, 0)

    @pl.when(tb == pl.num_programs(1) - 1)
    def _():
        for h in range(A_HEADS):
            sf_ref[0, h] = st_scr[h].T


def _hgrn_prompt(hq, hk, hg, hv, hgate, ng, batch, seq, tb):
    n = batch * seq
    nt = seq // tb
    blk = pl.BlockSpec((A_HEADS, tb, A_DK), lambda b, t: (0, b * nt + t, 0))
    return pl.pallas_call(
        _hgrn_prompt_kernel, grid=(batch, nt),
        in_specs=[blk, blk, blk, blk, blk, pl.BlockSpec((1, A_DV), lambda b, t: (0, 0))],
        out_specs=[blk, pl.BlockSpec((1, A_HEADS, A_DK, A_DV), lambda b, t: (b, 0, 0, 0))],
        out_shape=[jax.ShapeDtypeStruct((A_HEADS, n, A_DV), BF16),
                   jax.ShapeDtypeStruct((batch, A_HEADS, A_DK, A_DV), F32)],
        scratch_shapes=[pltpu.VMEM((A_HEADS, A_DV, A_DK), F32)],
        compiler_params=pltpu.CompilerParams(dimension_semantics=("parallel", "arbitrary"),
                                             vmem_limit_bytes=VMEM_LIMIT),
        name="hgrn_prompt",
    )(hq, hk, hg, hv, hgate, ng)


def _hgrn_sample_kernel(q_ref, k_ref, g_ref, v_ref, gate_ref, ng_ref, s0_ref, o_ref, sf_ref, *, t, nb):
    c = HGRN_SUB
    ng = ng_ref[...]
    zpad = jnp.zeros((c - t, A_DK), F32)
    pad = lambda a: jnp.concatenate([a, zpad], axis=0)
    for h in range(A_HEADS):
        q, k, g = q_ref[h].astype(F32), k_ref[h], g_ref[h]
        v, gate = v_ref[h].astype(F32), gate_ref[h].astype(F32)
        outs = []
        for b in range(nb):
            sl = slice(b * t, (b + 1) * t)
            o, st_new = _hgrn_chunk(pad(q[sl]), pad(k[sl]), pad(g[sl]), pad(v[sl]), pad(gate[sl]), ng,
                                    s0_ref[b, h].T, c)
            sf_ref[b, h] = st_new.T
            outs.append(o[:t])
        o_ref[h] = jnp.concatenate(outs, axis=0).astype(BF16)


def _hgrn_sample(hq, hk, hg, hv, hgate, ng, s0, t):
    bd = s0.shape[0]
    nb = 16 // t
    blk = pl.BlockSpec((A_HEADS, nb * t, A_DK), lambda i: (0, i, 0))
    sblk = pl.BlockSpec((nb, A_HEADS, A_DK, A_DV), lambda i: (i, 0, 0, 0))
    return pl.pallas_call(
        functools.partial(_hgrn_sample_kernel, t=t, nb=nb), grid=(bd // nb,),
        in_specs=[blk, blk, blk, blk, blk, pl.BlockSpec((1, A_DV), lambda i: (0, 0)), sblk],
        out_specs=[blk, sblk],
        out_shape=[jax.ShapeDtypeStruct((A_HEADS, bd * t, A_DV), BF16),
                   jax.ShapeDtypeStruct((bd, A_HEADS, A_DK, A_DV), F32)],
        compiler_params=pltpu.CompilerParams(dimension_semantics=("parallel",), vmem_limit_bytes=VMEM_LIMIT),
        name="hgrn_sample",
    )(hq, hk, hg, hv, hgate, ng, s0)


def _count_ge(keys_ref, ntiles, cand):
    def body(j, acc):
        tile = keys_ref[pl.ds(pl.multiple_of(j * K_TILE, K_TILE), K_TILE), :]
        hit = jnp.where(tile >= cand, 1, 0)
        return acc + jnp.sum(hit.reshape(K_TILE // 8, 8, Q_BLOCK), axis=0)
    acc = lax.fori_loop(0, ntiles, body, jnp.zeros((8, Q_BLOCK), I32))
    return jnp.sum(acc, axis=0, keepdims=True)


def _kth_largest(count_ge, topk):
    zero = jnp.zeros((1, Q_BLOCK), I32)
    c0 = count_ge(zero)
    ok = c0 >= topk
    thr = jnp.where(ok, 0, INT_MIN)
    nge = jnp.where(ok, c0, 0)

    def body(i, carry):
        thr, nge = carry
        cand = thr | jnp.left_shift(jnp.int32(1), 30 - i)
        cnt = count_ge(cand)
        ok = cnt >= topk
        return jnp.where(ok, cand, thr), jnp.where(ok, cnt, nge)

    return lax.fori_loop(0, 31, body, (thr, nge))


def _dsa_prompt_kernel(ki_ref, qiT_ref, wT_ref, k_ref, qT_ref, vT_ref, gb_ref, o_ref, keys_scr, oT_scr, *, topk):
    i = pl.program_id(1)
    ntiles = (i * Q_BLOCK + Q_BLOCK + K_TILE - 1) // K_TILE
    qpos = i * Q_BLOCK + lax.broadcasted_iota(I32, (K_TILE, Q_BLOCK), 1)
    krow = lax.broadcasted_iota(I32, (K_TILE, Q_BLOCK), 0)

    qi = qiT_ref[0]
    wT = wT_ref[...]

    def score_body(j, carry):
        ks = pl.multiple_of(j * K_TILE, K_TILE)
        x = _dot(ki_ref[pl.ds(ks, K_TILE), :], qi)
        sc = jnp.zeros((K_TILE, Q_BLOCK), F32)
        for h in range(IDX_HEADS):
            sc = sc + jnp.maximum(x[:, h * Q_BLOCK:(h + 1) * Q_BLOCK], 0.0) * wT[h:h + 1]
        sc = jnp.where(krow + ks <= qpos, sc, -jnp.inf)
        keys_scr[pl.ds(ks, K_TILE), :] = _sort_key(sc)
        return carry

    lax.fori_loop(0, ntiles, score_body, 0)

    count_ge = functools.partial(_count_ge, keys_scr, ntiles)
    thr, nge = _kth_largest(count_ge, topk)
    live = thr > KEY_NEG_INF
    thr = jnp.maximum(thr, KEY_NEG_INF + 1)
    cut_scr_val = jnp.full((1, Q_BLOCK), ntiles * K_TILE, I32)
    has_ties = jnp.max(jnp.where(live & (nge > topk), 1, 0)) > 0

    def tie_cut():
        need = topk - count_ge(thr + 1)

        def count_ties_below(pos):
            def body(j, acc):
                ks = pl.multiple_of(j * K_TILE, K_TILE)
                tile = keys_scr[pl.ds(ks, K_TILE), :]
                hit = jnp.where((tile == thr) & (krow + ks < pos), 1, 0)
                return acc + jnp.sum(hit.reshape(K_TILE // 8, 8, Q_BLOCK), axis=0)
            acc = lax.fori_loop(0, ntiles, body, jnp.zeros((8, Q_BLOCK), I32))
            return jnp.sum(acc, axis=0, keepdims=True)

        nbits = max(1, int(math.ceil(math.log2(keys_scr.shape[0] + 1))))

        def body(b, pos):
            cand = pos | jnp.left_shift(jnp.int32(1), nbits - 1 - b)
            return jnp.where(count_ties_below(cand) <= need, cand, pos)

        return lax.fori_loop(0, nbits, body, jnp.zeros((1, Q_BLOCK), I32))

    cut = lax.cond(has_ties, tie_cut, lambda: cut_scr_val)

    def sel_mask(ks):
        tile = keys_scr[pl.ds(ks, K_TILE), :]
        return (tile > thr) | ((tile == thr) & (krow + ks < cut))

    npair = B_HEADS // 2
    for p in range(npair):
        qp = qT_ref[p * LANES:(p + 1) * LANES, :]
        z = jnp.zeros((B_DH, Q_BLOCK), BF16)
        rhs = jnp.concatenate([jnp.concatenate([qp[:B_DH], z], axis=0),
                               jnp.concatenate([z, qp[B_DH:]], axis=0)], axis=1)

        def logits(j):
            ks = pl.multiple_of(j * K_TILE, K_TILE)
            l = _dot(k_ref[pl.ds(ks, K_TILE), p * LANES:(p + 1) * LANES], rhs)
            m = sel_mask(ks)
            return jnp.where(jnp.concatenate([m, m], axis=1), l, -jnp.inf), ks

        def max_body(j, mx):
            l, _ = logits(j)
            return jnp.maximum(mx, jnp.max(l.reshape(K_TILE // 8, 8, 2 * Q_BLOCK), axis=0))

        mx = lax.fori_loop(0, ntiles, max_body, jnp.full((8, 2 * Q_BLOCK), -jnp.inf, F32))
        mx = jnp.max(mx, axis=0, keepdims=True)

        def acc_body(j, carry):
            den, acc = carry
            l, ks = logits(j)
            e = jnp.exp(l - mx)
            den = den + jnp.sum(e.reshape(K_TILE // 8, 8, 2 * Q_BLOCK), axis=0)
            acc = acc + _dot(vT_ref[p * LANES:(p + 1) * LANES, pl.ds(ks, K_TILE)], e.astype(BF16))
            return den, acc

        den, acc = lax.fori_loop(0, ntiles, acc_body,
                                 (jnp.zeros((8, 2 * Q_BLOCK), F32), jnp.zeros((LANES, 2 * Q_BLOCK), F32)))
        den = jnp.sum(den, axis=0, keepdims=True)
        oT_scr[p * LANES:p * LANES + B_DH, :] = acc[:B_DH, :Q_BLOCK] / den[:, :Q_BLOCK]
        oT_scr[p * LANES + B_DH:(p + 1) * LANES, :] = acc[B_DH:, Q_BLOCK:] / den[:, Q_BLOCK:]

    o_ref[...] = (oT_scr[...].T * gb_ref[...].astype(F32)).astype(BF16)


def _dsa_prompt(kibf, qiT2, wT, kbf, qT, vT, gbs, batch, seq):
    n = batch * seq
    nq = seq // Q_BLOCK
    topk = min(TOPK_MAX, seq // 4)
    lpad = -(-seq // K_TILE) * K_TILE
    assert lpad == seq and topk <= K_TILE
    return pl.pallas_call(
        functools.partial(_dsa_prompt_kernel, topk=topk), grid=(batch, nq),
        in_specs=[
            pl.BlockSpec((seq, IDX_DIM), lambda b, i: (b, 0)),
            pl.BlockSpec((1, IDX_DIM, IDX_HEADS * Q_BLOCK), lambda b, i: (b * nq + i, 0, 0)),
            pl.BlockSpec((IDX_HEADS, Q_BLOCK), lambda b, i: (0, b * nq + i)),
            pl.BlockSpec((seq, B_WIDTH), lambda b, i: (b, 0)),
            pl.BlockSpec((B_WIDTH, Q_BLOCK), lambda b, i: (0, b * nq + i)),
            pl.BlockSpec((B_WIDTH, seq), lambda b, i: (0, b)),
            pl.BlockSpec((Q_BLOCK, B_WIDTH), lambda b, i: (b * nq + i, 0)),
        ],
        out_specs=pl.BlockSpec((Q_BLOCK, B_WIDTH), lambda b, i: (b * nq + i, 0)),
        out_shape=jax.ShapeDtypeStruct((n, B_WIDTH), BF16),
        scratch_shapes=[pltpu.VMEM((seq, Q_BLOCK), I32), pltpu.VMEM((B_WIDTH, Q_BLOCK), F32)],
        compiler_params=pltpu.CompilerParams(dimension_semantics=("parallel", "arbitrary"),
                                             vmem_limit_bytes=VMEM_LIMIT),
        name="dsa_prompt",
    )(kibf, qiT2, wT, kbf, qT, vT, gbs)


def _dsa_sample_kernel(pt_ref, qi_ref, wrep_ref, qbd_ref, kin_ref, kn_ref, vn_ref, gb_ref,
                       kidx_ref, ck_ref, cv_ref, o_ref, keys_scr, m_scr, l_scr, acc_scr, thr_scr, cut_scr,
                       *, t, npages, topk):
    ph = pl.program_id(1)
    p = pl.program_id(2)
    page = kidx_ref.shape[2]
    nrow = B_HEADS * t
    total = (npages + 1) * page
    lane = lax.broadcasted_iota(I32, (t, page), 1)
    qrow = lax.broadcasted_iota(I32, (t, page), 0)

    def scores(kidx):
        x = _dot_nt(qi_ref[0], kidx)
        xw = jnp.maximum(x, 0.0) * wrep_ref[0]
        sc = xw[0:t]
        for h in range(1, IDX_HEADS):
            sc = sc + xw[h * t:(h + 1) * t]
        return sc

    @pl.when((ph == 0) & (p < npages))
    def _():
        sc = scores(kidx_ref[0, 0].astype(BF16))
        keys_scr[:, pl.ds(pl.multiple_of(p * page, page), page)] = _sort_key(sc)

    @pl.when((ph == 0) & (p == npages))
    def _():
        sc = scores(kin_ref[0])
        key = jnp.where(lane <= qrow, _sort_key(sc), jnp.where(lane < t, KEY_NEG_INF, INT_MIN))
        keys_scr[:, npages * page:] = key

        def count_ge(cand):
            hit = jnp.where(keys_scr[...] >= cand, 1, 0)
            part = hit[:, 0:page]
            for c in range(1, npages + 1):
                part = part + hit[:, c * page:(c + 1) * page]
            return jnp.sum(part, axis=1, keepdims=True)

        zero = jnp.zeros((t, 1), I32)
        c0 = count_ge(zero)
        ok = c0 >= topk
        thr = jnp.where(ok, 0, INT_MIN)
        nge = jnp.where(ok, c0, 0)

        def body(i, carry):
            thr, nge = carry
            cand = thr | jnp.left_shift(jnp.int32(1), 30 - i)
            cnt = count_ge(cand)
            ok = cnt >= topk
            return jnp.where(ok, cand, thr), jnp.where(ok, cnt, nge)

        thr, nge = lax.fori_loop(0, 31, body, (thr, nge))
        live = thr > KEY_NEG_INF
        thr = jnp.maximum(thr, KEY_NEG_INF + 1)
        has_ties = jnp.max(jnp.where(live & (nge > topk), 1, 0)) > 0
        pos_all = lax.broadcasted_iota(I32, (t, total), 1)

        def tie_cut():
            need = topk - count_ge(thr + 1)
            nbits = max(1, int(math.ceil(math.log2(total + 1))))

            def body(b, pos):
                cand = pos | jnp.left_shift(jnp.int32(1), nbits - 1 - b)
                hit = jnp.where((keys_scr[...] == thr) & (pos_all < cand), 1, 0)
                cnt = jnp.sum(hit, axis=1, keepdims=True)
                return jnp.where(cnt <= need, cand, pos)

            return lax.fori_loop(0, nbits, body, jnp.zeros((t, 1), I32))

        cut = lax.cond(has_ties, tie_cut, lambda: jnp.full((t, 1), total, I32))
        thr_scr[...] = jnp.broadcast_to(thr, (t, page))
        cut_scr[...] = jnp.broadcast_to(cut, (t, page))
        m_scr[...] = jnp.full_like(m_scr, -jnp.inf)
        l_scr[...] = jnp.zeros_like(l_scr)
        acc_scr[...] = jnp.zeros_like(acc_scr)

    def attend(kpage, vpage):
        ks = pl.multiple_of(p * page, page)
        tile = keys_scr[:, pl.ds(ks, page)]
        thr = thr_scr[...]
        sel = (tile > thr) | ((tile == thr) & (lane + ks < cut_scr[...]))
        sel = jnp.concatenate([sel.astype(I32)] * B_HEADS, axis=0) > 0
        l = jnp.where(sel, _dot_nt(qbd_ref[0], kpage), -jnp.inf)
        m_old = m_scr[...]
        m_new = jnp.maximum(m_old, jnp.max(l, axis=1, keepdims=True))
        m_safe = jnp.where(m_new == -jnp.inf, 0.0, m_new)
        alpha = jnp.exp(m_old - m_safe)
        e = jnp.exp(l - m_safe)
        l_scr[...] = alpha * l_scr[...] + jnp.sum(e, axis=1, keepdims=True)
        acc_scr[...] = alpha * acc_scr[...] + _dot(e.astype(BF16), vpage)
        m_scr[...] = m_new

    @pl.when((ph == 1) & (p < npages))
    def _():
        attend(ck_ref[0, 0].astype(BF16), cv_ref[0, 0].astype(BF16))

    @pl.when((ph == 1) & (p == npages))
    def _():
        attend(kn_ref[0], vn_ref[0])
        o = acc_scr[...] / l_scr[...]
        col = lax.broadcasted_iota(I32, (t, B_WIDTH), 1)
        out = jnp.zeros((t, B_WIDTH), F32)
        for h in range(B_HEADS):
            out = jnp.where((col >= h * B_DH) & (col < (h + 1) * B_DH), o[h * t:(h + 1) * t], out)
        o_ref[0] = (out * gb_ref[0].astype(F32)).astype(BF16)


def _dsa_sample(page_table, qi_rows, wrep, qbd, kin_pad, kn_pad, vn_pad, gbs, cache_kidx, cache_k, cache_v, t):
    bd, npages = page_table.shape
    page = cache_kidx.shape[2]
    total = npages * page + t
    topk = min(TOPK_MAX, total // 4)
    n_pool = cache_k.shape[0]
    ck = cache_k.reshape(n_pool, cache_k.shape[1], page, B_WIDTH)
    cv = cache_v.reshape(n_pool, cache_v.shape[1], page, B_WIDTH)
    last = npages - 1
    per_b = lambda b, ph, p, pt: (b, 0, 0)
    grid_spec = pltpu.PrefetchScalarGridSpec(
        num_scalar_prefetch=1, grid=(bd, 2, npages + 1),
        in_specs=[
            pl.BlockSpec((1, IDX_HEADS * t, IDX_DIM), per_b),
            pl.BlockSpec((1, IDX_HEADS * t, page), per_b),
            pl.BlockSpec((1, B_HEADS * t, B_WIDTH), per_b),
            pl.BlockSpec((1, page, IDX_DIM), per_b),
            pl.BlockSpec((1, page, B_WIDTH), per_b),
            pl.BlockSpec((1, page, B_WIDTH), per_b),
            pl.BlockSpec((1, t, B_WIDTH), per_b),
            pl.BlockSpec((1, 1, page, IDX_DIM),
                         lambda b, ph, p, pt: (pt[b, jnp.where(ph == 0, jnp.minimum(p, last), last)], 0, 0, 0)),
            pl.BlockSpec((1, 1, page, B_WIDTH),
                         lambda b, ph, p, pt: (pt[b, jnp.where(ph == 0, 0, jnp.minimum(p, last))], 0, 0, 0)),
            pl.BlockSpec((1, 1, page, B_WIDTH),
                         lambda b, ph, p, pt: (pt[b, jnp.where(ph == 0, 0, jnp.minimum(p, last))], 0, 0, 0)),
        ],
        out_specs=pl.BlockSpec((1, t, B_WIDTH), per_b),
        scratch_shapes=[
            pltpu.VMEM((t, (npages + 1) * page), I32),
            pltpu.VMEM((B_HEADS * t, 1), F32), pltpu.VMEM((B_HEADS * t, 1), F32),
            pltpu.VMEM((B_HEADS * t, B_WIDTH), F32),
            pltpu.VMEM((t, page), I32), pltpu.VMEM((t, page), I32),
        ],
    )
    return pl.pallas_call(
        functools.partial(_dsa_sample_kernel, t=t, npages=npages, topk=topk),
        grid_spec=grid_spec,
        out_shape=jax.ShapeDtypeStruct((bd, t, B_WIDTH), BF16),
        compiler_params=pltpu.CompilerParams(dimension_semantics=("parallel", "arbitrary", "arbitrary"),
                                             vmem_limit_bytes=VMEM_LIMIT),
        name="dsa_sample",
    )(page_table, qi_rows, wrep, qbd, kin_pad, kn_pad, vn_pad, gbs, cache_kidx, ck, cv)


def _merge_kernel(x_ref, ma_ref, mb_ref, w_ref, g_ref, b_ref, y_ref, *, alpha):
    mix = jnp.concatenate([ma_ref[h] for h in range(A_HEADS)] + [mb_ref[...]], axis=1)
    y = alpha * x_ref[...] + _dot(mix, w_ref[...])
    mu = jnp.mean(y, axis=-1, keepdims=True)
    d = y - mu
    var = jnp.mean(d * d, axis=-1, keepdims=True)
    y_ref[...] = d * lax.rsqrt(var + LN_EPS) * g_ref[...] + b_ref[...]


def _merge(x, mix_a, mix_b, w_out, ln_g, ln_b, alpha, tm):
    n, dm = x.shape
    return pl.pallas_call(
        functools.partial(_merge_kernel, alpha=alpha), grid=(n // tm,),
        in_specs=[
            pl.BlockSpec((tm, dm), lambda i: (i, 0)),
            pl.BlockSpec((A_HEADS, tm, A_DV), lambda i: (0, i, 0)),
            pl.BlockSpec((tm, B_WIDTH), lambda i: (i, 0)),
            pl.BlockSpec((A_WIDTH + B_WIDTH, dm), lambda i: (0, 0)),
            pl.BlockSpec((1, dm), lambda i: (0, 0)),
            pl.BlockSpec((1, dm), lambda i: (0, 0)),
        ],
        out_specs=pl.BlockSpec((tm, dm), lambda i: (i, 0)),
        out_shape=jax.ShapeDtypeStruct((n, dm), F32),
        compiler_params=pltpu.CompilerParams(dimension_semantics=("parallel",), vmem_limit_bytes=VMEM_LIMIT),
        name="merge",
    )(x, mix_a, mix_b, w_out, ln_g, ln_b)


def _split_weights(w_in_l):
    offs = np.cumsum([0, 512, 512, 512, 512, 512, 512, 512, 512, IDX_HEADS * IDX_DIM, IDX_DIM, IDX_HEADS])
    col = lambda i: w_in_l[:, offs[i]:offs[i + 1]]
    qa, fa, ia, ga, qb, kb, vb, gb, qi, ki, wi = (col(i) for i in range(11))
    pad = jnp.zeros((w_in_l.shape[0], LANES - IDX_DIM), w_in_l.dtype)
    wn = jnp.concatenate([qa, fa, ia, ga, kb, vb, gb, ki, pad], axis=1).astype(BF16)
    wt = jnp.concatenate([qb, vb, qi, wi], axis=1).T.astype(BF16)
    return wn, wt


def _layer(xp, xs, cache_k, cache_v, cache_kidx, s0_sample, page_table, w_in_l, lb_l, norm_g_l, kn_g_l, kn_b_l,
           w_out_l, ln_g_l, ln_b_l, alpha):
    b, l, dm = xp.shape
    bd, t, _ = xs.shape
    npages, page = page_table.shape[1], cache_k.shape[2]
    past = npages * page
    wn, wt = _split_weights(w_in_l)
    lb = lb_l.reshape(1, A_WIDTH)
    padl = lambda a: jnp.concatenate([a, jnp.zeros((LANES - IDX_DIM,), a.dtype)]).reshape(1, LANES)
    kng, knb = padl(kn_g_l), padl(kn_b_l)
    ng = norm_g_l.reshape(1, A_DV)
    w_out_b = w_out_l.astype(BF16)
    lng, lnb = ln_g_l.reshape(1, dm), ln_b_l.reshape(1, dm)

    tm = 256
    xp2 = xp.reshape(b * l, dm)
    (hq, hk, hg, hv, hgate, k_p, kbf, v_p, gbs, ki_p, kibf, qT, vT, qiT2, wT) = _project(
        xp2, jnp.arange(l, dtype=I32), wn, wt, lb, kng, knb, tm)
    mix_a, s_p = _hgrn_prompt(hq, hk, hg, hv, hgate, ng, b, l, min(l, 512))
    mix_b = _dsa_prompt(kibf, qiT2, wT, kbf, qT, vT, gbs, b, l)
    y_p = _merge(xp2, mix_a, mix_b, w_out_b, lng, lnb, alpha, 512).reshape(b, l, dm)

    ns = bd * t
    xs2 = xs.reshape(ns, dm)
    pos_s = past + (jnp.arange(ns, dtype=I32) % t)
    (hq, hk, hg, hv, hgate, k_s, kbf, v_s, gbs, ki_s, kibf, qT, vT, qiT2, wT) = _project(
        xs2, pos_s, wn, wt, lb, kng, knb, ns)
    mix_a, s_s = _hgrn_sample(hq, hk, hg, hv, hgate, ng, s0_sample, t)
    qi_nat = qiT2.reshape(ns // Q_BLOCK, IDX_DIM, IDX_HEADS, Q_BLOCK).transpose(0, 3, 2, 1)
    qi_rows = qi_nat.reshape(bd, t, IDX_HEADS, IDX_DIM).transpose(0, 2, 1, 3).reshape(bd, IDX_HEADS * t, IDX_DIM)
    w_rows = wT.T.reshape(bd, t, IDX_HEADS).transpose(0, 2, 1).reshape(bd, IDX_HEADS * t, 1)
    wrep = jnp.broadcast_to(w_rows, (bd, IDX_HEADS * t, page))
    q_nat = qT.T.reshape(bd, 1, t, B_HEADS, B_DH)
    eye = jnp.eye(B_HEADS, dtype=BF16).reshape(1, B_HEADS, 1, B_HEADS, 1)
    qbd = (q_nat * eye).reshape(bd, B_HEADS * t, B_WIDTH)
    padrows = lambda a: jnp.concatenate(
        [a.reshape(bd, t, -1), jnp.zeros((bd, page - t, a.shape[-1]), a.dtype)], axis=1)
    mix_b = _dsa_sample(page_table, qi_rows, wrep, qbd, padrows(kibf), padrows(kbf), padrows(v_s.astype(BF16)),
                        gbs.reshape(bd, t, B_WIDTH), cache_kidx, cache_k, cache_v, t).reshape(ns, B_WIDTH)
    y_s = _merge(xs2, mix_a, mix_b, w_out_b, lng, lnb, alpha, ns).reshape(bd, t, dm)

    return (y_p, y_s, k_p, v_p, ki_p, s_p, k_s, v_s, ki_s, s_s)


def kernel(x_prompt, x_sample, cache_k, cache_v, cache_kidx, state_hgrn, page_table, w_in, hgrn_lb_logits,
           hgrn_norm_g, idx_norm_g, idx_norm_b, w_out, ln_g, ln_b):
    depth = w_in.shape[0]
    assert depth == 1, "one layer per step"
    b, l, _ = x_prompt.shape
    bd, t, _ = x_sample.shape
    page = cache_k.shape[2]
    alpha = (2.0 * depth) ** 0.25
    lbs = jnp.cumsum(jax.nn.softmax(hgrn_lb_logits.astype(F32), axis=0), axis=0)[:depth]
    (y_p, y_s, k_p, v_p, ki_p, s_p, k_s, v_s, ki_s, s_s) = _layer(
        x_prompt, x_sample, cache_k, cache_v, cache_kidx, state_hgrn[0], page_table, w_in[0], lbs[0],
        hgrn_norm_g[0], idx_norm_g[0], idx_norm_b[0], w_out[0], ln_g[0], ln_b[0], alpha)
    return (
        y_p, y_s,
        k_p.reshape(b, l // page, 1, page, B_HEADS, B_DH),
        v_p.reshape(b, l // page, 1, page, B_HEADS, B_DH),
        ki_p.reshape(b, l // page, 1, page, IDX_DIM),
        s_p[None],
        k_s.reshape(bd, 1, t, B_HEADS, B_DH),
        v_s.reshape(bd, 1, t, B_HEADS, B_DH),
        ki_s.reshape(bd, 1, t, IDX_DIM),
        s_s[None],
    )
```

```python
import functools
import math

import jax
import jax.numpy as jnp
import numpy as np
from jax import lax
from jax.experimental import pallas as pl
from jax.experimental.pallas import tpu as pltpu

F32 = jnp.float32
BF16 = jnp.bfloat16
I32 = jnp.int32

A_HEADS = 4
A_DK = 128
A_DV = 128
B_HEADS = 8
B_DH = 64
IDX_HEADS = 16
IDX_DIM = 64
TOPK_MAX = 256
ROPE_THETA = 500000.0
ROT = 16
ROT_HALF = ROT // 2
LN_EPS = 1e-5
Q_BLOCK = 128
K_TILE = 256
HGRN_CHUNK = 128
HGRN_SUB = 16
LANES = 128
VMEM_LIMIT = 56 * 1024 * 1024
SAMPLE_DMA_DEPTH = 8

A_WIDTH = A_HEADS * A_DV
B_WIDTH = B_HEADS * B_DH
NAT_WIDTH = 6 * 512 + LANES
TR_ROWS = 3 * 512 + IDX_HEADS * IDX_DIM + IDX_HEADS + IDX_DIM
INT_MIN = -(2 ** 31)
KEY_NEG_INF = -0x7F800000


def _dot(a, b):
    return jnp.dot(a, b, preferred_element_type=F32)


def _dot_nt(a, b):
    return lax.dot_general(a, b, (((1,), (1,)), ((), ())), preferred_element_type=F32)


def _dot_tn(a, b):
    return lax.dot_general(a, b, (((0,), (0,)), ((), ())), preferred_element_type=F32)


def _silu(x):
    return x * jax.nn.sigmoid(x)


def _sort_key(x):
    b = pltpu.bitcast(x, I32)
    m = b >> 31
    return ((b & 0x7FFFFFFF) ^ m) - m


def _proj_kernel(x_ref, wn_ref, wt_ref, lb_ref, kng_ref, knb_ref, kngc_ref, knbc_ref, cn_ref, sa_ref, sb_ref,
                 ct_ref, st_ref,
                 hq_ref, hk_ref, hg_ref, hv_ref, hgate_ref, kbf_ref, gb_ref, kibf_ref,
                 qT_ref, vT_ref, qiT_ref, wT_ref, kTp_ref, vTp_ref, kiTp_ref):
    tm = x_ref.shape[0]
    xb = x_ref[...].astype(BF16)

    def nat(col, width=512):
        return _dot(xb, wn_ref[:, col:col + width])

    lb = lb_ref[...]
    qa = nat(0)
    hq = _silu(qa)
    fa = nat(512)
    hg = jnp.log(lb + (1.0 - lb) * jax.nn.sigmoid(fa))
    hk = (1.0 - lb) * jax.nn.sigmoid(-fa)
    ia = nat(1024)
    hgate = _silu(nat(1536))
    for h in range(A_HEADS):
        sl = slice(h * A_DK, (h + 1) * A_DK)
        hq_ref[h] = hq[:, sl].astype(BF16)
        hk_ref[h] = hk[:, sl]
        hg_ref[h] = hg[:, sl]
        hv_ref[h] = ia[:, sl].astype(BF16)
        hgate_ref[h] = hgate[:, sl].astype(BF16)

    cn, sa, sb = cn_ref[...], sa_ref[...], sb_ref[...]

    def rope_nat(xc):
        return xc * cn + pltpu.roll(xc, LANES - ROT_HALF, 1) * sa + pltpu.roll(xc, ROT_HALF, 1) * sb

    kb = nat(2048)
    kbf_ref[...] = jnp.concatenate(
        [rope_nat(kb[:, c * LANES:(c + 1) * LANES]) for c in range(B_WIDTH // LANES)], axis=1).astype(BF16)
    gb_ref[...] = _silu(nat(2560)).astype(BF16)

    kic = nat(3072, LANES)
    lane = lax.broadcasted_iota(I32, (tm, LANES), 1)
    inb = lane < IDX_DIM
    mu = jnp.sum(jnp.where(inb, kic, 0.0), axis=-1, keepdims=True) * (1.0 / IDX_DIM)
    d = jnp.where(inb, kic - mu, 0.0)
    var = jnp.sum(d * d, axis=-1, keepdims=True) * (1.0 / IDX_DIM)
    kin = d * lax.rsqrt(var + LN_EPS) * kng_ref[...] + knb_ref[...]
    kibf_ref[...] = rope_nat(kin)[:, :IDX_DIM].astype(BF16)

    ct, st = ct_ref[...], st_ref[...]
    npage = tm // Q_BLOCK

    def tr(row, height):
        return _dot_nt(wt_ref[row:row + height, :], xb)

    def rope_tr(blk):
        x1, x2 = blk[0:ROT_HALF], blk[ROT_HALF:ROT]
        return jnp.concatenate([x1 * ct - x2 * st, x1 * st + x2 * ct, blk[ROT:]], axis=0)

    qbT = tr(0, B_WIDTH)
    for h in range(B_HEADS):
        blk = rope_tr(qbT[h * B_DH:(h + 1) * B_DH])
        qT_ref[h * B_DH:(h + 1) * B_DH, :] = (blk * (B_DH ** -0.5)).astype(BF16)
    kbT = tr(B_WIDTH, B_WIDTH)
    for h in range(B_HEADS):
        blk = rope_tr(kbT[h * B_DH:(h + 1) * B_DH])
        for pg in range(npage):
            kTp_ref[pg, h * B_DH:(h + 1) * B_DH, :] = blk[:, pg * Q_BLOCK:(pg + 1) * Q_BLOCK]
    vbT = tr(2 * B_WIDTH, B_WIDTH)
    vT_ref[...] = vbT.astype(BF16)
    for pg in range(npage):
        vTp_ref[pg] = vbT[:, pg * Q_BLOCK:(pg + 1) * Q_BLOCK]
    row = 3 * B_WIDTH
    qiT = tr(row, IDX_HEADS * IDX_DIM)
    for h in range(IDX_HEADS):
        blk = rope_tr(qiT[h * IDX_DIM:(h + 1) * IDX_DIM]).astype(BF16)
        for pg in range(npage):
            qiT_ref[pg, :, h * Q_BLOCK:(h + 1) * Q_BLOCK] = blk[:, pg * Q_BLOCK:(pg + 1) * Q_BLOCK]
    row += IDX_HEADS * IDX_DIM
    wT_ref[...] = tr(row, IDX_HEADS) * (IDX_HEADS ** -0.5 * IDX_DIM ** -0.5)
    row += IDX_HEADS
    kiT = tr(row, IDX_DIM)
    muT = jnp.mean(kiT, axis=0, keepdims=True)
    dT = kiT - muT
    varT = jnp.mean(dT * dT, axis=0, keepdims=True)
    kiT = rope_tr(dT * lax.rsqrt(varT + LN_EPS) * kngc_ref[...] + knbc_ref[...])
    for pg in range(npage):
        kiTp_ref[pg] = kiT[:, pg * Q_BLOCK:(pg + 1) * Q_BLOCK]


def _rope_tables(pos):
    inv = ROPE_THETA ** (-jnp.arange(ROT_HALF, dtype=F32) / ROT_HALF)
    ang = pos.astype(F32)[:, None] * inv[None, :]
    cos, sin = jnp.cos(ang), jnp.sin(ang)
    p = pos.shape[0]
    one = jnp.ones((p, B_DH - ROT), F32)
    zero8 = jnp.zeros((p, ROT_HALF), F32)
    zero = jnp.zeros((p, B_DH - ROT), F32)
    cn = jnp.concatenate([cos, cos, one], axis=1)
    sa = jnp.concatenate([-sin, zero8, zero], axis=1)
    sb = jnp.concatenate([zero8, sin, zero], axis=1)
    tile2 = lambda a: jnp.concatenate([a, a], axis=1)
    return tile2(cn), tile2(sa), tile2(sb), cos.T, sin.T


def _project(x, pos, wn, wt, lb, kn_g, kn_b, tm):
    n, dm = x.shape
    p = pos.shape[0]
    nper = p // tm
    cn, sa, sb, ct, st = _rope_tables(pos)
    padl = lambda a: jnp.concatenate([a, jnp.zeros((LANES - IDX_DIM,), a.dtype)]).reshape(1, LANES)
    grid = (n // tm,)
    row = lambda i: (i, 0)
    full = lambda i: (0, 0)
    head = lambda i: (0, i, 0)
    page = lambda i: (i, 0, 0)
    per = lambda i: (i % nper, 0)
    perT = lambda i: (0, i % nper)
    colT = lambda i: (0, i)
    in_specs = [
        pl.BlockSpec((tm, dm), row),
        pl.BlockSpec((dm, NAT_WIDTH), full, pipeline_mode=pl.Buffered(1)),
        pl.BlockSpec((TR_ROWS, dm), full, pipeline_mode=pl.Buffered(1)),
        pl.BlockSpec((1, A_WIDTH), full),
        pl.BlockSpec((1, LANES), full),
        pl.BlockSpec((1, LANES), full),
        pl.BlockSpec((IDX_DIM, 1), full),
        pl.BlockSpec((IDX_DIM, 1), full),
        pl.BlockSpec((tm, LANES), per),
        pl.BlockSpec((tm, LANES), per),
        pl.BlockSpec((tm, LANES), per),
        pl.BlockSpec((ROT_HALF, tm), perT),
        pl.BlockSpec((ROT_HALF, tm), perT),
    ]
    npage = tm // Q_BLOCK
    hshape = lambda dt: jax.ShapeDtypeStruct((A_HEADS, n, A_DK), dt)
    out_shape = [
        hshape(BF16), hshape(F32), hshape(F32), hshape(BF16), hshape(BF16),
        jax.ShapeDtypeStruct((n, B_WIDTH), BF16), jax.ShapeDtypeStruct((n, B_WIDTH), BF16),
        jax.ShapeDtypeStruct((n, IDX_DIM), BF16),
        jax.ShapeDtypeStruct((B_WIDTH, n), BF16), jax.ShapeDtypeStruct((B_WIDTH, n), BF16),
        jax.ShapeDtypeStruct((n // Q_BLOCK, IDX_DIM, IDX_HEADS * Q_BLOCK), BF16),
        jax.ShapeDtypeStruct((IDX_HEADS, n), F32),
        jax.ShapeDtypeStruct((n // Q_BLOCK, B_WIDTH, Q_BLOCK), F32),
        jax.ShapeDtypeStruct((n // Q_BLOCK, B_WIDTH, Q_BLOCK), F32),
        jax.ShapeDtypeStruct((n // Q_BLOCK, IDX_DIM, Q_BLOCK), F32),
    ]
    hspec = pl.BlockSpec((A_HEADS, tm, A_DK), head)
    out_specs = [
        hspec, hspec, hspec, hspec, hspec,
        pl.BlockSpec((tm, B_WIDTH), row), pl.BlockSpec((tm, B_WIDTH), row),
        pl.BlockSpec((tm, IDX_DIM), row),
        pl.BlockSpec((B_WIDTH, tm), colT), pl.BlockSpec((B_WIDTH, tm), colT),
        pl.BlockSpec((npage, IDX_DIM, IDX_HEADS * Q_BLOCK), page),
        pl.BlockSpec((IDX_HEADS, tm), colT),
        pl.BlockSpec((npage, B_WIDTH, Q_BLOCK), page),
        pl.BlockSpec((npage, B_WIDTH, Q_BLOCK), page),
        pl.BlockSpec((npage, IDX_DIM, Q_BLOCK), page),
    ]
    return pl.pallas_call(
        _proj_kernel, grid=grid, in_specs=in_specs, out_specs=out_specs, out_shape=out_shape,
        compiler_params=pltpu.CompilerParams(dimension_semantics=("parallel",), vmem_limit_bytes=VMEM_LIMIT),
        name="proj",
    )(x, wn, wt, lb, padl(kn_g), padl(kn_b), kn_g.reshape(IDX_DIM, 1), kn_b.reshape(IDX_DIM, 1),
      cn, sa, sb, ct, st)


def _hgrn_chunk(q, k, g, v, gate, ng, st, c):
    r = HGRN_SUB
    nsub = c // r
    if c == LANES:
        ri = lax.broadcasted_iota(I32, (c, c), 0)
        ci = lax.broadcasted_iota(I32, (c, c), 1)
        tri = jnp.where(ci <= ri, 1.0, 0.0).astype(BF16)
        g1 = g.astype(BF16)
        e1 = g - g1.astype(F32)
        g2 = e1.astype(BF16)
        g3 = (e1 - g2.astype(F32)).astype(BF16)
        cum = _dot(tri, g1) + _dot(tri, g2) + _dot(tri, g3)
    else:
        ri = lax.broadcasted_iota(I32, (c, A_DK), 0)
        cum = jnp.zeros((c, A_DK), F32)
        for s in range(c):
            cum = cum + jnp.where(ri >= s, g[s:s + 1], 0.0)
    stb = st.astype(BF16)
    o = _dot_nt((q * jnp.exp(cum)).astype(BF16), stb)

    rowi = lax.broadcasted_iota(I32, (r, A_DK), 0)
    lane = lax.broadcasted_iota(I32, (r, LANES), 1)
    prods = []
    for i in range(nsub):
        qs, cs = q[i * r:(i + 1) * r], cum[i * r:(i + 1) * r]
        for s in range(r):
            row = i * r + s
            p = (qs * k[row:row + 1]) * jnp.exp(cs - cum[row:row + 1])
            prods.append(jnp.where(rowi >= s, p, 0.0).astype(BF16))
    rsum = _dot(jnp.concatenate(prods, axis=0), jnp.ones((A_DK, LANES), BF16))
    sc_rows = []
    for i in range(nsub):
        sci = jnp.zeros((r, LANES), F32)
        for s in range(r):
            row = i * r + s
            sci = jnp.where(lane == row, rsum[row * r:(row + 1) * r], sci)
        if i > 0:
            ref = cum[i * r - 1:i * r]
            qt = (q[i * r:(i + 1) * r] * jnp.exp(cum[i * r:(i + 1) * r] - ref)).astype(BF16)
            kt = (k[:i * r] * jnp.exp(ref - cum[:i * r])).astype(BF16)
            kt = jnp.concatenate([kt, jnp.zeros((LANES - i * r, A_DK), BF16)], axis=0)
            sci = sci + _dot_nt(qt, kt)
        sc_rows.append(sci)
    sc = jnp.concatenate(sc_rows, axis=0).astype(BF16)
    vb = v.astype(BF16)
    last = cum[c - 1:c]
    kh = (k * jnp.exp(last - cum)).astype(BF16)
    if c < LANES:
        zpad = jnp.zeros((LANES - c, A_DK), BF16)
        vb = jnp.concatenate([vb, zpad], axis=0)
        kh = jnp.concatenate([kh, zpad], axis=0)
    o = o + _dot(sc, vb)
    st_new = st * jnp.exp(last) + _dot_tn(vb, kh)
    ms = jnp.mean(o * o, axis=-1, keepdims=True)
    return o * lax.rsqrt(ms + LN_EPS) * ng * gate, st_new


def _hgrn_prompt_kernel(q_ref, k_ref, g_ref, v_ref, gate_ref, ng_ref, o_ref, sf_ref, st_scr):
    c = HGRN_CHUNK
    nchunk = q_ref.shape[1] // c
    tb = pl.program_id(1)

    @pl.when(tb == 0)
    def _():
        st_scr[...] = jnp.zeros_like(st_scr)

    ng = ng_ref[...]

    def body(j, carry):
        h = j % A_HEADS
        sl = pl.ds(pl.multiple_of((j // A_HEADS) * c, c), c)
        o, st_new = _hgrn_chunk(q_ref[h, sl, :].astype(F32), k_ref[h, sl, :], g_ref[h, sl, :],
                                v_ref[h, sl, :].astype(F32), gate_ref[h, sl, :].astype(F32), ng, st_scr[h], c)
        st_scr[h] = st_new
        o_ref[h, sl, :] = o.astype(BF16)
        return carry

    lax.fori_loop(0, nchunk * A_HEADS, body, 0)

    @pl.when(tb == pl.num_programs(1) - 1)
    def _():
        for h in range(A_HEADS):
            sf_ref[0, h] = st_scr[h].T


def _hgrn_prompt(hq, hk, hg, hv, hgate, ng, batch, seq, tb):
    n = batch * seq
    nt = seq // tb
    blk = pl.BlockSpec((A_HEADS, tb, A_DK), lambda b, t: (0, b * nt + t, 0))
    return pl.pallas_call(
        _hgrn_prompt_kernel, grid=(batch, nt),
        in_specs=[blk, blk, blk, blk, blk, pl.BlockSpec((1, A_DV), lambda b, t: (0, 0))],
        out_specs=[blk, pl.BlockSpec((1, A_HEADS, A_DK, A_DV), lambda b, t: (b, 0, 0, 0))],
        out_shape=[jax.ShapeDtypeStruct((A_HEADS, n, A_DV), BF16),
                   jax.ShapeDtypeStruct((batch, A_HEADS, A_DK, A_DV), F32)],
        scratch_shapes=[pltpu.VMEM((A_HEADS, A_DV, A_DK), F32)],
        compiler_params=pltpu.CompilerParams(dimension_semantics=("parallel", "arbitrary"),
                                             vmem_limit_bytes=VMEM_LIMIT),
        name="hgrn_prompt",
    )(hq, hk, hg, hv, hgate, ng)


def _hgrn_sample_kernel(q_ref, k_ref, g_ref, v_ref, gate_ref, ng_ref, s0_ref, o_ref, sf_ref, *, t, nb):
    c = HGRN_SUB
    ng = ng_ref[...]
    zpad = jnp.zeros((c - t, A_DK), F32)
    pad = lambda a: jnp.concatenate([a, zpad], axis=0)
    for h in range(A_HEADS):
        q, k, g = q_ref[h].astype(F32), k_ref[h], g_ref[h]
        v, gate = v_ref[h].astype(F32), gate_ref[h].astype(F32)
        outs = []
        for b in range(nb):
            sl = slice(b * t, (b + 1) * t)
            o, st_new = _hgrn_chunk(pad(q[sl]), pad(k[sl]), pad(g[sl]), pad(v[sl]), pad(gate[sl]), ng,
                                    s0_ref[b, h].T, c)
            sf_ref[b, h] = st_new.T
            outs.append(o[:t])
        o_ref[h] = jnp.concatenate(outs, axis=0).astype(BF16)


def _hgrn_sample(hq, hk, hg, hv, hgate, ng, s0, t):
    bd = s0.shape[0]
    nb = 16 // t
    blk = pl.BlockSpec((A_HEADS, nb * t, A_DK), lambda i: (0, i, 0))
    sblk = pl.BlockSpec((nb, A_HEADS, A_DK, A_DV), lambda i: (i, 0, 0, 0))
    return pl.pallas_call(
        functools.partial(_hgrn_sample_kernel, t=t, nb=nb), grid=(bd // nb,),
        in_specs=[blk, blk, blk, blk, blk, pl.BlockSpec((1, A_DV), lambda i: (0, 0)), sblk],
        out_specs=[blk, sblk],
        out_shape=[jax.ShapeDtypeStruct((A_HEADS, bd * t, A_DV), BF16),
                   jax.ShapeDtypeStruct((bd, A_HEADS, A_DK, A_DV), F32)],
        compiler_params=pltpu.CompilerParams(dimension_semantics=("parallel",), vmem_limit_bytes=VMEM_LIMIT),
        name="hgrn_sample",
    )(hq, hk, hg, hv, hgate, ng, s0)


def _count_ge(keys_ref, ntiles, cand):
    def body(j, acc):
        tile = keys_ref[pl.ds(pl.multiple_of(j * K_TILE, K_TILE), K_TILE), :]
        hit = jnp.where(tile >= cand, 1, 0)
        return acc + jnp.sum(hit.reshape(K_TILE // 8, 8, Q_BLOCK), axis=0)
    acc = lax.fori_loop(0, ntiles, body, jnp.zeros((8, Q_BLOCK), I32))
    return jnp.sum(acc, axis=0, keepdims=True)


def _kth_largest(count_ge, topk):
    zero = jnp.zeros((1, Q_BLOCK), I32)
    c0 = count_ge(zero)
    ok = c0 >= topk
    thr = jnp.where(ok, 0, INT_MIN)
    nge = jnp.where(ok, c0, 0)

    def body(i, carry):
        thr, nge = carry
        cand = thr | jnp.left_shift(jnp.int32(1), 30 - i)
        cnt = count_ge(cand)
        ok = cnt >= topk
        return jnp.where(ok, cand, thr), jnp.where(ok, cnt, nge)

    return lax.fori_loop(0, 31, body, (thr, nge))


def _dsa_prompt_kernel(ki_ref, qiT_ref, wT_ref, k_ref, qT_ref, vT_ref, gb_ref, o_ref, keys_scr, oT_scr, *, topk):
    i = pl.program_id(1)
    ntiles = (i * Q_BLOCK + Q_BLOCK + K_TILE - 1) // K_TILE
    qpos = i * Q_BLOCK + lax.broadcasted_iota(I32, (K_TILE, Q_BLOCK), 1)
    krow = lax.broadcasted_iota(I32, (K_TILE, Q_BLOCK), 0)

    qi = qiT_ref[0]
    wT = wT_ref[...]

    def score_body(j, carry):
        ks = pl.multiple_of(j * K_TILE, K_TILE)
        x = _dot(ki_ref[pl.ds(ks, K_TILE), :], qi)
        sc = jnp.zeros((K_TILE, Q_BLOCK), F32)
        for h in range(IDX_HEADS):
            sc = sc + jnp.maximum(x[:, h * Q_BLOCK:(h + 1) * Q_BLOCK], 0.0) * wT[h:h + 1]
        sc = jnp.where(krow + ks <= qpos, sc, -jnp.inf)
        keys_scr[pl.ds(ks, K_TILE), :] = _sort_key(sc)
        return carry

    lax.fori_loop(0, ntiles, score_body, 0)

    count_ge = functools.partial(_count_ge, keys_scr, ntiles)
    thr, nge = _kth_largest(count_ge, topk)
    live = thr > KEY_NEG_INF
    thr = jnp.maximum(thr, KEY_NEG_INF + 1)
    cut_scr_val = jnp.full((1, Q_BLOCK), ntiles * K_TILE, I32)
    has_ties = jnp.max(jnp.where(live & (nge > topk), 1, 0)) > 0

    def tie_cut():
        need = topk - count_ge(thr + 1)

        def count_ties_below(pos):
            def body(j, acc):
                ks = pl.multiple_of(j * K_TILE, K_TILE)
                tile = keys_scr[pl.ds(ks, K_TILE), :]
                hit = jnp.where((tile == thr) & (krow + ks < pos), 1, 0)
                return acc + jnp.sum(hit.reshape(K_TILE // 8, 8, Q_BLOCK), axis=0)
            acc = lax.fori_loop(0, ntiles, body, jnp.zeros((8, Q_BLOCK), I32))
            return jnp.sum(acc, axis=0, keepdims=True)

        nbits = max(1, int(math.ceil(math.log2(keys_scr.shape[0] + 1))))

        def body(b, pos):
            cand = pos | jnp.left_shift(jnp.int32(1), nbits - 1 - b)
            return jnp.where(count_ties_below(cand) <= need, cand, pos)

        return lax.fori_loop(0, nbits, body, jnp.zeros((1, Q_BLOCK), I32))

    cut = lax.cond(has_ties, tie_cut, lambda: cut_scr_val)

    def sel_mask(ks):
        tile = keys_scr[pl.ds(ks, K_TILE), :]
        return (tile > thr) | ((tile == thr) & (krow + ks < cut))

    npair = B_HEADS // 2
    for p in range(npair):
        qp = qT_ref[p * LANES:(p + 1) * LANES, :]
        z = jnp.zeros((B_DH, Q_BLOCK), BF16)
        rhs = jnp.concatenate([jnp.concatenate([qp[:B_DH], z], axis=0),
                               jnp.concatenate([z, qp[B_DH:]], axis=0)], axis=1)

        def logits(j):
            ks = pl.multiple_of(j * K_TILE, K_TILE)
            l = _dot(k_ref[pl.ds(ks, K_TILE), p * LANES:(p + 1) * LANES], rhs)
            m = sel_mask(ks)
            return jnp.where(jnp.concatenate([m, m], axis=1), l, -jnp.inf), ks

        def max_body(j, mx):
            l, _ = logits(j)
            return jnp.maximum(mx, jnp.max(l.reshape(K_TILE // 8, 8, 2 * Q_BLOCK), axis=0))

        mx = lax.fori_loop(0, ntiles, max_body, jnp.full((8, 2 * Q_BLOCK), -jnp.inf, F32))
        mx = jnp.max(mx, axis=0, keepdims=True)

        def acc_body(j, carry):
            den, acc = carry
            l, ks = logits(j)
            e = jnp.exp(l - mx)
            den = den + jnp.sum(e.reshape(K_TILE // 8, 8, 2 * Q_BLOCK), axis=0)
            acc = acc + _dot(vT_ref[p * LANES:(p + 1) * LANES, pl.ds(ks, K_TILE)], e.astype(BF16))
            return den, acc

        den, acc = lax.fori_loop(0, ntiles, acc_body,
                                 (jnp.zeros((8, 2 * Q_BLOCK), F32), jnp.zeros((LANES, 2 * Q_BLOCK), F32)))
        den = jnp.sum(den, axis=0, keepdims=True)
        oT_scr[p * LANES:p * LANES + B_DH, :] = acc[:B_DH, :Q_BLOCK] / den[:, :Q_BLOCK]
        oT_scr[p * LANES + B_DH:(p + 1) * LANES, :] = acc[B_DH:, Q_BLOCK:] / den[:, Q_BLOCK:]

    o_ref[...] = (oT_scr[...].T * gb_ref[...].astype(F32)).astype(BF16)


def _dsa_prompt(kibf, qiT2, wT, kbf, qT, vT, gbs, batch, seq):
    n = batch * seq
    nq = seq // Q_BLOCK
    topk = min(TOPK_MAX, seq // 4)
    lpad = -(-seq // K_TILE) * K_TILE
    assert lpad == seq and topk <= K_TILE
    return pl.pallas_call(
        functools.partial(_dsa_prompt_kernel, topk=topk), grid=(batch, nq),
        in_specs=[
            pl.BlockSpec((seq, IDX_DIM), lambda b, i: (b, 0)),
            pl.BlockSpec((1, IDX_DIM, IDX_HEADS * Q_BLOCK), lambda b, i: (b * nq + i, 0, 0)),
            pl.BlockSpec((IDX_HEADS, Q_BLOCK), lambda b, i: (0, b * nq + i)),
            pl.BlockSpec((seq, B_WIDTH), lambda b, i: (b, 0)),
            pl.BlockSpec((B_WIDTH, Q_BLOCK), lambda b, i: (0, b * nq + i)),
            pl.BlockSpec((B_WIDTH, seq), lambda b, i: (0, b)),
            pl.BlockSpec((Q_BLOCK, B_WIDTH), lambda b, i: (b * nq + i, 0)),
        ],
        out_specs=pl.BlockSpec((Q_BLOCK, B_WIDTH), lambda b, i: (b * nq + i, 0)),
        out_shape=jax.ShapeDtypeStruct((n, B_WIDTH), BF16),
        scratch_shapes=[pltpu.VMEM((seq, Q_BLOCK), I32), pltpu.VMEM((B_WIDTH, Q_BLOCK), F32)],
        compiler_params=pltpu.CompilerParams(dimension_semantics=("parallel", "arbitrary"),
                                             vmem_limit_bytes=VMEM_LIMIT),
        name="dsa_prompt",
    )(kibf, qiT2, wT, kbf, qT, vT, gbs)


def _dsa_sample_kernel(pt_ref, qi_ref, wrep_ref, qbd_ref, kin_ref, kn_ref, vn_ref, gb_ref,
                       cki_hbm, ck_hbm, cv_hbm, o_ref, ki_buf, k_buf, v_buf, keys_scr, sem_i, sem_k, sem_v,
                       *, t, npages, topk, depth):
    b = pl.program_id(0)
    page = ki_buf.shape[2]
    total = (npages + 1) * page
    lane = lax.broadcasted_iota(I32, (t, page), 1)
    qrow = lax.broadcasted_iota(I32, (t, page), 0)

    def ki_copy(p):
        return pltpu.make_async_copy(cki_hbm.at[pt_ref[b, p], 0], ki_buf.at[p], sem_i.at[p])

    def k_copy(p, slot):
        return pltpu.make_async_copy(ck_hbm.at[pt_ref[b, p], 0], k_buf.at[slot], sem_k.at[slot])

    def v_copy(p, slot):
        return pltpu.make_async_copy(cv_hbm.at[pt_ref[b, p], 0], v_buf.at[slot], sem_v.at[slot])

    def start_ki(p, carry):
        ki_copy(p).start()
        return carry

    lax.fori_loop(0, npages, start_ki, 0)
    for s in range(depth):
        k_copy(s, s).start()
        v_copy(s, s).start()

    def scores(kidx_t):
        x = _dot(qi_ref[0], kidx_t)
        xw = jnp.maximum(x, 0.0) * wrep_ref[0]
        sc = xw[0:t]
        for h in range(1, IDX_HEADS):
            sc = sc + xw[h * t:(h + 1) * t]
        return sc

    def score_page(p, carry):
        ki_copy(p).wait()
        sc = scores(ki_buf[p].astype(BF16))
        keys_scr[:, pl.ds(pl.multiple_of(p * page, page), page)] = _sort_key(sc)
        return carry

    lax.fori_loop(0, npages, score_page, 0)

    sc = scores(kin_ref[0])
    keys_scr[:, npages * page:] = jnp.where(lane <= qrow, _sort_key(sc), jnp.where(lane < t, KEY_NEG_INF, INT_MIN))

    def count_ge(cand):
        hit = jnp.where(keys_scr[...] >= cand, 1, 0)
        part = hit[:, 0:page]
        for c in range(1, npages + 1):
            part = part + hit[:, c * page:(c + 1) * page]
        return jnp.sum(part, axis=1, keepdims=True)

    c0 = count_ge(jnp.zeros((t, 1), I32))
    ok = c0 >= topk
    thr = jnp.where(ok, 0, INT_MIN)
    nge = jnp.where(ok, c0, 0)

    def bit_step(i, carry):
        thr, nge = carry
        cand = thr | jnp.left_shift(jnp.int32(1), 30 - i)
        cnt = count_ge(cand)
        ok = cnt >= topk
        return jnp.where(ok, cand, thr), jnp.where(ok, cnt, nge)

    thr, nge = lax.fori_loop(0, 31, bit_step, (thr, nge))
    live = thr > KEY_NEG_INF
    thr = jnp.maximum(thr, KEY_NEG_INF + 1)
    has_ties = jnp.max(jnp.where(live & (nge > topk), 1, 0)) > 0
    pos_all = lax.broadcasted_iota(I32, (t, total), 1)

    def tie_cut():
        need = topk - count_ge(thr + 1)
        nbits = max(1, int(math.ceil(math.log2(total + 1))))

        def pos_step(j, pos):
            cand = pos | jnp.left_shift(jnp.int32(1), nbits - 1 - j)
            hit = jnp.where((keys_scr[...] == thr) & (pos_all < cand), 1, 0)
            cnt = jnp.sum(hit, axis=1, keepdims=True)
            return jnp.where(cnt <= need, cand, pos)

        return lax.fori_loop(0, nbits, pos_step, jnp.zeros((t, 1), I32))

    cut = lax.cond(has_ties, tie_cut, lambda: jnp.full((t, 1), total, I32))

    qbd = qbd_ref[0]

    def attend(k_t, v_t, ks, carry):
        m_old, l_old, acc = carry
        tile = keys_scr[:, pl.ds(ks, page)]
        sel = (tile > thr) | ((tile == thr) & (lane + ks < cut))
        sel = jnp.concatenate([sel.astype(I32)] * B_HEADS, axis=0) > 0
        lg = jnp.where(sel, _dot(qbd, k_t), -jnp.inf)
        m_new = jnp.maximum(m_old, jnp.max(lg, axis=1, keepdims=True))
        m_safe = jnp.where(m_new == -jnp.inf, 0.0, m_new)
        alpha = jnp.exp(m_old - m_safe)
        e = jnp.exp(lg - m_safe)
        l_new = alpha * l_old + jnp.sum(e, axis=1, keepdims=True)
        return m_new, l_new, alpha * acc + _dot_nt(e.astype(BF16), v_t)

    def attend_page(p, carry):
        slot = p % depth
        k_copy(p, slot).wait()
        v_copy(p, slot).wait()
        carry = attend(k_buf[slot].astype(BF16), v_buf[slot].astype(BF16), pl.multiple_of(p * page, page), carry)

        @pl.when(p + depth < npages)
        def _():
            k_copy(p + depth, slot).start()
            v_copy(p + depth, slot).start()

        return carry

    nrow = B_HEADS * t
    carry = (jnp.full((nrow, 1), -jnp.inf, F32), jnp.zeros((nrow, 1), F32), jnp.zeros((nrow, B_WIDTH), F32))
    carry = lax.fori_loop(0, npages, attend_page, carry)
    _, den, acc = attend(kn_ref[0], vn_ref[0], npages * page, carry)
    o = acc / den
    col = lax.broadcasted_iota(I32, (t, B_WIDTH), 1)
    out = jnp.zeros((t, B_WIDTH), F32)
    for h in range(B_HEADS):
        out = jnp.where((col >= h * B_DH) & (col < (h + 1) * B_DH), o[h * t:(h + 1) * t], out)
    o_ref[0] = (out * gb_ref[0].astype(F32)).astype(BF16)


def _dsa_sample(page_table, qi_rows, wrep, qbd, kin_t, kn_t, vn_t, gbs, cache_kidx_t, cache_k_t, cache_v_t, t):
    bd, npages = page_table.shape
    page = cache_kidx_t.shape[3]
    total = npages * page + t
    topk = min(TOPK_MAX, total // 4)
    depth = min(SAMPLE_DMA_DEPTH, npages)
    per_b = lambda b, pt: (b, 0, 0)
    hbm = pl.BlockSpec(memory_space=pl.ANY)
    grid_spec = pltpu.PrefetchScalarGridSpec(
        num_scalar_prefetch=1, grid=(bd,),
        in_specs=[
            pl.BlockSpec((1, IDX_HEADS * t, IDX_DIM), per_b),
            pl.BlockSpec((1, IDX_HEADS * t, page), per_b),
            pl.BlockSpec((1, B_HEADS * t, B_WIDTH), per_b),
            pl.BlockSpec((1, IDX_DIM, page), per_b),
            pl.BlockSpec((1, B_WIDTH, page), per_b),
            pl.BlockSpec((1, B_WIDTH, page), per_b),
            pl.BlockSpec((1, t, B_WIDTH), per_b),
            hbm, hbm, hbm,
        ],
        out_specs=pl.BlockSpec((1, t, B_WIDTH), per_b),
        scratch_shapes=[
            pltpu.VMEM((npages, IDX_DIM, page), F32),
            pltpu.VMEM((depth, B_WIDTH, page), F32),
            pltpu.VMEM((depth, B_WIDTH, page), F32),
            pltpu.VMEM((t, (npages + 1) * page), I32),
            pltpu.SemaphoreType.DMA((npages,)),
            pltpu.SemaphoreType.DMA((depth,)),
            pltpu.SemaphoreType.DMA((depth,)),
        ],
    )
    return pl.pallas_call(
        functools.partial(_dsa_sample_kernel, t=t, npages=npages, topk=topk, depth=depth),
        grid_spec=grid_spec,
        out_shape=jax.ShapeDtypeStruct((bd, t, B_WIDTH), BF16),
        compiler_params=pltpu.CompilerParams(dimension_semantics=("arbitrary",), vmem_limit_bytes=VMEM_LIMIT),
        name="dsa_sample",
    )(page_table, qi_rows, wrep, qbd, kin_t, kn_t, vn_t, gbs, cache_kidx_t, cache_k_t, cache_v_t)


def _merge_kernel(x_ref, ma_ref, mb_ref, w_ref, g_ref, b_ref, y_ref, *, alpha):
    mix = jnp.concatenate([ma_ref[h] for h in range(A_HEADS)] + [mb_ref[...]], axis=1)
    y = alpha * x_ref[...] + _dot(mix, w_ref[...])
    mu = jnp.mean(y, axis=-1, keepdims=True)
    d = y - mu
    var = jnp.mean(d * d, axis=-1, keepdims=True)
    y_ref[...] = d * lax.rsqrt(var + LN_EPS) * g_ref[...] + b_ref[...]


def _merge(x, mix_a, mix_b, w_out, ln_g, ln_b, alpha, tm):
    n, dm = x.shape
    return pl.pallas_call(
        functools.partial(_merge_kernel, alpha=alpha), grid=(n // tm,),
        in_specs=[
            pl.BlockSpec((tm, dm), lambda i: (i, 0)),
            pl.BlockSpec((A_HEADS, tm, A_DV), lambda i: (0, i, 0)),
            pl.BlockSpec((tm, B_WIDTH), lambda i: (i, 0)),
            pl.BlockSpec((A_WIDTH + B_WIDTH, dm), lambda i: (0, 0)),
            pl.BlockSpec((1, dm), lambda i: (0, 0)),
            pl.BlockSpec((1, dm), lambda i: (0, 0)),
        ],
        out_specs=pl.BlockSpec((tm, dm), lambda i: (i, 0)),
        out_shape=jax.ShapeDtypeStruct((n, dm), F32),
        compiler_params=pltpu.CompilerParams(dimension_semantics=("parallel",), vmem_limit_bytes=VMEM_LIMIT),
        name="merge",
    )(x, mix_a, mix_b, w_out, ln_g, ln_b)


def _split_weights(w_in_l):
    offs = np.cumsum([0, 512, 512, 512, 512, 512, 512, 512, 512, IDX_HEADS * IDX_DIM, IDX_DIM, IDX_HEADS])
    col = lambda i: w_in_l[:, offs[i]:offs[i + 1]]
    qa, fa, ia, ga, qb, kb, vb, gb, qi, ki, wi = (col(i) for i in range(11))
    pad = jnp.zeros((w_in_l.shape[0], LANES - IDX_DIM), w_in_l.dtype)
    wn = jnp.concatenate([qa, fa, ia, ga, kb, gb, ki, pad], axis=1).astype(BF16)
    wt = jnp.concatenate([qb, kb, vb, qi, wi, ki], axis=1).T.astype(BF16)
    return wn, wt


def _layer(xp, xs, cache_k, cache_v, cache_kidx, s0_sample, page_table, w_in_l, lb_l, norm_g_l, kn_g_l, kn_b_l,
           w_out_l, ln_g_l, ln_b_l, alpha):
    b, l, dm = xp.shape
    bd, t, _ = xs.shape
    npages, page = page_table.shape[1], cache_k.shape[2]
    assert page == Q_BLOCK and l % page == 0
    past = npages * page
    wn, wt = _split_weights(w_in_l)
    lb = lb_l.reshape(1, A_WIDTH)
    ng = norm_g_l.reshape(1, A_DV)
    w_out_b = w_out_l.astype(BF16)
    lng, lnb = ln_g_l.reshape(1, dm), ln_b_l.reshape(1, dm)

    tm = 256
    xp2 = xp.reshape(b * l, dm)
    (hq, hk, hg, hv, hgate, kbf, gbs, kibf, qT, vT, qiT2, wT, k_p, v_p, ki_p) = _project(
        xp2, jnp.arange(l, dtype=I32), wn, wt, lb, kn_g_l, kn_b_l, tm)
    mix_a, s_p = _hgrn_prompt(hq, hk, hg, hv, hgate, ng, b, l, min(l, 512))
    mix_b = _dsa_prompt(kibf, qiT2, wT, kbf, qT, vT, gbs, b, l)
    y_p = _merge(xp2, mix_a, mix_b, w_out_b, lng, lnb, alpha, 512).reshape(b, l, dm)

    ns = bd * t
    xs2 = xs.reshape(ns, dm)
    pos_s = past + (jnp.arange(ns, dtype=I32) % t)
    (hq, hk, hg, hv, hgate, kbf, gbs, kibf, qT, vT, qiT2, wT, k_s, v_s, ki_s) = _project(
        xs2, pos_s, wn, wt, lb, kn_g_l, kn_b_l, ns)
    mix_a, s_s = _hgrn_sample(hq, hk, hg, hv, hgate, ng, s0_sample, t)
    qi_nat = qiT2.reshape(ns // Q_BLOCK, IDX_DIM, IDX_HEADS, Q_BLOCK).transpose(0, 3, 2, 1)
    qi_rows = qi_nat.reshape(bd, t, IDX_HEADS, IDX_DIM).transpose(0, 2, 1, 3).reshape(bd, IDX_HEADS * t, IDX_DIM)
    w_rows = wT.T.reshape(bd, t, IDX_HEADS).transpose(0, 2, 1).reshape(bd, IDX_HEADS * t, 1)
    wrep = jnp.broadcast_to(w_rows, (bd, IDX_HEADS * t, page))
    q_nat = qT.T.reshape(bd, 1, t, B_HEADS, B_DH)
    eye = jnp.eye(B_HEADS, dtype=BF16).reshape(1, B_HEADS, 1, B_HEADS, 1)
    qbd = (q_nat * eye).reshape(bd, B_HEADS * t, B_WIDTH)

    def per_seq(pages):
        feat = pages.shape[1]
        a = pages.transpose(1, 0, 2).reshape(feat, bd, t).transpose(1, 0, 2).astype(BF16)
        return jnp.concatenate([a, jnp.zeros((bd, feat, page - t), BF16)], axis=2)

    token_minor = lambda c: jnp.moveaxis(c, 2, -1).reshape(c.shape[0], c.shape[1], -1, page)
    mix_b = _dsa_sample(page_table, qi_rows, wrep, qbd, per_seq(ki_s), per_seq(k_s), per_seq(v_s),
                        gbs.reshape(bd, t, B_WIDTH), token_minor(cache_kidx), token_minor(cache_k),
                        token_minor(cache_v), t).reshape(ns, B_WIDTH)
    y_s = _merge(xs2, mix_a, mix_b, w_out_b, lng, lnb, alpha, ns).reshape(bd, t, dm)

    return (y_p, y_s, k_p, v_p, ki_p, s_p, k_s, v_s, ki_s, s_s)


def kernel(x_prompt, x_sample, cache_k, cache_v, cache_kidx, state_hgrn, page_table, w_in, hgrn_lb_logits,
           hgrn_norm_g, idx_norm_g, idx_norm_b, w_out, ln_g, ln_b):
    depth = w_in.shape[0]
    assert depth == 1, "one layer per step"
    b, l, _ = x_prompt.shape
    bd, t, _ = x_sample.shape
    page = cache_k.shape[2]
    alpha = (2.0 * depth) ** 0.25
    lbs = jnp.cumsum(jax.nn.softmax(hgrn_lb_logits.astype(F32), axis=0), axis=0)[:depth]
    (y_p, y_s, k_p, v_p, ki_p, s_p, k_s, v_s, ki_s, s_s) = _layer(
        x_prompt, x_sample, cache_k, cache_v, cache_kidx, state_hgrn[0], page_table, w_in[0], lbs[0],
        hgrn_norm_g[0], idx_norm_g[0], idx_norm_b[0], w_out[0], ln_g[0], ln_b[0], alpha)
    nat_p = lambda pg, *f: jnp.moveaxis(pg.reshape(b, l // page, 1, *f, page), -1, 3)
    nat_s = lambda pg, *f: pg.transpose(0, 2, 1).reshape(bd, 1, t, *f)
    return (
        y_p, y_s,
        nat_p(k_p, B_HEADS, B_DH), nat_p(v_p, B_HEADS, B_DH), nat_p(ki_p, IDX_DIM),
        s_p[None],
        nat_s(k_s, B_HEADS, B_DH), nat_s(v_s, B_HEADS, B_DH), nat_s(ki_s, IDX_DIM),
        s_s[None],
    )
```

```python
import functools
import math

import jax
import jax.numpy as jnp
import numpy as np
from jax import lax
from jax.experimental import pallas as pl
from jax.experimental.pallas import tpu as pltpu

F32 = jnp.float32
BF16 = jnp.bfloat16
I32 = jnp.int32

A_HEADS = 4
A_DK = 128
A_DV = 128
B_HEADS = 8
B_DH = 64
IDX_HEADS = 16
IDX_DIM = 64
TOPK_MAX = 256
ROPE_THETA = 500000.0
ROT = 16
ROT_HALF = ROT // 2
LN_EPS = 1e-5
Q_BLOCK = 128
K_TILE = 256
ATT_TILE = 512
HGRN_CHUNK = 128
HGRN_SUB = 16
LANES = 128
VMEM_LIMIT = 56 * 1024 * 1024
SAMPLE_DMA_DEPTH = 8
SAMPLE_PAGE_GROUP = 4

A_WIDTH = A_HEADS * A_DV
B_WIDTH = B_HEADS * B_DH
NAT_WIDTH = 6 * 512 + LANES
TR_ROWS = 3 * 512 + IDX_HEADS * IDX_DIM + IDX_HEADS + IDX_DIM
INT_MIN = -(2 ** 31)
KEY_NEG_INF = -0x7F800000


def _dot(a, b):
    return jnp.dot(a, b, preferred_element_type=F32)


def _dot_nt(a, b):
    return lax.dot_general(a, b, (((1,), (1,)), ((), ())), preferred_element_type=F32)


def _dot_tn(a, b):
    return lax.dot_general(a, b, (((0,), (0,)), ((), ())), preferred_element_type=F32)


def _silu(x):
    return x * jax.nn.sigmoid(x)


def _key_to_float(key):
    m = key >> 31
    mag = (key ^ m) - m
    return pltpu.bitcast(mag | (m & INT_MIN), F32)


def _proj_kernel(x_ref, wn_ref, wt_ref, lb_ref, kng_ref, knb_ref, kngc_ref, knbc_ref, cn_ref, sa_ref, sb_ref,
                 ct_ref, st_ref,
                 hq_ref, hk_ref, hg_ref, hv_ref, hgate_ref, kbf_ref, gb_ref, kibf_ref,
                 qT_ref, vT_ref, qiT_ref, wT_ref, kTp_ref, vTp_ref, kiTp_ref):
    tm = x_ref.shape[0]
    xb = x_ref[...].astype(BF16)

    def nat(col, width=512):
        return _dot(xb, wn_ref[:, col:col + width])

    lb = lb_ref[...]
    qa = nat(0)
    hq = _silu(qa)
    fa = nat(512)
    hg = jnp.log(lb + (1.0 - lb) * jax.nn.sigmoid(fa))
    hk = (1.0 - lb) * jax.nn.sigmoid(-fa)
    ia = nat(1024)
    hgate = _silu(nat(1536))
    for h in range(A_HEADS):
        sl = slice(h * A_DK, (h + 1) * A_DK)
        hq_ref[h] = hq[:, sl].astype(BF16)
        hk_ref[h] = hk[:, sl]
        hg_ref[h] = hg[:, sl]
        hv_ref[h] = ia[:, sl].astype(BF16)
        hgate_ref[h] = hgate[:, sl].astype(BF16)

    cn, sa, sb = cn_ref[...], sa_ref[...], sb_ref[...]

    def rope_nat(xc):
        return xc * cn + pltpu.roll(xc, LANES - ROT_HALF, 1) * sa + pltpu.roll(xc, ROT_HALF, 1) * sb

    kb = nat(2048)
    kbf_ref[...] = jnp.concatenate(
        [rope_nat(kb[:, c * LANES:(c + 1) * LANES]) for c in range(B_WIDTH // LANES)], axis=1).astype(BF16)
    gb_ref[...] = _silu(nat(2560)).astype(BF16)

    kic = nat(3072, LANES)
    lane = lax.broadcasted_iota(I32, (tm, LANES), 1)
    inb = lane < IDX_DIM
    mu = jnp.sum(jnp.where(inb, kic, 0.0), axis=-1, keepdims=True) * (1.0 / IDX_DIM)
    d = jnp.where(inb, kic - mu, 0.0)
    var = jnp.sum(d * d, axis=-1, keepdims=True) * (1.0 / IDX_DIM)
    kin = d * lax.rsqrt(var + LN_EPS) * kng_ref[...] + knb_ref[...]
    kibf_ref[...] = rope_nat(kin)[:, :IDX_DIM].astype(BF16)

    ct, st = ct_ref[...], st_ref[...]
    npage = tm // Q_BLOCK

    def tr(row, height):
        return _dot_nt(wt_ref[row:row + height, :], xb)

    def rope_tr(blk):
        x1, x2 = blk[0:ROT_HALF], blk[ROT_HALF:ROT]
        return jnp.concatenate([x1 * ct - x2 * st, x1 * st + x2 * ct, blk[ROT:]], axis=0)

    qbT = tr(0, B_WIDTH)
    for h in range(B_HEADS):
        blk = rope_tr(qbT[h * B_DH:(h + 1) * B_DH])
        qT_ref[h * B_DH:(h + 1) * B_DH, :] = (blk * (B_DH ** -0.5)).astype(BF16)
    kbT = tr(B_WIDTH, B_WIDTH)
    for h in range(B_HEADS):
        blk = rope_tr(kbT[h * B_DH:(h + 1) * B_DH])
        for pg in range(npage):
            kTp_ref[pg, h * B_DH:(h + 1) * B_DH, :] = blk[:, pg * Q_BLOCK:(pg + 1) * Q_BLOCK]
    vbT = tr(2 * B_WIDTH, B_WIDTH)
    vT_ref[...] = vbT.astype(BF16)
    for pg in range(npage):
        vTp_ref[pg] = vbT[:, pg * Q_BLOCK:(pg + 1) * Q_BLOCK]
    row = 3 * B_WIDTH
    qiT = tr(row, IDX_HEADS * IDX_DIM)
    for h in range(IDX_HEADS):
        blk = rope_tr(qiT[h * IDX_DIM:(h + 1) * IDX_DIM]).astype(BF16)
        for pg in range(npage):
            qiT_ref[pg, :, h * Q_BLOCK:(h + 1) * Q_BLOCK] = blk[:, pg * Q_BLOCK:(pg + 1) * Q_BLOCK]
    row += IDX_HEADS * IDX_DIM
    wT_ref[...] = tr(row, IDX_HEADS) * (IDX_HEADS ** -0.5 * IDX_DIM ** -0.5)
    row += IDX_HEADS
    kiT = tr(row, IDX_DIM)
    muT = jnp.mean(kiT, axis=0, keepdims=True)
    dT = kiT - muT
    varT = jnp.mean(dT * dT, axis=0, keepdims=True)
    kiT = rope_tr(dT * lax.rsqrt(varT + LN_EPS) * kngc_ref[...] + knbc_ref[...])
    for pg in range(npage):
        kiTp_ref[pg] = kiT[:, pg * Q_BLOCK:(pg + 1) * Q_BLOCK]


def _rope_tables(pos):
    inv = ROPE_THETA ** (-jnp.arange(ROT_HALF, dtype=F32) / ROT_HALF)
    ang = pos.astype(F32)[:, None] * inv[None, :]
    cos, sin = jnp.cos(ang), jnp.sin(ang)
    p = pos.shape[0]
    one = jnp.ones((p, B_DH - ROT), F32)
    zero8 = jnp.zeros((p, ROT_HALF), F32)
    zero = jnp.zeros((p, B_DH - ROT), F32)
    cn = jnp.concatenate([cos, cos, one], axis=1)
    sa = jnp.concatenate([-sin, zero8, zero], axis=1)
    sb = jnp.concatenate([zero8, sin, zero], axis=1)
    tile2 = lambda a: jnp.concatenate([a, a], axis=1)
    return tile2(cn), tile2(sa), tile2(sb), cos.T, sin.T


def _project(x, pos, wn, wt, lb, kn_g, kn_b, tm):
    n, dm = x.shape
    p = pos.shape[0]
    nper = p // tm
    cn, sa, sb, ct, st = _rope_tables(pos)
    padl = lambda a: jnp.concatenate([a, jnp.zeros((LANES - IDX_DIM,), a.dtype)]).reshape(1, LANES)
    grid = (n // tm,)
    row = lambda i: (i, 0)
    full = lambda i: (0, 0)
    head = lambda i: (0, i, 0)
    page = lambda i: (i, 0, 0)
    per = lambda i: (i % nper, 0)
    perT = lambda i: (0, i % nper)
    colT = lambda i: (0, i)
    in_specs = [
        pl.BlockSpec((tm, dm), row),
        pl.BlockSpec((dm, NAT_WIDTH), full, pipeline_mode=pl.Buffered(1)),
        pl.BlockSpec((TR_ROWS, dm), full, pipeline_mode=pl.Buffered(1)),
        pl.BlockSpec((1, A_WIDTH), full),
        pl.BlockSpec((1, LANES), full),
        pl.BlockSpec((1, LANES), full),
        pl.BlockSpec((IDX_DIM, 1), full),
        pl.BlockSpec((IDX_DIM, 1), full),
        pl.BlockSpec((tm, LANES), per),
        pl.BlockSpec((tm, LANES), per),
        pl.BlockSpec((tm, LANES), per),
        pl.BlockSpec((ROT_HALF, tm), perT),
        pl.BlockSpec((ROT_HALF, tm), perT),
    ]
    npage = tm // Q_BLOCK
    hshape = lambda dt: jax.ShapeDtypeStruct((A_HEADS, n, A_DK), dt)
    out_shape = [
        hshape(BF16), hshape(F32), hshape(F32), hshape(BF16), hshape(BF16),
        jax.ShapeDtypeStruct((n, B_WIDTH), BF16), jax.ShapeDtypeStruct((n, B_WIDTH), BF16),
        jax.ShapeDtypeStruct((n, IDX_DIM), BF16),
        jax.ShapeDtypeStruct((B_WIDTH, n), BF16), jax.ShapeDtypeStruct((B_WIDTH, n), BF16),
        jax.ShapeDtypeStruct((n // Q_BLOCK, IDX_DIM, IDX_HEADS * Q_BLOCK), BF16),
        jax.ShapeDtypeStruct((IDX_HEADS, n), F32),
        jax.ShapeDtypeStruct((n // Q_BLOCK, B_WIDTH, Q_BLOCK), F32),
        jax.ShapeDtypeStruct((n // Q_BLOCK, B_WIDTH, Q_BLOCK), F32),
        jax.ShapeDtypeStruct((n // Q_BLOCK, IDX_DIM, Q_BLOCK), F32),
    ]
    hspec = pl.BlockSpec((A_HEADS, tm, A_DK), head)
    out_specs = [
        hspec, hspec, hspec, hspec, hspec,
        pl.BlockSpec((tm, B_WIDTH), row), pl.BlockSpec((tm, B_WIDTH), row),
        pl.BlockSpec((tm, IDX_DIM), row),
        pl.BlockSpec((B_WIDTH, tm), colT), pl.BlockSpec((B_WIDTH, tm), colT),
        pl.BlockSpec((npage, IDX_DIM, IDX_HEADS * Q_BLOCK), page),
        pl.BlockSpec((IDX_HEADS, tm), colT),
        pl.BlockSpec((npage, B_WIDTH, Q_BLOCK), page),
        pl.BlockSpec((npage, B_WIDTH, Q_BLOCK), page),
        pl.BlockSpec((npage, IDX_DIM, Q_BLOCK), page),
    ]
    return pl.pallas_call(
        _proj_kernel, grid=grid, in_specs=in_specs, out_specs=out_specs, out_shape=out_shape,
        compiler_params=pltpu.CompilerParams(dimension_semantics=("parallel",), vmem_limit_bytes=VMEM_LIMIT),
        name="proj",
    )(x, wn, wt, lb, padl(kn_g), padl(kn_b), kn_g.reshape(IDX_DIM, 1), kn_b.reshape(IDX_DIM, 1),
      cn, sa, sb, ct, st)


def _hgrn_cumsum(g, c):
    if c == LANES:
        ri = lax.broadcasted_iota(I32, (c, c), 0)
        ci = lax.broadcasted_iota(I32, (c, c), 1)
        tri = jnp.where(ci <= ri, 1.0, 0.0).astype(BF16)
        g1 = g.astype(BF16)
        e1 = g - g1.astype(F32)
        g2 = e1.astype(BF16)
        g3 = (e1 - g2.astype(F32)).astype(BF16)
        return _dot(tri, g1) + _dot(tri, g2) + _dot(tri, g3)
    ri = lax.broadcasted_iota(I32, (c, A_DK), 0)
    cum = jnp.zeros((c, A_DK), F32)
    for s in range(c):
        cum = cum + jnp.where(ri >= s, g[s:s + 1], 0.0)
    return cum


def _hgrn_chunks(chains, ng, c):
    cums = [_hgrn_cumsum(g, c) for _, _, g, _, _, _ in chains]
    return [_hgrn_chunk(q, k, cum, v, gate, ng, st, c) for (q, k, _, v, gate, st), cum in zip(chains, cums)]


def _hgrn_chunk(q, k, cum, v, gate, ng, st, c):
    r = HGRN_SUB
    nsub = c // r
    stb = st.astype(BF16)
    o = _dot_nt((q * jnp.exp(cum)).astype(BF16), stb)

    rowi = lax.broadcasted_iota(I32, (r, A_DK), 0)
    lane = lax.broadcasted_iota(I32, (r, LANES), 1)
    prods = []
    for i in range(nsub):
        qs, cs = q[i * r:(i + 1) * r], cum[i * r:(i + 1) * r]
        for s in range(r):
            row = i * r + s
            p = (qs * k[row:row + 1]) * jnp.exp(cs - cum[row:row + 1])
            prods.append(jnp.where(rowi >= s, p, 0.0).astype(BF16))
    if nsub % 2 == 0:
        half = len(prods) // 2
        both = jnp.concatenate([jnp.concatenate(prods[:half], axis=0), jnp.concatenate(prods[half:], axis=0)], axis=1)
        wr = lax.broadcasted_iota(I32, (2 * A_DK, 2 * LANES), 0) < A_DK
        wc = lax.broadcasted_iota(I32, (2 * A_DK, 2 * LANES), 1) < LANES
        sums = _dot(both, jnp.where(wr == wc, 1.0, 0.0).astype(BF16))
        rsum = lambda row: (sums[row * r:(row + 1) * r, :LANES] if row < half
                            else sums[(row - half) * r:(row - half + 1) * r, LANES:])
    else:
        sums = _dot(jnp.concatenate(prods, axis=0), jnp.ones((A_DK, LANES), BF16))
        rsum = lambda row: sums[row * r:(row + 1) * r]
    sc_rows = []
    for i in range(nsub):
        sci = jnp.zeros((r, LANES), F32)
        for s in range(r):
            row = i * r + s
            sci = jnp.where(lane == row, rsum(row), sci)
        if i > 0:
            ref = cum[i * r - 1:i * r]
            qt = (q[i * r:(i + 1) * r] * jnp.exp(cum[i * r:(i + 1) * r] - ref)).astype(BF16)
            kt = (k[:i * r] * jnp.exp(ref - cum[:i * r])).astype(BF16)
            kt = jnp.concatenate([kt, jnp.zeros((LANES - i * r, A_DK), BF16)], axis=0)
            sci = sci + _dot_nt(qt, kt)
        sc_rows.append(sci)
    sc = jnp.concatenate(sc_rows, axis=0).astype(BF16)
    vb = v.astype(BF16)
    last = cum[c - 1:c]
    kh = (k * jnp.exp(last - cum)).astype(BF16)
    if c < LANES:
        zpad = jnp.zeros((LANES - c, A_DK), BF16)
        vb = jnp.concatenate([vb, zpad], axis=0)
        kh = jnp.concatenate([kh, zpad], axis=0)
    o = o + _dot(sc, vb)
    st_new = st * jnp.exp(last) + _dot_tn(vb, kh)
    ms = jnp.mean(o * o, axis=-1, keepdims=True)
    return o * lax.rsqrt(ms + LN_EPS) * ng * gate, st_new


def _hgrn_prompt_kernel(q_ref, k_ref, g_ref, v_ref, gate_ref, ng_ref, o_ref, sf_ref, st_scr):
    c = HGRN_CHUNK
    nchunk = q_ref.shape[1] // c
    tb = pl.program_id(1)

    @pl.when(tb == 0)
    def _():
        st_scr[...] = jnp.zeros_like(st_scr)

    ng = ng_ref[...]

    def body(j, carry):
        sl = pl.ds(pl.multiple_of(j * c, c), c)
        loaded = [(q_ref[h, sl, :].astype(F32), k_ref[h, sl, :], g_ref[h, sl, :], v_ref[h, sl, :].astype(F32),
                   gate_ref[h, sl, :].astype(F32), st_scr[h]) for h in range(A_HEADS)]
        for h, (o, st_new) in enumerate(_hgrn_chunks(loaded, ng, c)):
            st_scr[h] = st_new
            o_ref[h, sl, :] = o.astype(BF16)
        return carry

    lax.fori_loop(0, nchunk, body, 0)

    @pl.when(tb == pl.num_programs(1) - 1)
    def _():
        for h in range(A_HEADS):
            sf_ref[0, h] = st_scr[h].T


def _hgrn_prompt(hq, hk, hg, hv, hgate, ng, batch, seq, tb):
    n = batch * seq
    nt = seq // tb
    blk = pl.BlockSpec((A_HEADS, tb, A_DK), lambda b, t: (0, b * nt + t, 0))
    return pl.pallas_call(
        _hgrn_prompt_kernel, grid=(batch, nt),
        in_specs=[blk, blk, blk, blk, blk, pl.BlockSpec((1, A_DV), lambda b, t: (0, 0))],
        out_specs=[blk, pl.BlockSpec((1, A_HEADS, A_DK, A_DV), lambda b, t: (b, 0, 0, 0))],
        out_shape=[jax.ShapeDtypeStruct((A_HEADS, n, A_DV), BF16),
                   jax.ShapeDtypeStruct((batch, A_HEADS, A_DK, A_DV), F32)],
        scratch_shapes=[pltpu.VMEM((A_HEADS, A_DV, A_DK), F32)],
        compiler_params=pltpu.CompilerParams(dimension_semantics=("parallel", "arbitrary"),
                                             vmem_limit_bytes=VMEM_LIMIT),
        name="hgrn_prompt",
    )(hq, hk, hg, hv, hgate, ng)


def _hgrn_sample_kernel(q_ref, k_ref, g_ref, v_ref, gate_ref, ng_ref, s0_ref, o_ref, sf_ref, *, t, nb):
    c = HGRN_SUB
    ng = ng_ref[...]
    zpad = jnp.zeros((c - t, A_DK), F32)
    pad = lambda a: jnp.concatenate([a, zpad], axis=0)
    chains = []
    for h in range(A_HEADS):
        q, k, g = q_ref[h].astype(F32), k_ref[h], g_ref[h]
        v, gate = v_ref[h].astype(F32), gate_ref[h].astype(F32)
        for b in range(nb):
            sl = slice(b * t, (b + 1) * t)
            chains.append((pad(q[sl]), pad(k[sl]), pad(g[sl]), pad(v[sl]), pad(gate[sl]), s0_ref[b, h].T))
    results = _hgrn_chunks(chains, ng, c)
    for h in range(A_HEADS):
        for b in range(nb):
            sf_ref[b, h] = results[h * nb + b][1].T
        o_ref[h] = jnp.concatenate([results[h * nb + b][0][:t] for b in range(nb)], axis=0).astype(BF16)


def _hgrn_sample(hq, hk, hg, hv, hgate, ng, s0, t):
    bd = s0.shape[0]
    nb = 16 // t
    blk = pl.BlockSpec((A_HEADS, nb * t, A_DK), lambda i: (0, i, 0))
    sblk = pl.BlockSpec((nb, A_HEADS, A_DK, A_DV), lambda i: (i, 0, 0, 0))
    return pl.pallas_call(
        functools.partial(_hgrn_sample_kernel, t=t, nb=nb), grid=(bd // nb,),
        in_specs=[blk, blk, blk, blk, blk, pl.BlockSpec((1, A_DV), lambda i: (0, 0)), sblk],
        out_specs=[blk, sblk],
        out_shape=[jax.ShapeDtypeStruct((A_HEADS, bd * t, A_DV), BF16),
                   jax.ShapeDtypeStruct((bd, A_HEADS, A_DK, A_DV), F32)],
        compiler_params=pltpu.CompilerParams(dimension_semantics=("parallel",), vmem_limit_bytes=VMEM_LIMIT),
        name="hgrn_sample",
    )(hq, hk, hg, hv, hgate, ng, s0)


def _count_tiles(score_ref, ntiles, pred):
    def body(j, acc):
        ks = pl.multiple_of(j * K_TILE, K_TILE)
        hit = jnp.where(pred(score_ref[pl.ds(ks, K_TILE), :], ks), 1, 0)
        return acc + jnp.sum(hit.reshape(K_TILE // 8, 8, Q_BLOCK), axis=0)
    acc = lax.fori_loop(0, ntiles, body, jnp.zeros((8, Q_BLOCK), I32))
    return jnp.sum(acc, axis=0, keepdims=True)


def _kth_largest(count_ge, topk, shape):
    c0 = count_ge(jnp.zeros(shape, F32))
    ok = c0 >= topk
    thr = jnp.where(ok, 0, INT_MIN)
    nge = jnp.where(ok, c0, 0)

    def body(i, carry):
        thr, nge = carry
        cand = thr | jnp.left_shift(jnp.int32(1), 30 - i)
        cnt = count_ge(_key_to_float(cand))
        ok = cnt >= topk
        return jnp.where(ok, cand, thr), jnp.where(ok, cnt, nge)

    return lax.fori_loop(0, 31, body, (thr, nge))


def _dsa_prompt_kernel(ki_ref, qiT_ref, wT_ref, k_ref, qT_ref, vT_ref, gb_ref, o_ref,
                       score_scr, bias_scr, lg_scr, oT_scr, *, topk):
    i = pl.program_id(1)
    ntiles = (i * Q_BLOCK + Q_BLOCK + K_TILE - 1) // K_TILE
    qpos = i * Q_BLOCK + lax.broadcasted_iota(I32, (K_TILE, Q_BLOCK), 1)
    krow = lax.broadcasted_iota(I32, (K_TILE, Q_BLOCK), 0)

    qi = qiT_ref[0]
    wT = wT_ref[...]

    def score_body(j, carry):
        ks = pl.multiple_of(j * K_TILE, K_TILE)
        x = _dot(ki_ref[pl.ds(ks, K_TILE), :], qi)
        sc = jnp.zeros((K_TILE, Q_BLOCK), F32)
        for h in range(IDX_HEADS):
            sc = sc + jnp.maximum(x[:, h * Q_BLOCK:(h + 1) * Q_BLOCK], 0.0) * wT[h:h + 1]
        score_scr[pl.ds(ks, K_TILE), :] = jnp.where(krow + ks <= qpos, sc, -jnp.inf)
        return carry

    lax.fori_loop(0, ntiles, score_body, 0)

    count = functools.partial(_count_tiles, score_scr, ntiles)
    thr, nge = _kth_largest(lambda x: count(lambda tile, ks: tile >= x), topk, (1, Q_BLOCK))
    live = thr > KEY_NEG_INF
    thr = _key_to_float(jnp.maximum(thr, KEY_NEG_INF + 1))
    no_cut = jnp.full((1, Q_BLOCK), ntiles * K_TILE, I32)
    has_ties = jnp.max(jnp.where(live & (nge > topk), 1, 0)) > 0

    def tie_cut():
        need = topk - count(lambda tile, ks: tile > thr)
        nbits = max(1, int(math.ceil(math.log2(score_scr.shape[0] + 1))))

        def body(b, pos):
            cand = pos | jnp.left_shift(jnp.int32(1), nbits - 1 - b)
            below = count(lambda tile, ks: (tile == thr) & (krow + ks < cand))
            return jnp.where(below <= need, cand, pos)

        return lax.fori_loop(0, nbits, body, jnp.zeros((1, Q_BLOCK), I32))

    cut = lax.cond(has_ties, tie_cut, lambda: no_cut)

    natt = (i * Q_BLOCK + Q_BLOCK + ATT_TILE - 1) // ATT_TILE

    @pl.when(natt * (ATT_TILE // K_TILE) > ntiles)
    def _():
        score_scr[pl.ds(pl.multiple_of(ntiles * K_TILE, K_TILE), K_TILE), :] = jnp.full((K_TILE, Q_BLOCK), -jnp.inf, F32)

    def bias_body(j, carry):
        ks = pl.multiple_of(j * K_TILE, K_TILE)
        tile = score_scr[pl.ds(ks, K_TILE), :]
        sel = (tile > thr) | ((tile == thr) & (krow + ks < cut))
        bias_scr[pl.ds(ks, K_TILE), :] = jnp.where(sel, 0.0, -jnp.inf)
        return carry

    lax.fori_loop(0, natt * (ATT_TILE // K_TILE), bias_body, 0)

    npair = B_HEADS // 2
    z = jnp.zeros((B_DH, Q_BLOCK), BF16)
    rhs = []
    for p in range(npair):
        qp = qT_ref[p * LANES:(p + 1) * LANES, :]
        rhs.append(jnp.concatenate([jnp.concatenate([qp[:B_DH], z], axis=0),
                                    jnp.concatenate([z, qp[B_DH:]], axis=0)], axis=1))
    fold = lambda a: a.reshape(ATT_TILE // 8, 8, a.shape[1])

    def logit_body(j, mx):
        ks = pl.multiple_of(j * ATT_TILE, ATT_TILE)
        bias = bias_scr[pl.ds(ks, ATT_TILE), :]
        bias2 = jnp.concatenate([bias, bias], axis=1)
        lgs = [_dot(k_ref[pl.ds(ks, ATT_TILE), p * LANES:(p + 1) * LANES], rhs[p]) + bias2 for p in range(npair)]
        lg_scr[pl.ds(ks, ATT_TILE), :] = jnp.concatenate(lgs, axis=1)
        return tuple(jnp.maximum(m, jnp.max(fold(lg), axis=0)) for m, lg in zip(mx, lgs))

    mx = lax.fori_loop(0, natt, logit_body,
                       tuple(jnp.full((8, 2 * Q_BLOCK), -jnp.inf, F32) for _ in range(npair)))
    mx = jnp.concatenate([jnp.max(m, axis=0, keepdims=True) for m in mx], axis=1)
    oT_scr[...] = jnp.zeros_like(oT_scr)

    def value_body(j, den):
        ks = pl.multiple_of(j * ATT_TILE, ATT_TILE)
        new_den, new_acc = [], []
        for h in range(B_HEADS):
            cols = slice(h * Q_BLOCK, (h + 1) * Q_BLOCK)
            rows = slice(h * B_DH, (h + 1) * B_DH)
            e = jnp.exp(lg_scr[pl.ds(ks, ATT_TILE), cols] - mx[:, cols])
            new_den.append(den[h] + jnp.sum(fold(e), axis=0))
            new_acc.append(oT_scr[rows, :] + _dot(vT_ref[rows, pl.ds(ks, ATT_TILE)], e.astype(BF16)))
        for h in range(B_HEADS):
            oT_scr[h * B_DH:(h + 1) * B_DH, :] = new_acc[h]
        return tuple(new_den)

    den = lax.fori_loop(0, natt, value_body, tuple(jnp.zeros((8, Q_BLOCK), F32) for _ in range(B_HEADS)))
    for h in range(B_HEADS):
        rows = slice(h * B_DH, (h + 1) * B_DH)
        oT_scr[rows, :] = oT_scr[rows, :] / jnp.sum(den[h], axis=0, keepdims=True)

    o_ref[...] = (oT_scr[...].T * gb_ref[...].astype(F32)).astype(BF16)


def _dsa_prompt(kibf, qiT2, wT, kbf, qT, vT, gbs, batch, seq):
    n = batch * seq
    nq = seq // Q_BLOCK
    topk = min(TOPK_MAX, seq // 4)
    assert seq % ATT_TILE == 0 and topk <= K_TILE
    return pl.pallas_call(
        functools.partial(_dsa_prompt_kernel, topk=topk), grid=(batch, nq),
        in_specs=[
            pl.BlockSpec((seq, IDX_DIM), lambda b, i: (b, 0)),
            pl.BlockSpec((1, IDX_DIM, IDX_HEADS * Q_BLOCK), lambda b, i: (b * nq + i, 0, 0)),
            pl.BlockSpec((IDX_HEADS, Q_BLOCK), lambda b, i: (0, b * nq + i)),
            pl.BlockSpec((seq, B_WIDTH), lambda b, i: (b, 0)),
            pl.BlockSpec((B_WIDTH, Q_BLOCK), lambda b, i: (0, b * nq + i)),
            pl.BlockSpec((B_WIDTH, seq), lambda b, i: (0, b)),
            pl.BlockSpec((Q_BLOCK, B_WIDTH), lambda b, i: (b * nq + i, 0)),
        ],
        out_specs=pl.BlockSpec((Q_BLOCK, B_WIDTH), lambda b, i: (b * nq + i, 0)),
        out_shape=jax.ShapeDtypeStruct((n, B_WIDTH), BF16),
        scratch_shapes=[
            pltpu.VMEM((seq, Q_BLOCK), F32),
            pltpu.VMEM((seq, Q_BLOCK), F32),
            pltpu.VMEM((seq, B_HEADS * Q_BLOCK), F32),
            pltpu.VMEM((B_WIDTH, Q_BLOCK), F32),
        ],
        compiler_params=pltpu.CompilerParams(dimension_semantics=("parallel", "arbitrary"),
                                             vmem_limit_bytes=VMEM_LIMIT),
        name="dsa_prompt",
    )(kibf, qiT2, wT, kbf, qT, vT, gbs)


def _dsa_sample_kernel(pt_ref, qi_ref, wrep_ref, qbd_ref, kin_ref, kn_ref, vn_ref, gb_ref,
                       cki_hbm, ck_hbm, cv_hbm, o_ref, ki_buf, k_buf, v_buf, score_scr, sem_i, sem_k, sem_v,
                       *, t, npages, topk, depth, group):
    b = pl.program_id(0)
    page = ki_buf.shape[2]
    total = (npages + 1) * page
    lane = lax.broadcasted_iota(I32, (t, page), 1)
    qrow = lax.broadcasted_iota(I32, (t, page), 0)

    def ki_copy(p):
        return pltpu.make_async_copy(cki_hbm.at[pt_ref[b, p], 0], ki_buf.at[p], sem_i.at[p])

    def k_copy(p, slot):
        return pltpu.make_async_copy(ck_hbm.at[pt_ref[b, p], 0], k_buf.at[slot], sem_k.at[slot])

    def v_copy(p, slot):
        return pltpu.make_async_copy(cv_hbm.at[pt_ref[b, p], 0], v_buf.at[slot], sem_v.at[slot])

    def start_ki(p, carry):
        ki_copy(p).start()
        return carry

    lax.fori_loop(0, npages, start_ki, 0)
    for s in range(depth):
        k_copy(s, s).start()
        v_copy(s, s).start()

    def scores(kidx_t):
        x = _dot(qi_ref[0], kidx_t)
        xw = jnp.maximum(x, 0.0) * wrep_ref[0]
        sc = xw[0:t]
        for h in range(1, IDX_HEADS):
            sc = sc + xw[h * t:(h + 1) * t]
        return sc

    def score_pages(i, carry):
        for g in range(group):
            ki_copy(i * group + g).wait()
        sc = [scores(ki_buf[i * group + g].astype(BF16)) for g in range(group)]
        score_scr[:, pl.ds(pl.multiple_of(i * group * page, group * page), group * page)] = jnp.concatenate(sc, axis=1)
        return carry

    lax.fori_loop(0, npages // group, score_pages, 0)

    score_scr[:, npages * page:] = jnp.where(lane <= qrow, scores(kin_ref[0]), -jnp.inf)

    def count(pred):
        hit = jnp.where(pred(score_scr[...]), 1, 0)
        part = hit[:, 0:page]
        for c in range(1, npages + 1):
            part = part + hit[:, c * page:(c + 1) * page]
        return jnp.sum(part, axis=1, keepdims=True)

    thr, nge = _kth_largest(lambda x: count(lambda s: s >= x), topk, (t, 1))
    live = thr > KEY_NEG_INF
    thr = _key_to_float(jnp.maximum(thr, KEY_NEG_INF + 1))
    has_ties = jnp.max(jnp.where(live & (nge > topk), 1, 0)) > 0
    pos_all = lax.broadcasted_iota(I32, (t, total), 1)

    def tie_cut():
        need = topk - count(lambda s: s > thr)
        nbits = max(1, int(math.ceil(math.log2(total + 1))))

        def pos_step(j, pos):
            cand = pos | jnp.left_shift(jnp.int32(1), nbits - 1 - j)
            below = count(lambda s: (s == thr) & (pos_all < cand))
            return jnp.where(below <= need, cand, pos)

        return lax.fori_loop(0, nbits, pos_step, jnp.zeros((t, 1), I32))

    cut = lax.cond(has_ties, tie_cut, lambda: jnp.full((t, 1), total, I32))

    qbd = qbd_ref[0]

    def attend(kv_pages, carry):
        m_old, l_old, acc = carry
        lgs = []
        for k_t, _, ks in kv_pages:
            tile = score_scr[:, pl.ds(ks, page)]
            sel = (tile > thr) | ((tile == thr) & (lane + ks < cut))
            sel = jnp.concatenate([sel.astype(I32)] * B_HEADS, axis=0) > 0
            lgs.append(jnp.where(sel, _dot(qbd, k_t), -jnp.inf))
        lg = jnp.concatenate(lgs, axis=1)
        m_new = jnp.maximum(m_old, jnp.max(lg, axis=1, keepdims=True))
        m_safe = jnp.where(m_new == -jnp.inf, 0.0, m_new)
        alpha = jnp.exp(m_old - m_safe)
        e = jnp.exp(lg - m_safe)
        l_new = alpha * l_old + jnp.sum(e, axis=1, keepdims=True)
        acc = alpha * acc
        for g, (_, v_t, _) in enumerate(kv_pages):
            acc = acc + _dot_nt(e[:, g * page:(g + 1) * page].astype(BF16), v_t)
        return m_new, l_new, acc

    def attend_pages(i, carry):
        pages = []
        for g in range(group):
            p = i * group + g
            slot = p % depth
            k_copy(p, slot).wait()
            v_copy(p, slot).wait()
            pages.append((k_buf[slot].astype(BF16), v_buf[slot].astype(BF16), pl.multiple_of(p * page, page)))
        carry = attend(pages, carry)
        for g in range(group):
            p = i * group + g

            @pl.when(p + depth < npages)
            def _():
                k_copy(p + depth, p % depth).start()
                v_copy(p + depth, p % depth).start()

        return carry

    nrow = B_HEADS * t
    carry = (jnp.full((nrow, 1), -jnp.inf, F32), jnp.zeros((nrow, 1), F32), jnp.zeros((nrow, B_WIDTH), F32))
    carry = lax.fori_loop(0, npages // group, attend_pages, carry)
    _, den, acc = attend([(kn_ref[0], vn_ref[0], npages * page)], carry)
    o = acc / den
    col = lax.broadcasted_iota(I32, (t, B_WIDTH), 1)
    out = jnp.zeros((t, B_WIDTH), F32)
    for h in range(B_HEADS):
        out = jnp.where((col >= h * B_DH) & (col < (h + 1) * B_DH), o[h * t:(h + 1) * t], out)
    o_ref[0] = (out * gb_ref[0].astype(F32)).astype(BF16)


def _dsa_sample(page_table, qi_rows, wrep, qbd, kin_t, kn_t, vn_t, gbs, cache_kidx_t, cache_k_t, cache_v_t, t):
    bd, npages = page_table.shape
    page = cache_kidx_t.shape[3]
    total = npages * page + t
    topk = min(TOPK_MAX, total // 4)
    group = max(g for g in range(1, SAMPLE_PAGE_GROUP + 1) if npages % g == 0)
    depth = group * max(1, min(SAMPLE_DMA_DEPTH // group, npages // group))
    per_b = lambda b, pt: (b, 0, 0)
    hbm = pl.BlockSpec(memory_space=pl.ANY)
    grid_spec = pltpu.PrefetchScalarGridSpec(
        num_scalar_prefetch=1, grid=(bd,),
        in_specs=[
            pl.BlockSpec((1, IDX_HEADS * t, IDX_DIM), per_b),
            pl.BlockSpec((1, IDX_HEADS * t, page), per_b),
            pl.BlockSpec((1, B_HEADS * t, B_WIDTH), per_b),
            pl.BlockSpec((1, IDX_DIM, page), per_b),
            pl.BlockSpec((1, B_WIDTH, page), per_b),
            pl.BlockSpec((1, B_WIDTH, page), per_b),
            pl.BlockSpec((1, t, B_WIDTH), per_b),
            hbm, hbm, hbm,
        ],
        out_specs=pl.BlockSpec((1, t, B_WIDTH), per_b),
        scratch_shapes=[
            pltpu.VMEM((npages, IDX_DIM, page), F32),
            pltpu.VMEM((depth, B_WIDTH, page), F32),
            pltpu.VMEM((depth, B_WIDTH, page), F32),
            pltpu.VMEM((t, (npages + 1) * page), F32),
            pltpu.SemaphoreType.DMA((npages,)),
            pltpu.SemaphoreType.DMA((depth,)),
            pltpu.SemaphoreType.DMA((depth,)),
        ],
    )
    return pl.pallas_call(
        functools.partial(_dsa_sample_kernel, t=t, npages=npages, topk=topk, depth=depth, group=group),
        grid_spec=grid_spec,
        out_shape=jax.ShapeDtypeStruct((bd, t, B_WIDTH), BF16),
        compiler_params=pltpu.CompilerParams(dimension_semantics=("arbitrary",), vmem_limit_bytes=VMEM_LIMIT),
        name="dsa_sample",
    )(page_table, qi_rows, wrep, qbd, kin_t, kn_t, vn_t, gbs, cache_kidx_t, cache_k_t, cache_v_t)


def _merge_kernel(x_ref, ma_ref, mb_ref, w_ref, g_ref, b_ref, y_ref, *, alpha):
    mix = jnp.concatenate([ma_ref[h] for h in range(A_HEADS)] + [mb_ref[...]], axis=1)
    y = alpha * x_ref[...] + _dot(mix, w_ref[...])
    mu = jnp.mean(y, axis=-1, keepdims=True)
    d = y - mu
    var = jnp.mean(d * d, axis=-1, keepdims=True)
    y_ref[...] = d * lax.rsqrt(var + LN_EPS) * g_ref[...] + b_ref[...]


def _merge(x, mix_a, mix_b, w_out, ln_g, ln_b, alpha, tm):
    n, dm = x.shape
    return pl.pallas_call(
        functools.partial(_merge_kernel, alpha=alpha), grid=(n // tm,),
        in_specs=[
            pl.BlockSpec((tm, dm), lambda i: (i, 0)),
            pl.BlockSpec((A_HEADS, tm, A_DV), lambda i: (0, i, 0)),
            pl.BlockSpec((tm, B_WIDTH), lambda i: (i, 0)),
            pl.BlockSpec((A_WIDTH + B_WIDTH, dm), lambda i: (0, 0)),
            pl.BlockSpec((1, dm), lambda i: (0, 0)),
            pl.BlockSpec((1, dm), lambda i: (0, 0)),
        ],
        out_specs=pl.BlockSpec((tm, dm), lambda i: (i, 0)),
        out_shape=jax.ShapeDtypeStruct((n, dm), F32),
        compiler_params=pltpu.CompilerParams(dimension_semantics=("parallel",), vmem_limit_bytes=VMEM_LIMIT),
        name="merge",
    )(x, mix_a, mix_b, w_out, ln_g, ln_b)


def _split_weights(w_in_l):
    offs = np.cumsum([0, 512, 512, 512, 512, 512, 512, 512, 512, IDX_HEADS * IDX_DIM, IDX_DIM, IDX_HEADS])
    col = lambda i: w_in_l[:, offs[i]:offs[i + 1]]
    qa, fa, ia, ga, qb, kb, vb, gb, qi, ki, wi = (col(i) for i in range(11))
    pad = jnp.zeros((w_in_l.shape[0], LANES - IDX_DIM), w_in_l.dtype)
    wn = jnp.concatenate([qa, fa, ia, ga, kb, gb, ki, pad], axis=1).astype(BF16)
    wt = jnp.concatenate([qb, kb, vb, qi, wi, ki], axis=1).T.astype(BF16)
    return wn, wt


def _layer(xp, xs, cache_k, cache_v, cache_kidx, s0_sample, page_table, w_in_l, lb_l, norm_g_l, kn_g_l, kn_b_l,
           w_out_l, ln_g_l, ln_b_l, alpha):
    b, l, dm = xp.shape
    bd, t, _ = xs.shape
    npages, page = page_table.shape[1], cache_k.shape[2]
    assert page == Q_BLOCK and l % page == 0
    past = npages * page
    wn, wt = _split_weights(w_in_l)
    lb = lb_l.reshape(1, A_WIDTH)
    ng = norm_g_l.reshape(1, A_DV)
    w_out_b = w_out_l.astype(BF16)
    lng, lnb = ln_g_l.reshape(1, dm), ln_b_l.reshape(1, dm)

    tm = 256
    xp2 = xp.reshape(b * l, dm)
    (hq, hk, hg, hv, hgate, kbf, gbs, kibf, qT, vT, qiT2, wT, k_p, v_p, ki_p) = _project(
        xp2, jnp.arange(l, dtype=I32), wn, wt, lb, kn_g_l, kn_b_l, tm)
    mix_a, s_p = _hgrn_prompt(hq, hk, hg, hv, hgate, ng, b, l, min(l, 512))
    mix_b = _dsa_prompt(kibf, qiT2, wT, kbf, qT, vT, gbs, b, l)
    y_p = _merge(xp2, mix_a, mix_b, w_out_b, lng, lnb, alpha, 512).reshape(b, l, dm)

    ns = bd * t
    xs2 = xs.reshape(ns, dm)
    pos_s = past + (jnp.arange(ns, dtype=I32) % t)
    (hq, hk, hg, hv, hgate, kbf, gbs, kibf, qT, vT, qiT2, wT, k_s, v_s, ki_s) = _project(
        xs2, pos_s, wn, wt, lb, kn_g_l, kn_b_l, ns)
    mix_a, s_s = _hgrn_sample(hq, hk, hg, hv, hgate, ng, s0_sample, t)
    qi_nat = qiT2.reshape(ns // Q_BLOCK, IDX_DIM, IDX_HEADS, Q_BLOCK).transpose(0, 3, 2, 1)
    qi_rows = qi_nat.reshape(bd, t, IDX_HEADS, IDX_DIM).transpose(0, 2, 1, 3).reshape(bd, IDX_HEADS * t, IDX_DIM)
    w_rows = wT.T.reshape(bd, t, IDX_HEADS).transpose(0, 2, 1).reshape(bd, IDX_HEADS * t, 1)
    wrep = jnp.broadcast_to(w_rows, (bd, IDX_HEADS * t, page))
    q_nat = qT.T.reshape(bd, 1, t, B_HEADS, B_DH)
    eye = jnp.eye(B_HEADS, dtype=BF16).reshape(1, B_HEADS, 1, B_HEADS, 1)
    qbd = (q_nat * eye).reshape(bd, B_HEADS * t, B_WIDTH)

    def per_seq(pages):
        feat = pages.shape[1]
        a = pages.transpose(1, 0, 2).reshape(feat, bd, t).transpose(1, 0, 2).astype(BF16)
        return jnp.concatenate([a, jnp.zeros((bd, feat, page - t), BF16)], axis=2)

    token_minor = lambda c: jnp.moveaxis(c, 2, -1).reshape(c.shape[0], c.shape[1], -1, page)
    mix_b = _dsa_sample(page_table, qi_rows, wrep, qbd, per_seq(ki_s), per_seq(k_s), per_seq(v_s),
                        gbs.reshape(bd, t, B_WIDTH), token_minor(cache_kidx), token_minor(cache_k),
                        token_minor(cache_v), t).reshape(ns, B_WIDTH)
    y_s = _merge(xs2, mix_a, mix_b, w_out_b, lng, lnb, alpha, ns).reshape(bd, t, dm)

    return (y_p, y_s, k_p, v_p, ki_p, s_p, k_s, v_s, ki_s, s_s)


def kernel(x_prompt, x_sample, cache_k, cache_v, cache_kidx, state_hgrn, page_table, w_in, hgrn_lb_logits,
           hgrn_norm_g, idx_norm_g, idx_norm_b, w_out, ln_g, ln_b):
    depth = w_in.shape[0]
    assert depth == 1, "one layer per step"
    b, l, _ = x_prompt.shape
    bd, t, _ = x_sample.shape
    page = cache_k.shape[2]
    alpha = (2.0 * depth) ** 0.25
    lbs = jnp.cumsum(jax.nn.softmax(hgrn_lb_logits.astype(F32), axis=0), axis=0)[:depth]
    (y_p, y_s, k_p, v_p, ki_p, s_p, k_s, v_s, ki_s, s_s) = _layer(
        x_prompt, x_sample, cache_k, cache_v, cache_kidx, state_hgrn[0], page_table, w_in[0], lbs[0],
        hgrn_norm_g[0], idx_norm_g[0], idx_norm_b[0], w_out[0], ln_g[0], ln_b[0], alpha)
    nat_p = lambda pg, *f: jnp.moveaxis(pg.reshape(b, l // page, 1, *f, page), -1, 3)
    nat_s = lambda pg, *f: pg.transpose(0, 2, 1).reshape(bd, 1, t, *f)
    return (
        y_p, y_s,
        nat_p(k_p, B_HEADS, B_DH), nat_p(v_p, B_HEADS, B_DH), nat_p(ki_p, IDX_DIM),
        s_p[None],
        nat_s(k_s, B_HEADS, B_DH), nat_s(v_s, B_HEADS, B_DH), nat_s(ki_s, IDX_DIM),
        s_s[None],
    )
```

```python
import functools
import math

import jax
import jax.numpy as jnp
import numpy as np
from jax import lax
from jax.experimental import pallas as pl
from jax.experimental.pallas import tpu as pltpu

F32 = jnp.float32
BF16 = jnp.bfloat16
I32 = jnp.int32

A_HEADS = 4
A_DK = 128
A_DV = 128
B_HEADS = 8
B_DH = 64
IDX_HEADS = 16
IDX_DIM = 64
TOPK_MAX = 256
ROPE_THETA = 500000.0
ROT = 16
ROT_HALF = ROT // 2
LN_EPS = 1e-5
Q_BLOCK = 128
K_TILE = 256
ATT_TILE = 512
HGRN_CHUNK = 128
HGRN_SUB = 16
LANES = 128
VMEM_LIMIT = 56 * 1024 * 1024
SAMPLE_DMA_DEPTH = 16
SAMPLE_PAGE_GROUP = 4

A_WIDTH = A_HEADS * A_DV
B_WIDTH = B_HEADS * B_DH
NAT_WIDTH = 6 * 512 + LANES
TR_ROWS = 3 * 512 + IDX_HEADS * IDX_DIM + IDX_HEADS + IDX_DIM
INT_MIN = -(2 ** 31)
KEY_NEG_INF = -0x7F800000


def _dot(a, b):
    return jnp.dot(a, b, preferred_element_type=F32)


def _dot_nt(a, b):
    return lax.dot_general(a, b, (((1,), (1,)), ((), ())), preferred_element_type=F32)


def _dot_tn(a, b):
    return lax.dot_general(a, b, (((0,), (0,)), ((), ())), preferred_element_type=F32)


def _silu(x):
    return x * jax.nn.sigmoid(x)


def _key_to_float(key):
    m = key >> 31
    mag = (key ^ m) - m
    return pltpu.bitcast(mag | (m & INT_MIN), F32)


def _proj_kernel(x_ref, wn_ref, wt_ref, lb_ref, kng_ref, knb_ref, kngc_ref, knbc_ref, cn_ref, sa_ref, sb_ref,
                 ct_ref, st_ref,
                 hq_ref, hk_ref, hg_ref, hv_ref, hgate_ref, kbf_ref, gb_ref, kibf_ref,
                 qT_ref, vT_ref, qiT_ref, wT_ref, kTp_ref, vTp_ref, kiTp_ref):
    tm = x_ref.shape[0]
    xb = x_ref[...].astype(BF16)

    def nat(col, width=512):
        return _dot(xb, wn_ref[:, col:col + width])

    lb = lb_ref[...]
    qa = nat(0)
    hq = _silu(qa)
    fa = nat(512)
    hg = jnp.log(lb + (1.0 - lb) * jax.nn.sigmoid(fa))
    hk = (1.0 - lb) * jax.nn.sigmoid(-fa)
    ia = nat(1024)
    hgate = _silu(nat(1536))
    for h in range(A_HEADS):
        sl = slice(h * A_DK, (h + 1) * A_DK)
        hq_ref[h] = hq[:, sl].astype(BF16)
        hk_ref[h] = hk[:, sl]
        hg_ref[h] = hg[:, sl]
        hv_ref[h] = ia[:, sl].astype(BF16)
        hgate_ref[h] = hgate[:, sl].astype(BF16)

    cn, sa, sb = cn_ref[...], sa_ref[...], sb_ref[...]

    def rope_nat(xc):
        return xc * cn + pltpu.roll(xc, LANES - ROT_HALF, 1) * sa + pltpu.roll(xc, ROT_HALF, 1) * sb

    kb = nat(2048)
    kbf_ref[...] = jnp.concatenate(
        [rope_nat(kb[:, c * LANES:(c + 1) * LANES]) for c in range(B_WIDTH // LANES)], axis=1).astype(BF16)
    gb_ref[...] = _silu(nat(2560)).astype(BF16)

    kic = nat(3072, LANES)
    lane = lax.broadcasted_iota(I32, (tm, LANES), 1)
    inb = lane < IDX_DIM
    mu = jnp.sum(jnp.where(inb, kic, 0.0), axis=-1, keepdims=True) * (1.0 / IDX_DIM)
    d = jnp.where(inb, kic - mu, 0.0)
    var = jnp.sum(d * d, axis=-1, keepdims=True) * (1.0 / IDX_DIM)
    kin = d * lax.rsqrt(var + LN_EPS) * kng_ref[...] + knb_ref[...]
    kibf_ref[...] = rope_nat(kin)[:, :IDX_DIM].astype(BF16)

    ct, st = ct_ref[...], st_ref[...]
    npage = tm // Q_BLOCK

    def tr(row, height):
        return _dot_nt(wt_ref[row:row + height, :], xb)

    def rope_tr(blk):
        x1, x2 = blk[0:ROT_HALF], blk[ROT_HALF:ROT]
        return jnp.concatenate([x1 * ct - x2 * st, x1 * st + x2 * ct, blk[ROT:]], axis=0)

    qbT = tr(0, B_WIDTH)
    for h in range(B_HEADS):
        blk = rope_tr(qbT[h * B_DH:(h + 1) * B_DH])
        qT_ref[h * B_DH:(h + 1) * B_DH, :] = (blk * (B_DH ** -0.5)).astype(BF16)
    kbT = tr(B_WIDTH, B_WIDTH)
    for h in range(B_HEADS):
        blk = rope_tr(kbT[h * B_DH:(h + 1) * B_DH])
        for pg in range(npage):
            kTp_ref[pg, h * B_DH:(h + 1) * B_DH, :] = blk[:, pg * Q_BLOCK:(pg + 1) * Q_BLOCK]
    vbT = tr(2 * B_WIDTH, B_WIDTH)
    vT_ref[...] = vbT.astype(BF16)
    for pg in range(npage):
        vTp_ref[pg] = vbT[:, pg * Q_BLOCK:(pg + 1) * Q_BLOCK]
    row = 3 * B_WIDTH
    qiT = tr(row, IDX_HEADS * IDX_DIM)
    for h in range(IDX_HEADS):
        blk = rope_tr(qiT[h * IDX_DIM:(h + 1) * IDX_DIM]).astype(BF16)
        for pg in range(npage):
            qiT_ref[pg, :, h * Q_BLOCK:(h + 1) * Q_BLOCK] = blk[:, pg * Q_BLOCK:(pg + 1) * Q_BLOCK]
    row += IDX_HEADS * IDX_DIM
    wT_ref[...] = tr(row, IDX_HEADS) * (IDX_HEADS ** -0.5 * IDX_DIM ** -0.5)
    row += IDX_HEADS
    kiT = tr(row, IDX_DIM)
    muT = jnp.mean(kiT, axis=0, keepdims=True)
    dT = kiT - muT
    varT = jnp.mean(dT * dT, axis=0, keepdims=True)
    kiT = rope_tr(dT * lax.rsqrt(varT + LN_EPS) * kngc_ref[...] + knbc_ref[...])
    for pg in range(npage):
        kiTp_ref[pg] = kiT[:, pg * Q_BLOCK:(pg + 1) * Q_BLOCK]


def _rope_tables(pos):
    inv = ROPE_THETA ** (-jnp.arange(ROT_HALF, dtype=F32) / ROT_HALF)
    ang = pos.astype(F32)[:, None] * inv[None, :]
    cos, sin = jnp.cos(ang), jnp.sin(ang)
    p = pos.shape[0]
    one = jnp.ones((p, B_DH - ROT), F32)
    zero8 = jnp.zeros((p, ROT_HALF), F32)
    zero = jnp.zeros((p, B_DH - ROT), F32)
    cn = jnp.concatenate([cos, cos, one], axis=1)
    sa = jnp.concatenate([-sin, zero8, zero], axis=1)
    sb = jnp.concatenate([zero8, sin, zero], axis=1)
    tile2 = lambda a: jnp.concatenate([a, a], axis=1)
    return tile2(cn), tile2(sa), tile2(sb), cos.T, sin.T


def _project(x, pos, wn, wt, lb, kn_g, kn_b, tm):
    n, dm = x.shape
    p = pos.shape[0]
    nper = p // tm
    cn, sa, sb, ct, st = _rope_tables(pos)
    padl = lambda a: jnp.concatenate([a, jnp.zeros((LANES - IDX_DIM,), a.dtype)]).reshape(1, LANES)
    grid = (n // tm,)
    row = lambda i: (i, 0)
    full = lambda i: (0, 0)
    head = lambda i: (0, i, 0)
    page = lambda i: (i, 0, 0)
    per = lambda i: (i % nper, 0)
    perT = lambda i: (0, i % nper)
    colT = lambda i: (0, i)
    in_specs = [
        pl.BlockSpec((tm, dm), row),
        pl.BlockSpec((dm, NAT_WIDTH), full, pipeline_mode=pl.Buffered(1)),
        pl.BlockSpec((TR_ROWS, dm), full, pipeline_mode=pl.Buffered(1)),
        pl.BlockSpec((1, A_WIDTH), full),
        pl.BlockSpec((1, LANES), full),
        pl.BlockSpec((1, LANES), full),
        pl.BlockSpec((IDX_DIM, 1), full),
        pl.BlockSpec((IDX_DIM, 1), full),
        pl.BlockSpec((tm, LANES), per),
        pl.BlockSpec((tm, LANES), per),
        pl.BlockSpec((tm, LANES), per),
        pl.BlockSpec((ROT_HALF, tm), perT),
        pl.BlockSpec((ROT_HALF, tm), perT),
    ]
    npage = tm // Q_BLOCK
    hshape = lambda dt: jax.ShapeDtypeStruct((A_HEADS, n, A_DK), dt)
    out_shape = [
        hshape(BF16), hshape(F32), hshape(F32), hshape(BF16), hshape(BF16),
        jax.ShapeDtypeStruct((n, B_WIDTH), BF16), jax.ShapeDtypeStruct((n, B_WIDTH), BF16),
        jax.ShapeDtypeStruct((n, IDX_DIM), BF16),
        jax.ShapeDtypeStruct((B_WIDTH, n), BF16), jax.ShapeDtypeStruct((B_WIDTH, n), BF16),
        jax.ShapeDtypeStruct((n // Q_BLOCK, IDX_DIM, IDX_HEADS * Q_BLOCK), BF16),
        jax.ShapeDtypeStruct((IDX_HEADS, n), F32),
        jax.ShapeDtypeStruct((n // Q_BLOCK, B_WIDTH, Q_BLOCK), F32),
        jax.ShapeDtypeStruct((n // Q_BLOCK, B_WIDTH, Q_BLOCK), F32),
        jax.ShapeDtypeStruct((n // Q_BLOCK, IDX_DIM, Q_BLOCK), F32),
    ]
    hspec = pl.BlockSpec((A_HEADS, tm, A_DK), head)
    out_specs = [
        hspec, hspec, hspec, hspec, hspec,
        pl.BlockSpec((tm, B_WIDTH), row), pl.BlockSpec((tm, B_WIDTH), row),
        pl.BlockSpec((tm, IDX_DIM), row),
        pl.BlockSpec((B_WIDTH, tm), colT), pl.BlockSpec((B_WIDTH, tm), colT),
        pl.BlockSpec((npage, IDX_DIM, IDX_HEADS * Q_BLOCK), page),
        pl.BlockSpec((IDX_HEADS, tm), colT),
        pl.BlockSpec((npage, B_WIDTH, Q_BLOCK), page),
        pl.BlockSpec((npage, B_WIDTH, Q_BLOCK), page),
        pl.BlockSpec((npage, IDX_DIM, Q_BLOCK), page),
    ]
    return pl.pallas_call(
        _proj_kernel, grid=grid, in_specs=in_specs, out_specs=out_specs, out_shape=out_shape,
        compiler_params=pltpu.CompilerParams(dimension_semantics=("parallel",), vmem_limit_bytes=VMEM_LIMIT),
        name="proj",
    )(x, wn, wt, lb, padl(kn_g), padl(kn_b), kn_g.reshape(IDX_DIM, 1), kn_b.reshape(IDX_DIM, 1),
      cn, sa, sb, ct, st)


def _hgrn_cumsum(g, c):
    if c == LANES:
        ri = lax.broadcasted_iota(I32, (c, c), 0)
        ci = lax.broadcasted_iota(I32, (c, c), 1)
        tri = jnp.where(ci <= ri, 1.0, 0.0).astype(BF16)
        g1 = g.astype(BF16)
        e1 = g - g1.astype(F32)
        g2 = e1.astype(BF16)
        g3 = (e1 - g2.astype(F32)).astype(BF16)
        return _dot(tri, g1) + _dot(tri, g2) + _dot(tri, g3)
    ri = lax.broadcasted_iota(I32, (c, A_DK), 0)
    cum = jnp.zeros((c, A_DK), F32)
    for s in range(c):
        cum = cum + jnp.where(ri >= s, g[s:s + 1], 0.0)
    return cum


def _hgrn_chunks(chains, ng, c):
    cums = [_hgrn_cumsum(g, c) for _, _, g, _, _, _ in chains]
    return [_hgrn_chunk(q, k, cum, v, gate, ng, st, c) for (q, k, _, v, gate, st), cum in zip(chains, cums)]


def _hgrn_chunk(q, k, cum, v, gate, ng, st, c):
    r = HGRN_SUB
    nsub = c // r
    stb = st.astype(BF16)
    o = _dot_nt((q * jnp.exp(cum)).astype(BF16), stb)

    rowi = lax.broadcasted_iota(I32, (r, A_DK), 0)
    lane = lax.broadcasted_iota(I32, (r, LANES), 1)
    prods = []
    for i in range(nsub):
        qs, cs = q[i * r:(i + 1) * r], cum[i * r:(i + 1) * r]
        for s in range(r):
            row = i * r + s
            p = (qs * k[row:row + 1]) * jnp.exp(cs - cum[row:row + 1])
            prods.append(jnp.where(rowi >= s, p, 0.0).astype(BF16))
    if nsub % 2 == 0:
        half = len(prods) // 2
        both = jnp.concatenate([jnp.concatenate(prods[:half], axis=0), jnp.concatenate(prods[half:], axis=0)], axis=1)
        wr = lax.broadcasted_iota(I32, (2 * A_DK, 2 * LANES), 0) < A_DK
        wc = lax.broadcasted_iota(I32, (2 * A_DK, 2 * LANES), 1) < LANES
        sums = _dot(both, jnp.where(wr == wc, 1.0, 0.0).astype(BF16))
        rsum = lambda row: (sums[row * r:(row + 1) * r, :LANES] if row < half
                            else sums[(row - half) * r:(row - half + 1) * r, LANES:])
    else:
        sums = _dot(jnp.concatenate(prods, axis=0), jnp.ones((A_DK, LANES), BF16))
        rsum = lambda row: sums[row * r:(row + 1) * r]
    sc_rows = []
    for i in range(nsub):
        sci = jnp.zeros((r, LANES), F32)
        for s in range(r):
            row = i * r + s
            sci = jnp.where(lane == row, rsum(row), sci)
        if i > 0:
            ref = cum[i * r - 1:i * r]
            qt = (q[i * r:(i + 1) * r] * jnp.exp(cum[i * r:(i + 1) * r] - ref)).astype(BF16)
            kt = (k[:i * r] * jnp.exp(ref - cum[:i * r])).astype(BF16)
            kt = jnp.concatenate([kt, jnp.zeros((LANES - i * r, A_DK), BF16)], axis=0)
            sci = sci + _dot_nt(qt, kt)
        sc_rows.append(sci)
    sc = jnp.concatenate(sc_rows, axis=0).astype(BF16)
    vb = v.astype(BF16)
    last = cum[c - 1:c]
    kh = (k * jnp.exp(last - cum)).astype(BF16)
    if c < LANES:
        zpad = jnp.zeros((LANES - c, A_DK), BF16)
        vb = jnp.concatenate([vb, zpad], axis=0)
        kh = jnp.concatenate([kh, zpad], axis=0)
    o = o + _dot(sc, vb)
    st_new = st * jnp.exp(last) + _dot_tn(vb, kh)
    ms = jnp.mean(o * o, axis=-1, keepdims=True)
    return o * lax.rsqrt(ms + LN_EPS) * ng * gate, st_new


def _hgrn_prompt_kernel(q_ref, k_ref, g_ref, v_ref, gate_ref, ng_ref, o_ref, sf_ref, st_scr):
    c = HGRN_CHUNK
    nchunk = q_ref.shape[1] // c
    tb = pl.program_id(1)

    @pl.when(tb == 0)
    def _():
        st_scr[...] = jnp.zeros_like(st_scr)

    ng = ng_ref[...]

    def body(j, carry):
        sl = pl.ds(pl.multiple_of(j * c, c), c)
        loaded = [(q_ref[h, sl, :].astype(F32), k_ref[h, sl, :], g_ref[h, sl, :], v_ref[h, sl, :].astype(F32),
                   gate_ref[h, sl, :].astype(F32), st_scr[h]) for h in range(A_HEADS)]
        for h, (o, st_new) in enumerate(_hgrn_chunks(loaded, ng, c)):
            st_scr[h] = st_new
            o_ref[h, sl, :] = o.astype(BF16)
        return carry

    lax.fori_loop(0, nchunk, body, 0)

    @pl.when(tb == pl.num_programs(1) - 1)
    def _():
        for h in range(A_HEADS):
            sf_ref[0, h] = st_scr[h].T


def _hgrn_prompt(hq, hk, hg, hv, hgate, ng, batch, seq, tb):
    n = batch * seq
    nt = seq // tb
    blk = pl.BlockSpec((A_HEADS, tb, A_DK), lambda b, t: (0, b * nt + t, 0))
    return pl.pallas_call(
        _hgrn_prompt_kernel, grid=(batch, nt),
        in_specs=[blk, blk, blk, blk, blk, pl.BlockSpec((1, A_DV), lambda b, t: (0, 0))],
        out_specs=[blk, pl.BlockSpec((1, A_HEADS, A_DK, A_DV), lambda b, t: (b, 0, 0, 0))],
        out_shape=[jax.ShapeDtypeStruct((A_HEADS, n, A_DV), BF16),
                   jax.ShapeDtypeStruct((batch, A_HEADS, A_DK, A_DV), F32)],
        scratch_shapes=[pltpu.VMEM((A_HEADS, A_DV, A_DK), F32)],
        compiler_params=pltpu.CompilerParams(dimension_semantics=("parallel", "arbitrary"),
                                             vmem_limit_bytes=VMEM_LIMIT),
        name="hgrn_prompt",
    )(hq, hk, hg, hv, hgate, ng)


def _hgrn_sample_kernel(q_ref, k_ref, g_ref, v_ref, gate_ref, ng_ref, s0_ref, o_ref, sf_ref, *, t, nb):
    c = HGRN_SUB
    ng = ng_ref[...]
    zpad = jnp.zeros((c - t, A_DK), F32)
    pad = lambda a: jnp.concatenate([a, zpad], axis=0)
    chains = []
    for h in range(A_HEADS):
        q, k, g = q_ref[h].astype(F32), k_ref[h], g_ref[h]
        v, gate = v_ref[h].astype(F32), gate_ref[h].astype(F32)
        for b in range(nb):
            sl = slice(b * t, (b + 1) * t)
            chains.append((pad(q[sl]), pad(k[sl]), pad(g[sl]), pad(v[sl]), pad(gate[sl]), s0_ref[b, h].T))
    results = _hgrn_chunks(chains, ng, c)
    for h in range(A_HEADS):
        for b in range(nb):
            sf_ref[b, h] = results[h * nb + b][1].T
        o_ref[h] = jnp.concatenate([results[h * nb + b][0][:t] for b in range(nb)], axis=0).astype(BF16)


def _hgrn_sample(hq, hk, hg, hv, hgate, ng, s0, t):
    bd = s0.shape[0]
    nb = 16 // t
    blk = pl.BlockSpec((A_HEADS, nb * t, A_DK), lambda i: (0, i, 0))
    sblk = pl.BlockSpec((nb, A_HEADS, A_DK, A_DV), lambda i: (i, 0, 0, 0))
    return pl.pallas_call(
        functools.partial(_hgrn_sample_kernel, t=t, nb=nb), grid=(bd // nb,),
        in_specs=[blk, blk, blk, blk, blk, pl.BlockSpec((1, A_DV), lambda i: (0, 0)), sblk],
        out_specs=[blk, sblk],
        out_shape=[jax.ShapeDtypeStruct((A_HEADS, bd * t, A_DV), BF16),
                   jax.ShapeDtypeStruct((bd, A_HEADS, A_DK, A_DV), F32)],
        compiler_params=pltpu.CompilerParams(dimension_semantics=("parallel",), vmem_limit_bytes=VMEM_LIMIT),
        name="hgrn_sample",
    )(hq, hk, hg, hv, hgate, ng, s0)


def _count_tiles(score_ref, ntiles, pred):
    def body(j, acc):
        ks = pl.multiple_of(j * ATT_TILE, ATT_TILE)
        hit = jnp.where(pred(score_ref[pl.ds(ks, ATT_TILE), :], ks), 1, 0)
        return acc + jnp.sum(hit.reshape(ATT_TILE // 8, 8, Q_BLOCK), axis=0)
    acc = lax.fori_loop(0, ntiles, body, jnp.zeros((8, Q_BLOCK), I32))
    return jnp.sum(acc, axis=0, keepdims=True)


def _kth_largest(count_ge, topk, shape, two_bits=False):
    c0 = count_ge(jnp.zeros(shape, F32))
    ok = c0 >= topk
    thr = jnp.where(ok, 0, INT_MIN)
    nge = jnp.where(ok, c0, 0)

    def place(bit, carry):
        thr, nge = carry
        cand = thr | bit
        cnt = count_ge(_key_to_float(cand))
        ok = cnt >= topk
        return jnp.where(ok, cand, thr), jnp.where(ok, cnt, nge)

    if not two_bits:
        return lax.fori_loop(0, 31, lambda i, c: place(jnp.left_shift(jnp.int32(1), 30 - i), c), (thr, nge))

    def place_two(i, carry):
        thr, nge = carry
        hi = jnp.left_shift(jnp.int32(1), 30 - 2 * i)
        lo = jnp.left_shift(jnp.int32(1), 29 - 2 * i)
        n_hi, n_lo, n_both = (count_ge(_key_to_float(thr | bits)) for bits in (hi, lo, hi | lo))
        ok_hi = n_hi >= topk
        thr1, nge1 = jnp.where(ok_hi, thr | hi, thr), jnp.where(ok_hi, n_hi, nge)
        n2 = jnp.where(ok_hi, n_both, n_lo)
        ok2 = n2 >= topk
        return jnp.where(ok2, thr1 | lo, thr1), jnp.where(ok2, n2, nge1)

    return place(jnp.int32(1), lax.fori_loop(0, 15, place_two, (thr, nge)))


def _dsa_prompt_kernel(ki_ref, qiT_ref, wT_ref, k_ref, qT_ref, vT_ref, gb_ref, o_ref,
                       score_scr, bias_scr, lg_scr, oT_scr, *, topk):
    i = pl.program_id(1)
    ntiles = (i * Q_BLOCK + Q_BLOCK + K_TILE - 1) // K_TILE
    qpos = i * Q_BLOCK + lax.broadcasted_iota(I32, (K_TILE, Q_BLOCK), 1)
    krow = lax.broadcasted_iota(I32, (K_TILE, Q_BLOCK), 0)

    qi = qiT_ref[0]
    wT = wT_ref[...]

    def score_body(j, carry):
        ks = pl.multiple_of(j * K_TILE, K_TILE)
        x = _dot(ki_ref[pl.ds(ks, K_TILE), :], qi)
        sc = jnp.zeros((K_TILE, Q_BLOCK), F32)
        for h in range(IDX_HEADS):
            sc = sc + jnp.maximum(x[:, h * Q_BLOCK:(h + 1) * Q_BLOCK], 0.0) * wT[h:h + 1]
        score_scr[pl.ds(ks, K_TILE), :] = jnp.where(krow + ks <= qpos, sc, -jnp.inf)
        return carry

    lax.fori_loop(0, ntiles, score_body, 0)

    natt = (i * Q_BLOCK + Q_BLOCK + ATT_TILE - 1) // ATT_TILE

    @pl.when(natt * (ATT_TILE // K_TILE) > ntiles)
    def _():
        score_scr[pl.ds(pl.multiple_of(ntiles * K_TILE, K_TILE), K_TILE), :] = jnp.full((K_TILE, Q_BLOCK), -jnp.inf, F32)

    count = functools.partial(_count_tiles, score_scr, natt)
    arow = lax.broadcasted_iota(I32, (ATT_TILE, Q_BLOCK), 0)
    thr, nge = _kth_largest(lambda x: count(lambda tile, ks: tile >= x), topk, (1, Q_BLOCK))
    live = thr > KEY_NEG_INF
    thr = _key_to_float(jnp.maximum(thr, KEY_NEG_INF + 1))
    no_cut = jnp.full((1, Q_BLOCK), natt * ATT_TILE, I32)
    has_ties = jnp.max(jnp.where(live & (nge > topk), 1, 0)) > 0

    def tie_cut():
        need = topk - count(lambda tile, ks: tile > thr)
        nbits = max(1, int(math.ceil(math.log2(score_scr.shape[0] + 1))))

        def body(b, pos):
            cand = pos | jnp.left_shift(jnp.int32(1), nbits - 1 - b)
            below = count(lambda tile, ks: (tile == thr) & (arow + ks < cand))
            return jnp.where(below <= need, cand, pos)

        return lax.fori_loop(0, nbits, body, jnp.zeros((1, Q_BLOCK), I32))

    cut = lax.cond(has_ties, tie_cut, lambda: no_cut)

    def bias_body(j, carry):
        ks = pl.multiple_of(j * K_TILE, K_TILE)
        tile = score_scr[pl.ds(ks, K_TILE), :]
        sel = (tile > thr) | ((tile == thr) & (krow + ks < cut))
        bias_scr[pl.ds(ks, K_TILE), :] = jnp.where(sel, 0.0, -jnp.inf)
        return carry

    lax.fori_loop(0, natt * (ATT_TILE // K_TILE), bias_body, 0)

    npair = B_HEADS // 2
    z = jnp.zeros((B_DH, Q_BLOCK), BF16)
    rhs = []
    for p in range(npair):
        qp = qT_ref[p * LANES:(p + 1) * LANES, :]
        rhs.append(jnp.concatenate([jnp.concatenate([qp[:B_DH], z], axis=0),
                                    jnp.concatenate([z, qp[B_DH:]], axis=0)], axis=1))
    fold = lambda a: a.reshape(ATT_TILE // 8, 8, a.shape[1])

    def logit_body(j, mx):
        ks = pl.multiple_of(j * ATT_TILE, ATT_TILE)
        bias = bias_scr[pl.ds(ks, ATT_TILE), :]
        bias2 = jnp.concatenate([bias, bias], axis=1)
        lgs = [_dot(k_ref[pl.ds(ks, ATT_TILE), p * LANES:(p + 1) * LANES], rhs[p]) + bias2 for p in range(npair)]
        lg_scr[pl.ds(ks, ATT_TILE), :] = jnp.concatenate(lgs, axis=1)
        return tuple(jnp.maximum(m, jnp.max(fold(lg), axis=0)) for m, lg in zip(mx, lgs))

    mx = lax.fori_loop(0, natt, logit_body,
                       tuple(jnp.full((8, 2 * Q_BLOCK), -jnp.inf, F32) for _ in range(npair)))
    mx = jnp.concatenate([jnp.max(m, axis=0, keepdims=True) for m in mx], axis=1)
    oT_scr[...] = jnp.zeros_like(oT_scr)

    def value_body(j, den):
        ks = pl.multiple_of(j * ATT_TILE, ATT_TILE)
        new_den, new_acc = [], []
        for h in range(B_HEADS):
            cols = slice(h * Q_BLOCK, (h + 1) * Q_BLOCK)
            rows = slice(h * B_DH, (h + 1) * B_DH)
            e = jnp.exp(lg_scr[pl.ds(ks, ATT_TILE), cols] - mx[:, cols])
            new_den.append(den[h] + jnp.sum(fold(e), axis=0))
            new_acc.append(oT_scr[rows, :] + _dot(vT_ref[rows, pl.ds(ks, ATT_TILE)], e.astype(BF16)))
        for h in range(B_HEADS):
            oT_scr[h * B_DH:(h + 1) * B_DH, :] = new_acc[h]
        return tuple(new_den)

    den = lax.fori_loop(0, natt, value_body, tuple(jnp.zeros((8, Q_BLOCK), F32) for _ in range(B_HEADS)))
    for h in range(B_HEADS):
        rows = slice(h * B_DH, (h + 1) * B_DH)
        oT_scr[rows, :] = oT_scr[rows, :] / jnp.sum(den[h], axis=0, keepdims=True)

    o_ref[...] = (oT_scr[...].T * gb_ref[...].astype(F32)).astype(BF16)


def _dsa_prompt(kibf, qiT2, wT, kbf, qT, vT, gbs, batch, seq):
    n = batch * seq
    nq = seq // Q_BLOCK
    topk = min(TOPK_MAX, seq // 4)
    assert seq % ATT_TILE == 0 and topk <= K_TILE
    return pl.pallas_call(
        functools.partial(_dsa_prompt_kernel, topk=topk), grid=(batch, nq),
        in_specs=[
            pl.BlockSpec((seq, IDX_DIM), lambda b, i: (b, 0)),
            pl.BlockSpec((1, IDX_DIM, IDX_HEADS * Q_BLOCK), lambda b, i: (b * nq + i, 0, 0)),
            pl.BlockSpec((IDX_HEADS, Q_BLOCK), lambda b, i: (0, b * nq + i)),
            pl.BlockSpec((seq, B_WIDTH), lambda b, i: (b, 0)),
            pl.BlockSpec((B_WIDTH, Q_BLOCK), lambda b, i: (0, b * nq + i)),
            pl.BlockSpec((B_WIDTH, seq), lambda b, i: (0, b)),
            pl.BlockSpec((Q_BLOCK, B_WIDTH), lambda b, i: (b * nq + i, 0)),
        ],
        out_specs=pl.BlockSpec((Q_BLOCK, B_WIDTH), lambda b, i: (b * nq + i, 0)),
        out_shape=jax.ShapeDtypeStruct((n, B_WIDTH), BF16),
        scratch_shapes=[
            pltpu.VMEM((seq, Q_BLOCK), F32),
            pltpu.VMEM((seq, Q_BLOCK), F32),
            pltpu.VMEM((seq, B_HEADS * Q_BLOCK), F32),
            pltpu.VMEM((B_WIDTH, Q_BLOCK), F32),
        ],
        compiler_params=pltpu.CompilerParams(dimension_semantics=("parallel", "arbitrary"),
                                             vmem_limit_bytes=VMEM_LIMIT),
        name="dsa_prompt",
    )(kibf, qiT2, wT, kbf, qT, vT, gbs)


def _dsa_sample_kernel(pt_ref, qi_ref, wrep_ref, qbd_ref, kin_ref, kn_ref, vn_ref, gb_ref,
                       cki_hbm, ck_hbm, cv_hbm, o_ref, ki_buf, k_buf, v_buf, score_scr, sem_i, sem_k, sem_v,
                       *, t, npages, topk, depth, group):
    b = pl.program_id(0)
    nb = pl.num_programs(0)
    page = ki_buf.shape[3]
    total = (npages + 1) * page
    lane = lax.broadcasted_iota(I32, (t, page), 1)
    qrow = lax.broadcasted_iota(I32, (t, page), 0)

    def ki_copy(seq, p):
        half = seq % 2
        return pltpu.make_async_copy(cki_hbm.at[pt_ref[seq, p], 0], ki_buf.at[half, p], sem_i.at[half * npages + p])

    def k_copy(seq, p):
        return pltpu.make_async_copy(ck_hbm.at[pt_ref[seq, p], 0], k_buf.at[p % depth], sem_k.at[p % depth])

    def v_copy(seq, p):
        return pltpu.make_async_copy(cv_hbm.at[pt_ref[seq, p], 0], v_buf.at[p % depth], sem_v.at[p % depth])

    def start_ki(seq):
        def body(p, carry):
            ki_copy(seq, p).start()
            return carry
        lax.fori_loop(0, npages, body, 0)

    @pl.when(b == 0)
    def _():
        start_ki(b)
        for s in range(depth):
            k_copy(b, s).start()
            v_copy(b, s).start()

    @pl.when(b + 1 < nb)
    def _():
        start_ki(b + 1)

    def scores(kidx_t):
        x = _dot(qi_ref[0], kidx_t)
        xw = jnp.maximum(x, 0.0) * wrep_ref[0]
        sc = xw[0:t]
        for h in range(1, IDX_HEADS):
            sc = sc + xw[h * t:(h + 1) * t]
        return sc

    def score_pages(i, carry):
        for g in range(group):
            ki_copy(b, i * group + g).wait()
        sc = [scores(ki_buf[b % 2, i * group + g].astype(BF16)) for g in range(group)]
        score_scr[:, pl.ds(pl.multiple_of(i * group * page, group * page), group * page)] = jnp.concatenate(sc, axis=1)
        return carry

    lax.fori_loop(0, npages // group, score_pages, 0)

    score_scr[:, npages * page:] = jnp.where(lane <= qrow, scores(kin_ref[0]), -jnp.inf)

    def count(pred):
        hit = jnp.where(pred(score_scr[...]), 1, 0)
        part = hit[:, 0:page]
        for c in range(1, npages + 1):
            part = part + hit[:, c * page:(c + 1) * page]
        return jnp.sum(part, axis=1, keepdims=True)

    thr, nge = _kth_largest(lambda x: count(lambda s: s >= x), topk, (t, 1), two_bits=True)
    live = thr > KEY_NEG_INF
    thr = _key_to_float(jnp.maximum(thr, KEY_NEG_INF + 1))
    has_ties = jnp.max(jnp.where(live & (nge > topk), 1, 0)) > 0
    pos_all = lax.broadcasted_iota(I32, (t, total), 1)

    def tie_cut():
        need = topk - count(lambda s: s > thr)
        nbits = max(1, int(math.ceil(math.log2(total + 1))))

        def pos_step(j, pos):
            cand = pos | jnp.left_shift(jnp.int32(1), nbits - 1 - j)
            below = count(lambda s: (s == thr) & (pos_all < cand))
            return jnp.where(below <= need, cand, pos)

        return lax.fori_loop(0, nbits, pos_step, jnp.zeros((t, 1), I32))

    cut = lax.cond(has_ties, tie_cut, lambda: jnp.full((t, 1), total, I32))

    qbd = qbd_ref[0]

    def attend(kv_pages, carry):
        m_old, l_old, acc = carry
        lgs = []
        for k_t, _, ks in kv_pages:
            tile = score_scr[:, pl.ds(ks, page)]
            sel = (tile > thr) | ((tile == thr) & (lane + ks < cut))
            sel = jnp.concatenate([sel.astype(I32)] * B_HEADS, axis=0) > 0
            lgs.append(jnp.where(sel, _dot(qbd, k_t), -jnp.inf))
        lg = jnp.concatenate(lgs, axis=1)
        m_new = jnp.maximum(m_old, jnp.max(lg, axis=1, keepdims=True))
        m_safe = jnp.where(m_new == -jnp.inf, 0.0, m_new)
        alpha = jnp.exp(m_old - m_safe)
        e = jnp.exp(lg - m_safe)
        l_new = alpha * l_old + jnp.sum(e, axis=1, keepdims=True)
        acc = alpha * acc
        for g, (_, v_t, _) in enumerate(kv_pages):
            acc = acc + _dot_nt(e[:, g * page:(g + 1) * page].astype(BF16), v_t)
        return m_new, l_new, acc

    def attend_pages(i, carry):
        pages = []
        for g in range(group):
            p = i * group + g
            slot = p % depth
            k_copy(b, p).wait()
            v_copy(b, p).wait()
            pages.append((k_buf[slot].astype(BF16), v_buf[slot].astype(BF16), pl.multiple_of(p * page, page)))
        carry = attend(pages, carry)
        for g in range(group):
            p = i * group + g

            @pl.when(p + depth < npages)
            def _():
                k_copy(b, p + depth).start()
                v_copy(b, p + depth).start()

            @pl.when((p + depth >= npages) & (b + 1 < nb))
            def _():
                k_copy(b + 1, p + depth - npages).start()
                v_copy(b + 1, p + depth - npages).start()

        return carry

    nrow = B_HEADS * t
    carry = (jnp.full((nrow, 1), -jnp.inf, F32), jnp.zeros((nrow, 1), F32), jnp.zeros((nrow, B_WIDTH), F32))
    carry = lax.fori_loop(0, npages // group, attend_pages, carry)
    _, den, acc = attend([(kn_ref[0], vn_ref[0], npages * page)], carry)
    o = acc / den
    col = lax.broadcasted_iota(I32, (t, B_WIDTH), 1)
    out = jnp.zeros((t, B_WIDTH), F32)
    for h in range(B_HEADS):
        out = jnp.where((col >= h * B_DH) & (col < (h + 1) * B_DH), o[h * t:(h + 1) * t], out)
    o_ref[0] = (out * gb_ref[0].astype(F32)).astype(BF16)


def _dsa_sample(page_table, qi_rows, wrep, qbd, kin_t, kn_t, vn_t, gbs, cache_kidx_t, cache_k_t, cache_v_t, t):
    bd, npages = page_table.shape
    page = cache_kidx_t.shape[3]
    total = npages * page + t
    topk = min(TOPK_MAX, total // 4)
    group = max(g for g in range(1, SAMPLE_PAGE_GROUP + 1) if npages % g == 0)
    depth = max(d for d in range(group, min(SAMPLE_DMA_DEPTH, npages) + 1, group) if npages % d == 0)
    per_b = lambda b, pt: (b, 0, 0)
    hbm = pl.BlockSpec(memory_space=pl.ANY)
    grid_spec = pltpu.PrefetchScalarGridSpec(
        num_scalar_prefetch=1, grid=(bd,),
        in_specs=[
            pl.BlockSpec((1, IDX_HEADS * t, IDX_DIM), per_b),
            pl.BlockSpec((1, IDX_HEADS * t, page), per_b),
            pl.BlockSpec((1, B_HEADS * t, B_WIDTH), per_b),
            pl.BlockSpec((1, IDX_DIM, page), per_b),
            pl.BlockSpec((1, B_WIDTH, page), per_b),
            pl.BlockSpec((1, B_WIDTH, page), per_b),
            pl.BlockSpec((1, t, B_WIDTH), per_b),
            hbm, hbm, hbm,
        ],
        out_specs=pl.BlockSpec((1, t, B_WIDTH), per_b),
        scratch_shapes=[
            pltpu.VMEM((2, npages, IDX_DIM, page), F32),
            pltpu.VMEM((depth, B_WIDTH, page), F32),
            pltpu.VMEM((depth, B_WIDTH, page), F32),
            pltpu.VMEM((t, (npages + 1) * page), F32),
            pltpu.SemaphoreType.DMA((2 * npages,)),
            pltpu.SemaphoreType.DMA((depth,)),
            pltpu.SemaphoreType.DMA((depth,)),
        ],
    )
    return pl.pallas_call(
        functools.partial(_dsa_sample_kernel, t=t, npages=npages, topk=topk, depth=depth, group=group),
        grid_spec=grid_spec,
        out_shape=jax.ShapeDtypeStruct((bd, t, B_WIDTH), BF16),
        compiler_params=pltpu.CompilerParams(dimension_semantics=("arbitrary",), vmem_limit_bytes=VMEM_LIMIT),
        name="dsa_sample",
    )(page_table, qi_rows, wrep, qbd, kin_t, kn_t, vn_t, gbs, cache_kidx_t, cache_k_t, cache_v_t)


def _merge_kernel(x_ref, ma_ref, mb_ref, w_ref, g_ref, b_ref, y_ref, *, alpha):
    mix = jnp.concatenate([ma_ref[h] for h in range(A_HEADS)] + [mb_ref[...]], axis=1)
    y = alpha * x_ref[...] + _dot(mix, w_ref[...])
    mu = jnp.mean(y, axis=-1, keepdims=True)
    d = y - mu
    var = jnp.mean(d * d, axis=-1, keepdims=True)
    y_ref[...] = d * lax.rsqrt(var + LN_EPS) * g_ref[...] + b_ref[...]


def _merge(x, mix_a, mix_b, w_out, ln_g, ln_b, alpha, tm):
    n, dm = x.shape
    return pl.pallas_call(
        functools.partial(_merge_kernel, alpha=alpha), grid=(n // tm,),
        in_specs=[
            pl.BlockSpec((tm, dm), lambda i: (i, 0)),
            pl.BlockSpec((A_HEADS, tm, A_DV), lambda i: (0, i, 0)),
            pl.BlockSpec((tm, B_WIDTH), lambda i: (i, 0)),
            pl.BlockSpec((A_WIDTH + B_WIDTH, dm), lambda i: (0, 0)),
            pl.BlockSpec((1, dm), lambda i: (0, 0)),
            pl.BlockSpec((1, dm), lambda i: (0, 0)),
        ],
        out_specs=pl.BlockSpec((tm, dm), lambda i: (i, 0)),
        out_shape=jax.ShapeDtypeStruct((n, dm), F32),
        compiler_params=pltpu.CompilerParams(dimension_semantics=("parallel",), vmem_limit_bytes=VMEM_LIMIT),
        name="merge",
    )(x, mix_a, mix_b, w_out, ln_g, ln_b)


def _split_weights(w_in_l):
    offs = np.cumsum([0, 512, 512, 512, 512, 512, 512, 512, 512, IDX_HEADS * IDX_DIM, IDX_DIM, IDX_HEADS])
    col = lambda i: w_in_l[:, offs[i]:offs[i + 1]]
    qa, fa, ia, ga, qb, kb, vb, gb, qi, ki, wi = (col(i) for i in range(11))
    pad = jnp.zeros((w_in_l.shape[0], LANES - IDX_DIM), w_in_l.dtype)
    wn = jnp.concatenate([qa, fa, ia, ga, kb, gb, ki, pad], axis=1).astype(BF16)
    wt = jnp.concatenate([qb, kb, vb, qi, wi, ki], axis=1).T.astype(BF16)
    return wn, wt


def _layer(xp, xs, cache_k, cache_v, cache_kidx, s0_sample, page_table, w_in_l, lb_l, norm_g_l, kn_g_l, kn_b_l,
           w_out_l, ln_g_l, ln_b_l, alpha):
    b, l, dm = xp.shape
    bd, t, _ = xs.shape
    npages, page = page_table.shape[1], cache_k.shape[2]
    assert page == Q_BLOCK and l % page == 0
    past = npages * page
    wn, wt = _split_weights(w_in_l)
    lb = lb_l.reshape(1, A_WIDTH)
    ng = norm_g_l.reshape(1, A_DV)
    w_out_b = w_out_l.astype(BF16)
    lng, lnb = ln_g_l.reshape(1, dm), ln_b_l.reshape(1, dm)

    tm = 256
    xp2 = xp.reshape(b * l, dm)
    (hq, hk, hg, hv, hgate, kbf, gbs, kibf, qT, vT, qiT2, wT, k_p, v_p, ki_p) = _project(
        xp2, jnp.arange(l, dtype=I32), wn, wt, lb, kn_g_l, kn_b_l, tm)
    mix_a, s_p = _hgrn_prompt(hq, hk, hg, hv, hgate, ng, b, l, min(l, 512))
    mix_b = _dsa_prompt(kibf, qiT2, wT, kbf, qT, vT, gbs, b, l)
    y_p = _merge(xp2, mix_a, mix_b, w_out_b, lng, lnb, alpha, 512).reshape(b, l, dm)

    ns = bd * t
    xs2 = xs.reshape(ns, dm)
    pos_s = past + (jnp.arange(ns, dtype=I32) % t)
    (hq, hk, hg, hv, hgate, kbf, gbs, kibf, qT, vT, qiT2, wT, k_s, v_s, ki_s) = _project(
        xs2, pos_s, wn, wt, lb, kn_g_l, kn_b_l, ns)
    mix_a, s_s = _hgrn_sample(hq, hk, hg, hv, hgate, ng, s0_sample, t)
    qi_nat = qiT2.reshape(ns // Q_BLOCK, IDX_DIM, IDX_HEADS, Q_BLOCK).transpose(0, 3, 2, 1)
    qi_rows = qi_nat.reshape(bd, t, IDX_HEADS, IDX_DIM).transpose(0, 2, 1, 3).reshape(bd, IDX_HEADS * t, IDX_DIM)
    w_rows = wT.T.reshape(bd, t, IDX_HEADS).transpose(0, 2, 1).reshape(bd, IDX_HEADS * t, 1)
    wrep = jnp.broadcast_to(w_rows, (bd, IDX_HEADS * t, page))
    q_nat = qT.T.reshape(bd, 1, t, B_HEADS, B_DH)
    eye = jnp.eye(B_HEADS, dtype=BF16).reshape(1, B_HEADS, 1, B_HEADS, 1)
    qbd = (q_nat * eye).reshape(bd, B_HEADS * t, B_WIDTH)

    def per_seq(pages):
        feat = pages.shape[1]
        a = pages.transpose(1, 0, 2).reshape(feat, bd, t).transpose(1, 0, 2).astype(BF16)
        return jnp.concatenate([a, jnp.zeros((bd, feat, page - t), BF16)], axis=2)

    token_minor = lambda c: jnp.moveaxis(c, 2, -1).reshape(c.shape[0], c.shape[1], -1, page)
    mix_b = _dsa_sample(page_table, qi_rows, wrep, qbd, per_seq(ki_s), per_seq(k_s), per_seq(v_s),
                        gbs.reshape(bd, t, B_WIDTH), token_minor(cache_kidx), token_minor(cache_k),
                        token_minor(cache_v), t).reshape(ns, B_WIDTH)
    y_s = _merge(xs2, mix_a, mix_b, w_out_b, lng, lnb, alpha, ns).reshape(bd, t, dm)

    return (y_p, y_s, k_p, v_p, ki_p, s_p, k_s, v_s, ki_s, s_s)


def kernel(x_prompt, x_sample, cache_k, cache_v, cache_kidx, state_hgrn, page_table, w_in, hgrn_lb_logits,
           hgrn_norm_g, idx_norm_g, idx_norm_b, w_out, ln_g, ln_b):
    depth = w_in.shape[0]
    assert depth == 1, "one layer per step"
    b, l, _ = x_prompt.shape
    bd, t, _ = x_sample.shape
    page = cache_k.shape[2]
    alpha = (2.0 * depth) ** 0.25
    lbs = jnp.cumsum(jax.nn.softmax(hgrn_lb_logits.astype(F32), axis=0), axis=0)[:depth]
    (y_p, y_s, k_p, v_p, ki_p, s_p, k_s, v_s, ki_s, s_s) = _layer(
        x_prompt, x_sample, cache_k, cache_v, cache_kidx, state_hgrn[0], page_table, w_in[0], lbs[0],
        hgrn_norm_g[0], idx_norm_g[0], idx_norm_b[0], w_out[0], ln_g[0], ln_b[0], alpha)
    nat_p = lambda pg, *f: jnp.moveaxis(pg.reshape(b, l // page, 1, *f, page), -1, 3)
    nat_s = lambda pg, *f: pg.transpose(0, 2, 1).reshape(bd, 1, t, *f)
    return (
        y_p, y_s,
        nat_p(k_p, B_HEADS, B_DH), nat_p(v_p, B_HEADS, B_DH), nat_p(ki_p, IDX_DIM),
        s_p[None],
        nat_s(k_s, B_HEADS, B_DH), nat_s(v_s, B_HEADS, B_DH), nat_s(ki_s, IDX_DIM),
        s_s[None],
    )
```

```python
import functools
import math

import jax
import jax.numpy as jnp
import numpy as np
from jax import lax
from jax.experimental import pallas as pl
from jax.experimental.pallas import tpu as pltpu

F32 = jnp.float32
BF16 = jnp.bfloat16
I32 = jnp.int32

A_HEADS = 4
A_DK = 128
A_DV = 128
B_HEADS = 8
B_DH = 64
IDX_HEADS = 16
IDX_DIM = 64
TOPK_MAX = 256
ROPE_THETA = 500000.0
ROT = 16
ROT_HALF = ROT // 2
LN_EPS = 1e-5
Q_BLOCK = 128
K_TILE = 256
ATT_TILE = 512
HGRN_CHUNK = 128
HGRN_SUB = 16
LANES = 128
VMEM_LIMIT = 56 * 1024 * 1024
SAMPLE_DMA_DEPTH = 32
SAMPLE_PAGE_GROUP = 8

A_WIDTH = A_HEADS * A_DV
B_WIDTH = B_HEADS * B_DH
NAT_WIDTH = 6 * 512 + LANES
TR_ROWS = 3 * 512 + IDX_HEADS * IDX_DIM + IDX_HEADS + IDX_DIM
INT_MIN = -(2 ** 31)
KEY_NEG_INF = -0x7F800000


def _dot(a, b):
    return jnp.dot(a, b, preferred_element_type=F32)


def _dot_nt(a, b):
    return lax.dot_general(a, b, (((1,), (1,)), ((), ())), preferred_element_type=F32)


def _dot_tn(a, b):
    return lax.dot_general(a, b, (((0,), (0,)), ((), ())), preferred_element_type=F32)


def _silu(x):
    return x * jax.nn.sigmoid(x)


def _key_to_float(key):
    m = key >> 31
    mag = (key ^ m) - m
    return pltpu.bitcast(mag | (m & INT_MIN), F32)


def _proj_kernel(x_ref, wn_ref, wt_ref, lb_ref, kng_ref, knb_ref, kngc_ref, knbc_ref, cn_ref, sa_ref, sb_ref,
                 ct_ref, st_ref,
                 hq_ref, hk_ref, hg_ref, hv_ref, hgate_ref, kbf_ref, gb_ref, kibf_ref,
                 qT_ref, vT_ref, qiT_ref, wT_ref, kTp_ref, vTp_ref, kiTp_ref):
    tm = x_ref.shape[0]
    xb = x_ref[...].astype(BF16)

    def nat(col, width=512):
        return _dot(xb, wn_ref[:, col:col + width])

    lb = lb_ref[...]
    qa = nat(0)
    hq = _silu(qa)
    fa = nat(512)
    hg = jnp.log2(lb + (1.0 - lb) * jax.nn.sigmoid(fa))
    hk = (1.0 - lb) * jax.nn.sigmoid(-fa)
    ia = nat(1024)
    hgate = _silu(nat(1536))
    for h in range(A_HEADS):
        sl = slice(h * A_DK, (h + 1) * A_DK)
        hq_ref[h] = hq[:, sl].astype(BF16)
        hk_ref[h] = hk[:, sl]
        hg_ref[h] = hg[:, sl]
        hv_ref[h] = ia[:, sl].astype(BF16)
        hgate_ref[h] = hgate[:, sl].astype(BF16)

    cn, sa, sb = cn_ref[...], sa_ref[...], sb_ref[...]

    def rope_nat(xc):
        return xc * cn + pltpu.roll(xc, LANES - ROT_HALF, 1) * sa + pltpu.roll(xc, ROT_HALF, 1) * sb

    kb = nat(2048)
    kbf_ref[...] = jnp.concatenate(
        [rope_nat(kb[:, c * LANES:(c + 1) * LANES]) for c in range(B_WIDTH // LANES)], axis=1).astype(BF16)
    gb_ref[...] = _silu(nat(2560)).astype(BF16)

    kic = nat(3072, LANES)
    lane = lax.broadcasted_iota(I32, (tm, LANES), 1)
    inb = lane < IDX_DIM
    mu = jnp.sum(jnp.where(inb, kic, 0.0), axis=-1, keepdims=True) * (1.0 / IDX_DIM)
    d = jnp.where(inb, kic - mu, 0.0)
    var = jnp.sum(d * d, axis=-1, keepdims=True) * (1.0 / IDX_DIM)
    kin = d * lax.rsqrt(var + LN_EPS) * kng_ref[...] + knb_ref[...]
    kibf_ref[...] = rope_nat(kin)[:, :IDX_DIM].astype(BF16)

    ct, st = ct_ref[...], st_ref[...]
    npage = tm // Q_BLOCK

    def tr(row, height):
        return _dot_nt(wt_ref[row:row + height, :], xb)

    def rope_tr(blk):
        x1, x2 = blk[0:ROT_HALF], blk[ROT_HALF:ROT]
        return jnp.concatenate([x1 * ct - x2 * st, x1 * st + x2 * ct, blk[ROT:]], axis=0)

    qbT = tr(0, B_WIDTH)
    for h in range(B_HEADS):
        blk = rope_tr(qbT[h * B_DH:(h + 1) * B_DH])
        qT_ref[h * B_DH:(h + 1) * B_DH, :] = (blk * (B_DH ** -0.5)).astype(BF16)
    kbT = tr(B_WIDTH, B_WIDTH)
    for h in range(B_HEADS):
        blk = rope_tr(kbT[h * B_DH:(h + 1) * B_DH])
        for pg in range(npage):
            kTp_ref[pg, h * B_DH:(h + 1) * B_DH, :] = blk[:, pg * Q_BLOCK:(pg + 1) * Q_BLOCK]
    vbT = tr(2 * B_WIDTH, B_WIDTH)
    vT_ref[...] = vbT.astype(BF16)
    for pg in range(npage):
        vTp_ref[pg] = vbT[:, pg * Q_BLOCK:(pg + 1) * Q_BLOCK]
    row = 3 * B_WIDTH
    qiT = tr(row, IDX_HEADS * IDX_DIM)
    for h in range(IDX_HEADS):
        blk = rope_tr(qiT[h * IDX_DIM:(h + 1) * IDX_DIM]).astype(BF16)
        for pg in range(npage):
            qiT_ref[pg, :, h * Q_BLOCK:(h + 1) * Q_BLOCK] = blk[:, pg * Q_BLOCK:(pg + 1) * Q_BLOCK]
    row += IDX_HEADS * IDX_DIM
    wT_ref[...] = tr(row, IDX_HEADS) * (IDX_HEADS ** -0.5 * IDX_DIM ** -0.5)
    row += IDX_HEADS
    kiT = tr(row, IDX_DIM)
    muT = jnp.mean(kiT, axis=0, keepdims=True)
    dT = kiT - muT
    varT = jnp.mean(dT * dT, axis=0, keepdims=True)
    kiT = rope_tr(dT * lax.rsqrt(varT + LN_EPS) * kngc_ref[...] + knbc_ref[...])
    for pg in range(npage):
        kiTp_ref[pg] = kiT[:, pg * Q_BLOCK:(pg + 1) * Q_BLOCK]


def _rope_tables(pos):
    inv = ROPE_THETA ** (-jnp.arange(ROT_HALF, dtype=F32) / ROT_HALF)
    ang = pos.astype(F32)[:, None] * inv[None, :]
    cos, sin = jnp.cos(ang), jnp.sin(ang)
    p = pos.shape[0]
    one = jnp.ones((p, B_DH - ROT), F32)
    zero8 = jnp.zeros((p, ROT_HALF), F32)
    zero = jnp.zeros((p, B_DH - ROT), F32)
    cn = jnp.concatenate([cos, cos, one], axis=1)
    sa = jnp.concatenate([-sin, zero8, zero], axis=1)
    sb = jnp.concatenate([zero8, sin, zero], axis=1)
    tile2 = lambda a: jnp.concatenate([a, a], axis=1)
    return tile2(cn), tile2(sa), tile2(sb), cos.T, sin.T


def _project(x, pos, wn, wt, lb, kn_g, kn_b, tm):
    n, dm = x.shape
    p = pos.shape[0]
    nper = p // tm
    cn, sa, sb, ct, st = _rope_tables(pos)
    padl = lambda a: jnp.concatenate([a, jnp.zeros((LANES - IDX_DIM,), a.dtype)]).reshape(1, LANES)
    grid = (n // tm,)
    row = lambda i: (i, 0)
    full = lambda i: (0, 0)
    head = lambda i: (0, i, 0)
    page = lambda i: (i, 0, 0)
    per = lambda i: (i % nper, 0)
    perT = lambda i: (0, i % nper)
    colT = lambda i: (0, i)
    in_specs = [
        pl.BlockSpec((tm, dm), row),
        pl.BlockSpec((dm, NAT_WIDTH), full, pipeline_mode=pl.Buffered(1)),
        pl.BlockSpec((TR_ROWS, dm), full, pipeline_mode=pl.Buffered(1)),
        pl.BlockSpec((1, A_WIDTH), full),
        pl.BlockSpec((1, LANES), full),
        pl.BlockSpec((1, LANES), full),
        pl.BlockSpec((IDX_DIM, 1), full),
        pl.BlockSpec((IDX_DIM, 1), full),
        pl.BlockSpec((tm, LANES), per),
        pl.BlockSpec((tm, LANES), per),
        pl.BlockSpec((tm, LANES), per),
        pl.BlockSpec((ROT_HALF, tm), perT),
        pl.BlockSpec((ROT_HALF, tm), perT),
    ]
    npage = tm // Q_BLOCK
    hshape = lambda dt: jax.ShapeDtypeStruct((A_HEADS, n, A_DK), dt)
    out_shape = [
        hshape(BF16), hshape(F32), hshape(F32), hshape(BF16), hshape(BF16),
        jax.ShapeDtypeStruct((n, B_WIDTH), BF16), jax.ShapeDtypeStruct((n, B_WIDTH), BF16),
        jax.ShapeDtypeStruct((n, IDX_DIM), BF16),
        jax.ShapeDtypeStruct((B_WIDTH, n), BF16), jax.ShapeDtypeStruct((B_WIDTH, n), BF16),
        jax.ShapeDtypeStruct((n // Q_BLOCK, IDX_DIM, IDX_HEADS * Q_BLOCK), BF16),
        jax.ShapeDtypeStruct((IDX_HEADS, n), F32),
        jax.ShapeDtypeStruct((n // Q_BLOCK, B_WIDTH, Q_BLOCK), F32),
        jax.ShapeDtypeStruct((n // Q_BLOCK, B_WIDTH, Q_BLOCK), F32),
        jax.ShapeDtypeStruct((n // Q_BLOCK, IDX_DIM, Q_BLOCK), F32),
    ]
    hspec = pl.BlockSpec((A_HEADS, tm, A_DK), head)
    out_specs = [
        hspec, hspec, hspec, hspec, hspec,
        pl.BlockSpec((tm, B_WIDTH), row), pl.BlockSpec((tm, B_WIDTH), row),
        pl.BlockSpec((tm, IDX_DIM), row),
        pl.BlockSpec((B_WIDTH, tm), colT), pl.BlockSpec((B_WIDTH, tm), colT),
        pl.BlockSpec((npage, IDX_DIM, IDX_HEADS * Q_BLOCK), page),
        pl.BlockSpec((IDX_HEADS, tm), colT),
        pl.BlockSpec((npage, B_WIDTH, Q_BLOCK), page),
        pl.BlockSpec((npage, B_WIDTH, Q_BLOCK), page),
        pl.BlockSpec((npage, IDX_DIM, Q_BLOCK), page),
    ]
    return pl.pallas_call(
        _proj_kernel, grid=grid, in_specs=in_specs, out_specs=out_specs, out_shape=out_shape,
        compiler_params=pltpu.CompilerParams(dimension_semantics=("parallel",), vmem_limit_bytes=VMEM_LIMIT),
        name="proj",
    )(x, wn, wt, lb, padl(kn_g), padl(kn_b), kn_g.reshape(IDX_DIM, 1), kn_b.reshape(IDX_DIM, 1),
      cn, sa, sb, ct, st)


def _hgrn_cumsum(g, c):
    if c == LANES:
        ri = lax.broadcasted_iota(I32, (c, c), 0)
        ci = lax.broadcasted_iota(I32, (c, c), 1)
        tri = jnp.where(ci <= ri, 1.0, 0.0).astype(BF16)
        g1 = g.astype(BF16)
        e1 = g - g1.astype(F32)
        g2 = e1.astype(BF16)
        g3 = (e1 - g2.astype(F32)).astype(BF16)
        return _dot(tri, g1) + _dot(tri, g2) + _dot(tri, g3)
    ri = lax.broadcasted_iota(I32, (c, A_DK), 0)
    cum = jnp.zeros((c, A_DK), F32)
    for s in range(c):
        cum = cum + jnp.where(ri >= s, g[s:s + 1], 0.0)
    return cum


def _hgrn_chunks(chains, ng, c):
    cums = [_hgrn_cumsum(g, c) for _, _, g, _, _, _ in chains]
    return [_hgrn_chunk(q, k, cum, v, gate, ng, st, c) for (q, k, _, v, gate, st), cum in zip(chains, cums)]


def _hgrn_chunk(q, k, cum, v, gate, ng, st, c):
    r = HGRN_SUB
    nsub = c // r
    stb = st.astype(BF16)
    o = _dot_nt((q * jnp.exp2(cum)).astype(BF16), stb)

    rowi = lax.broadcasted_iota(I32, (r, A_DK), 0)
    lane = lax.broadcasted_iota(I32, (r, LANES), 1)
    prods = []
    for i in range(nsub):
        qs, cs = q[i * r:(i + 1) * r], cum[i * r:(i + 1) * r]
        for s in range(r):
            row = i * r + s
            p = (qs * k[row:row + 1]) * jnp.exp2(cs - cum[row:row + 1])
            prods.append(jnp.where(rowi >= s, p, 0.0).astype(BF16))
    if nsub % 2 == 0:
        half = len(prods) // 2
        both = jnp.concatenate([jnp.concatenate(prods[:half], axis=0), jnp.concatenate(prods[half:], axis=0)], axis=1)
        wr = lax.broadcasted_iota(I32, (2 * A_DK, 2 * LANES), 0) < A_DK
        wc = lax.broadcasted_iota(I32, (2 * A_DK, 2 * LANES), 1) < LANES
        sums = _dot(both, jnp.where(wr == wc, 1.0, 0.0).astype(BF16))
        rsum = lambda row: (sums[row * r:(row + 1) * r, :LANES] if row < half
                            else sums[(row - half) * r:(row - half + 1) * r, LANES:])
    else:
        sums = _dot(jnp.concatenate(prods, axis=0), jnp.ones((A_DK, LANES), BF16))
        rsum = lambda row: sums[row * r:(row + 1) * r]
    sc_rows = []
    for i in range(nsub):
        sci = jnp.zeros((r, LANES), F32)
        for s in range(r):
            row = i * r + s
            sci = jnp.where(lane == row, rsum(row), sci)
        if i > 0:
            ref = cum[i * r - 1:i * r]
            qt = (q[i * r:(i + 1) * r] * jnp.exp2(cum[i * r:(i + 1) * r] - ref)).astype(BF16)
            kt = (k[:i * r] * jnp.exp2(ref - cum[:i * r])).astype(BF16)
            kt = jnp.concatenate([kt, jnp.zeros((LANES - i * r, A_DK), BF16)], axis=0)
            sci = sci + _dot_nt(qt, kt)
        sc_rows.append(sci)
    sc = jnp.concatenate(sc_rows, axis=0).astype(BF16)
    vb = v.astype(BF16)
    last = cum[c - 1:c]
    kh = (k * jnp.exp2(last - cum)).astype(BF16)
    if c < LANES:
        zpad = jnp.zeros((LANES - c, A_DK), BF16)
        vb = jnp.concatenate([vb, zpad], axis=0)
        kh = jnp.concatenate([kh, zpad], axis=0)
    o = o + _dot(sc, vb)
    st_new = st * jnp.exp2(last) + _dot_tn(vb, kh)
    ms = jnp.mean(o * o, axis=-1, keepdims=True)
    return o * lax.rsqrt(ms + LN_EPS) * ng * gate, st_new


def _hgrn_prompt_kernel(q_ref, k_ref, g_ref, v_ref, gate_ref, ng_ref, o_ref, sf_ref, st_scr):
    c = HGRN_CHUNK
    nchunk = q_ref.shape[1] // c
    tb = pl.program_id(1)

    @pl.when(tb == 0)
    def _():
        st_scr[...] = jnp.zeros_like(st_scr)

    ng = ng_ref[...]

    def body(j, carry):
        sl = pl.ds(pl.multiple_of(j * c, c), c)
        loaded = [(q_ref[h, sl, :].astype(F32), k_ref[h, sl, :], g_ref[h, sl, :], v_ref[h, sl, :].astype(F32),
                   gate_ref[h, sl, :].astype(F32), st_scr[h]) for h in range(A_HEADS)]
        for h, (o, st_new) in enumerate(_hgrn_chunks(loaded, ng, c)):
            st_scr[h] = st_new
            o_ref[h, sl, :] = o.astype(BF16)
        return carry

    lax.fori_loop(0, nchunk, body, 0)

    @pl.when(tb == pl.num_programs(1) - 1)
    def _():
        for h in range(A_HEADS):
            sf_ref[0, h] = st_scr[h].T


def _hgrn_prompt(hq, hk, hg, hv, hgate, ng, batch, seq, tb):
    n = batch * seq
    nt = seq // tb
    blk = pl.BlockSpec((A_HEADS, tb, A_DK), lambda b, t: (0, b * nt + t, 0))
    return pl.pallas_call(
        _hgrn_prompt_kernel, grid=(batch, nt),
        in_specs=[blk, blk, blk, blk, blk, pl.BlockSpec((1, A_DV), lambda b, t: (0, 0))],
        out_specs=[blk, pl.BlockSpec((1, A_HEADS, A_DK, A_DV), lambda b, t: (b, 0, 0, 0))],
        out_shape=[jax.ShapeDtypeStruct((A_HEADS, n, A_DV), BF16),
                   jax.ShapeDtypeStruct((batch, A_HEADS, A_DK, A_DV), F32)],
        scratch_shapes=[pltpu.VMEM((A_HEADS, A_DV, A_DK), F32)],
        compiler_params=pltpu.CompilerParams(dimension_semantics=("parallel", "arbitrary"),
                                             vmem_limit_bytes=VMEM_LIMIT),
        name="hgrn_prompt",
    )(hq, hk, hg, hv, hgate, ng)


def _hgrn_sample_kernel(q_ref, k_ref, g_ref, v_ref, gate_ref, ng_ref, s0_ref, o_ref, sf_ref, *, t, nb):
    c = HGRN_SUB
    ng = ng_ref[...]
    zpad = jnp.zeros((c - t, A_DK), F32)
    pad = lambda a: jnp.concatenate([a, zpad], axis=0)
    chains = []
    for h in range(A_HEADS):
        q, k, g = q_ref[h].astype(F32), k_ref[h], g_ref[h]
        v, gate = v_ref[h].astype(F32), gate_ref[h].astype(F32)
        for b in range(nb):
            sl = slice(b * t, (b + 1) * t)
            chains.append((pad(q[sl]), pad(k[sl]), pad(g[sl]), pad(v[sl]), pad(gate[sl]), s0_ref[b, h].T))
    results = _hgrn_chunks(chains, ng, c)
    for h in range(A_HEADS):
        for b in range(nb):
            sf_ref[b, h] = results[h * nb + b][1].T
        o_ref[h] = jnp.concatenate([results[h * nb + b][0][:t] for b in range(nb)], axis=0).astype(BF16)


def _hgrn_sample(hq, hk, hg, hv, hgate, ng, s0, t):
    bd = s0.shape[0]
    nb = 16 // t
    blk = pl.BlockSpec((A_HEADS, nb * t, A_DK), lambda i: (0, i, 0))
    sblk = pl.BlockSpec((nb, A_HEADS, A_DK, A_DV), lambda i: (i, 0, 0, 0))
    return pl.pallas_call(
        functools.partial(_hgrn_sample_kernel, t=t, nb=nb), grid=(bd // nb,),
        in_specs=[blk, blk, blk, blk, blk, pl.BlockSpec((1, A_DV), lambda i: (0, 0)), sblk],
        out_specs=[blk, sblk],
        out_shape=[jax.ShapeDtypeStruct((A_HEADS, bd * t, A_DV), BF16),
                   jax.ShapeDtypeStruct((bd, A_HEADS, A_DK, A_DV), F32)],
        compiler_params=pltpu.CompilerParams(dimension_semantics=("parallel",), vmem_limit_bytes=VMEM_LIMIT),
        name="hgrn_sample",
    )(hq, hk, hg, hv, hgate, ng, s0)


def _count_tiles(score_ref, ntiles, pred):
    def body(j, acc):
        ks = pl.multiple_of(j * ATT_TILE, ATT_TILE)
        hit = jnp.where(pred(score_ref[pl.ds(ks, ATT_TILE), :], ks), 1, 0)
        return acc + jnp.sum(hit.reshape(ATT_TILE // 8, 8, Q_BLOCK), axis=0)
    acc = lax.fori_loop(0, ntiles, body, jnp.zeros((8, Q_BLOCK), I32))
    return jnp.sum(acc, axis=0, keepdims=True)


def _kth_largest(count_ge, topk, shape, two_bits=False):
    c0 = count_ge(jnp.zeros(shape, F32))
    ok = c0 >= topk
    thr = jnp.where(ok, 0, INT_MIN)
    nge = jnp.where(ok, c0, 0)

    def place(bit, carry):
        thr, nge = carry
        cand = thr | bit
        cnt = count_ge(_key_to_float(cand))
        ok = cnt >= topk
        return jnp.where(ok, cand, thr), jnp.where(ok, cnt, nge)

    if not two_bits:
        return lax.fori_loop(0, 31, lambda i, c: place(jnp.left_shift(jnp.int32(1), 30 - i), c), (thr, nge))

    def place_two(i, carry):
        thr, nge = carry
        hi = jnp.left_shift(jnp.int32(1), 30 - 2 * i)
        lo = jnp.left_shift(jnp.int32(1), 29 - 2 * i)
        n_hi, n_lo, n_both = (count_ge(_key_to_float(thr | bits)) for bits in (hi, lo, hi | lo))
        ok_hi = n_hi >= topk
        thr1, nge1 = jnp.where(ok_hi, thr | hi, thr), jnp.where(ok_hi, n_hi, nge)
        n2 = jnp.where(ok_hi, n_both, n_lo)
        ok2 = n2 >= topk
        return jnp.where(ok2, thr1 | lo, thr1), jnp.where(ok2, n2, nge1)

    return place(jnp.int32(1), lax.fori_loop(0, 15, place_two, (thr, nge)))


def _dsa_prompt_kernel(ki_ref, qiT_ref, wT_ref, k_ref, qT_ref, vT_ref, gb_ref, o_ref,
                       score_scr, bias_scr, lg_scr, oT_scr, *, topk):
    i = pl.program_id(1)
    ntiles = (i * Q_BLOCK + Q_BLOCK + K_TILE - 1) // K_TILE
    qpos = i * Q_BLOCK + lax.broadcasted_iota(I32, (K_TILE, Q_BLOCK), 1)
    krow = lax.broadcasted_iota(I32, (K_TILE, Q_BLOCK), 0)

    qi = qiT_ref[0]
    wT = wT_ref[...]

    def score_body(j, carry):
        ks = pl.multiple_of(j * K_TILE, K_TILE)
        x = _dot(ki_ref[pl.ds(ks, K_TILE), :], qi)
        sc = jnp.zeros((K_TILE, Q_BLOCK), F32)
        for h in range(IDX_HEADS):
            sc = sc + jnp.maximum(x[:, h * Q_BLOCK:(h + 1) * Q_BLOCK], 0.0) * wT[h:h + 1]
        score_scr[pl.ds(ks, K_TILE), :] = jnp.where(krow + ks <= qpos, sc, -jnp.inf)
        return carry

    lax.fori_loop(0, ntiles, score_body, 0)

    natt = (i * Q_BLOCK + Q_BLOCK + ATT_TILE - 1) // ATT_TILE

    @pl.when(natt * (ATT_TILE // K_TILE) > ntiles)
    def _():
        score_scr[pl.ds(pl.multiple_of(ntiles * K_TILE, K_TILE), K_TILE), :] = jnp.full((K_TILE, Q_BLOCK), -jnp.inf, F32)

    count = functools.partial(_count_tiles, score_scr, natt)
    arow = lax.broadcasted_iota(I32, (ATT_TILE, Q_BLOCK), 0)
    thr, nge = _kth_largest(lambda x: count(lambda tile, ks: tile >= x), topk, (1, Q_BLOCK))
    live = thr > KEY_NEG_INF
    thr = _key_to_float(jnp.maximum(thr, KEY_NEG_INF + 1))
    no_cut = jnp.full((1, Q_BLOCK), natt * ATT_TILE, I32)
    has_ties = jnp.max(jnp.where(live & (nge > topk), 1, 0)) > 0

    def tie_cut():
        need = topk - count(lambda tile, ks: tile > thr)
        nbits = max(1, int(math.ceil(math.log2(score_scr.shape[0] + 1))))

        def body(b, pos):
            cand = pos | jnp.left_shift(jnp.int32(1), nbits - 1 - b)
            below = count(lambda tile, ks: (tile == thr) & (arow + ks < cand))
            return jnp.where(below <= need, cand, pos)

        return lax.fori_loop(0, nbits, body, jnp.zeros((1, Q_BLOCK), I32))

    cut = lax.cond(has_ties, tie_cut, lambda: no_cut)

    def bias_body(j, carry):
        ks = pl.multiple_of(j * K_TILE, K_TILE)
        tile = score_scr[pl.ds(ks, K_TILE), :]
        sel = (tile > thr) | ((tile == thr) & (krow + ks < cut))
        bias_scr[pl.ds(ks, K_TILE), :] = jnp.where(sel, 0.0, -jnp.inf)
        return carry

    lax.fori_loop(0, natt * (ATT_TILE // K_TILE), bias_body, 0)

    npair = B_HEADS // 2
    z = jnp.zeros((B_DH, Q_BLOCK), BF16)
    rhs = []
    for p in range(npair):
        qp = qT_ref[p * LANES:(p + 1) * LANES, :]
        rhs.append(jnp.concatenate([jnp.concatenate([qp[:B_DH], z], axis=0),
                                    jnp.concatenate([z, qp[B_DH:]], axis=0)], axis=1))
    fold = lambda a: a.reshape(ATT_TILE // 8, 8, a.shape[1])

    def logit_body(j, mx):
        ks = pl.multiple_of(j * ATT_TILE, ATT_TILE)
        bias = bias_scr[pl.ds(ks, ATT_TILE), :]
        bias2 = jnp.concatenate([bias, bias], axis=1)
        lgs = [_dot(k_ref[pl.ds(ks, ATT_TILE), p * LANES:(p + 1) * LANES], rhs[p]) + bias2 for p in range(npair)]
        lg_scr[pl.ds(ks, ATT_TILE), :] = jnp.concatenate(lgs, axis=1)
        return tuple(jnp.maximum(m, jnp.max(fold(lg), axis=0)) for m, lg in zip(mx, lgs))

    mx = lax.fori_loop(0, natt, logit_body,
                       tuple(jnp.full((8, 2 * Q_BLOCK), -jnp.inf, F32) for _ in range(npair)))
    mx = jnp.concatenate([jnp.max(m, axis=0, keepdims=True) for m in mx], axis=1)
    oT_scr[...] = jnp.zeros_like(oT_scr)

    def value_body(j, den):
        ks = pl.multiple_of(j * ATT_TILE, ATT_TILE)
        new_den, new_acc = [], []
        for h in range(B_HEADS):
            cols = slice(h * Q_BLOCK, (h + 1) * Q_BLOCK)
            rows = slice(h * B_DH, (h + 1) * B_DH)
            e = jnp.exp(lg_scr[pl.ds(ks, ATT_TILE), cols] - mx[:, cols])
            new_den.append(den[h] + jnp.sum(fold(e), axis=0))
            new_acc.append(oT_scr[rows, :] + _dot(vT_ref[rows, pl.ds(ks, ATT_TILE)], e.astype(BF16)))
        for h in range(B_HEADS):
            oT_scr[h * B_DH:(h + 1) * B_DH, :] = new_acc[h]
        return tuple(new_den)

    den = lax.fori_loop(0, natt, value_body, tuple(jnp.zeros((8, Q_BLOCK), F32) for _ in range(B_HEADS)))
    for h in range(B_HEADS):
        rows = slice(h * B_DH, (h + 1) * B_DH)
        oT_scr[rows, :] = oT_scr[rows, :] / jnp.sum(den[h], axis=0, keepdims=True)

    o_ref[...] = (oT_scr[...].T * gb_ref[...].astype(F32)).astype(BF16)


def _dsa_prompt(kibf, qiT2, wT, kbf, qT, vT, gbs, batch, seq):
    n = batch * seq
    nq = seq // Q_BLOCK
    topk = min(TOPK_MAX, seq // 4)
    assert seq % ATT_TILE == 0 and topk <= K_TILE
    return pl.pallas_call(
        functools.partial(_dsa_prompt_kernel, topk=topk), grid=(batch, nq),
        in_specs=[
            pl.BlockSpec((seq, IDX_DIM), lambda b, i: (b, 0)),
            pl.BlockSpec((1, IDX_DIM, IDX_HEADS * Q_BLOCK), lambda b, i: (b * nq + i, 0, 0)),
            pl.BlockSpec((IDX_HEADS, Q_BLOCK), lambda b, i: (0, b * nq + i)),
            pl.BlockSpec((seq, B_WIDTH), lambda b, i: (b, 0)),
            pl.BlockSpec((B_WIDTH, Q_BLOCK), lambda b, i: (0, b * nq + i)),
            pl.BlockSpec((B_WIDTH, seq), lambda b, i: (0, b)),
            pl.BlockSpec((Q_BLOCK, B_WIDTH), lambda b, i: (b * nq + i, 0)),
        ],
        out_specs=pl.BlockSpec((Q_BLOCK, B_WIDTH), lambda b, i: (b * nq + i, 0)),
        out_shape=jax.ShapeDtypeStruct((n, B_WIDTH), BF16),
        scratch_shapes=[
            pltpu.VMEM((seq, Q_BLOCK), F32),
            pltpu.VMEM((seq, Q_BLOCK), F32),
            pltpu.VMEM((seq, B_HEADS * Q_BLOCK), F32),
            pltpu.VMEM((B_WIDTH, Q_BLOCK), F32),
        ],
        compiler_params=pltpu.CompilerParams(dimension_semantics=("parallel", "arbitrary"),
                                             vmem_limit_bytes=VMEM_LIMIT),
        name="dsa_prompt",
    )(kibf, qiT2, wT, kbf, qT, vT, gbs)


def _dsa_sample_kernel(pt_ref, qi_ref, wrep_ref, qbd_ref, kin_ref, kn_ref, vn_ref, gb_ref,
                       cki_hbm, ck_hbm, cv_hbm, o_ref, ki_buf, k_buf, v_buf, score_scr, sem_i, sem_k, sem_v,
                       *, t, npages, topk, depth, group):
    b = pl.program_id(0)
    nb = pl.num_programs(0)
    page = ki_buf.shape[3]
    total = (npages + 1) * page
    lane = lax.broadcasted_iota(I32, (t, page), 1)
    qrow = lax.broadcasted_iota(I32, (t, page), 0)

    def ki_copy(seq, p):
        half = seq % 2
        return pltpu.make_async_copy(cki_hbm.at[pt_ref[seq, p], 0], ki_buf.at[half, p], sem_i.at[half * npages + p])

    def k_copy(seq, p):
        return pltpu.make_async_copy(ck_hbm.at[pt_ref[seq, p], 0], k_buf.at[p % depth], sem_k.at[p % depth])

    def v_copy(seq, p):
        return pltpu.make_async_copy(cv_hbm.at[pt_ref[seq, p], 0], v_buf.at[p % depth], sem_v.at[p % depth])

    def start_ki(seq):
        def body(p, carry):
            ki_copy(seq, p).start()
            return carry
        lax.fori_loop(0, npages, body, 0)

    @pl.when(b == 0)
    def _():
        start_ki(b)
        for s in range(depth):
            k_copy(b, s).start()
            v_copy(b, s).start()

    @pl.when(b + 1 < nb)
    def _():
        start_ki(b + 1)

    def scores(kidx_t):
        x = _dot(qi_ref[0], kidx_t)
        xw = jnp.maximum(x, 0.0) * wrep_ref[0]
        sc = xw[0:t]
        for h in range(1, IDX_HEADS):
            sc = sc + xw[h * t:(h + 1) * t]
        return sc

    def score_pages(i, carry):
        for g in range(group):
            ki_copy(b, i * group + g).wait()
        sc = [scores(ki_buf[b % 2, i * group + g].astype(BF16)) for g in range(group)]
        score_scr[:, pl.ds(pl.multiple_of(i * group * page, group * page), group * page)] = jnp.concatenate(sc, axis=1)
        return carry

    lax.fori_loop(0, npages // group, score_pages, 0)

    score_scr[:, npages * page:] = jnp.where(lane <= qrow, scores(kin_ref[0]), -jnp.inf)

    def count(pred):
        hit = jnp.where(pred(score_scr[...]), 1, 0)
        part = hit[:, 0:page]
        for c in range(1, npages + 1):
            part = part + hit[:, c * page:(c + 1) * page]
        return jnp.sum(part, axis=1, keepdims=True)

    thr, nge = _kth_largest(lambda x: count(lambda s: s >= x), topk, (t, 1), two_bits=True)
    live = thr > KEY_NEG_INF
    thr = _key_to_float(jnp.maximum(thr, KEY_NEG_INF + 1))
    has_ties = jnp.max(jnp.where(live & (nge > topk), 1, 0)) > 0
    pos_all = lax.broadcasted_iota(I32, (t, total), 1)

    def tie_cut():
        need = topk - count(lambda s: s > thr)
        nbits = max(1, int(math.ceil(math.log2(total + 1))))

        def pos_step(j, pos):
            cand = pos | jnp.left_shift(jnp.int32(1), nbits - 1 - j)
            below = count(lambda s: (s == thr) & (pos_all < cand))
            return jnp.where(below <= need, cand, pos)

        return lax.fori_loop(0, nbits, pos_step, jnp.zeros((t, 1), I32))

    cut = lax.cond(has_ties, tie_cut, lambda: jnp.full((t, 1), total, I32))

    qbd = qbd_ref[0]

    def attend(kv_pages, carry):
        m_old, l_old, acc = carry
        lgs = []
        for k_t, _, ks in kv_pages:
            tile = score_scr[:, pl.ds(ks, page)]
            sel = (tile > thr) | ((tile == thr) & (lane + ks < cut))
            sel = jnp.concatenate([sel.astype(I32)] * B_HEADS, axis=0) > 0
            lgs.append(jnp.where(sel, _dot(qbd, k_t), -jnp.inf))
        lg = jnp.concatenate(lgs, axis=1)
        m_new = jnp.maximum(m_old, jnp.max(lg, axis=1, keepdims=True))
        m_safe = jnp.where(m_new == -jnp.inf, 0.0, m_new)
        alpha = jnp.exp(m_old - m_safe)
        e = jnp.exp(lg - m_safe)
        l_new = alpha * l_old + jnp.sum(e, axis=1, keepdims=True)
        acc = alpha * acc
        for g, (_, v_t, _) in enumerate(kv_pages):
            acc = acc + _dot_nt(e[:, g * page:(g + 1) * page].astype(BF16), v_t)
        return m_new, l_new, acc

    def attend_pages(i, carry):
        pages = []
        for g in range(group):
            p = i * group + g
            slot = p % depth
            k_copy(b, p).wait()
            v_copy(b, p).wait()
            pages.append((k_buf[slot].astype(BF16), v_buf[slot].astype(BF16), pl.multiple_of(p * page, page)))
        carry = attend(pages, carry)
        for g in range(group):
            p = i * group + g

            @pl.when(p + depth < npages)
            def _():
                k_copy(b, p + depth).start()
                v_copy(b, p + depth).start()

            @pl.when((p + depth >= npages) & (b + 1 < nb))
            def _():
                k_copy(b + 1, p + depth - npages).start()
                v_copy(b + 1, p + depth - npages).start()

        return carry

    nrow = B_HEADS * t
    carry = (jnp.full((nrow, 1), -jnp.inf, F32), jnp.zeros((nrow, 1), F32), jnp.zeros((nrow, B_WIDTH), F32))
    carry = lax.fori_loop(0, npages // group, attend_pages, carry)
    _, den, acc = attend([(kn_ref[0], vn_ref[0], npages * page)], carry)
    o = acc / den
    col = lax.broadcasted_iota(I32, (t, B_WIDTH), 1)
    out = jnp.zeros((t, B_WIDTH), F32)
    for h in range(B_HEADS):
        out = jnp.where((col >= h * B_DH) & (col < (h + 1) * B_DH), o[h * t:(h + 1) * t], out)
    o_ref[0] = (out * gb_ref[0].astype(F32)).astype(BF16)


def _dsa_sample(page_table, qi_rows, wrep, qbd, kin_t, kn_t, vn_t, gbs, cache_kidx_t, cache_k_t, cache_v_t, t):
    bd, npages = page_table.shape
    page = cache_kidx_t.shape[3]
    total = npages * page + t
    topk = min(TOPK_MAX, total // 4)
    group = max(g for g in range(1, SAMPLE_PAGE_GROUP + 1) if npages % g == 0)
    depth = max(d for d in range(group, min(SAMPLE_DMA_DEPTH, npages) + 1, group) if npages % d == 0)
    per_b = lambda b, pt: (b, 0, 0)
    hbm = pl.BlockSpec(memory_space=pl.ANY)
    grid_spec = pltpu.PrefetchScalarGridSpec(
        num_scalar_prefetch=1, grid=(bd,),
        in_specs=[
            pl.BlockSpec((1, IDX_HEADS * t, IDX_DIM), per_b),
            pl.BlockSpec((1, IDX_HEADS * t, page), per_b),
            pl.BlockSpec((1, B_HEADS * t, B_WIDTH), per_b),
            pl.BlockSpec((1, IDX_DIM, page), per_b),
            pl.BlockSpec((1, B_WIDTH, page), per_b),
            pl.BlockSpec((1, B_WIDTH, page), per_b),
            pl.BlockSpec((1, t, B_WIDTH), per_b),
            hbm, hbm, hbm,
        ],
        out_specs=pl.BlockSpec((1, t, B_WIDTH), per_b),
        scratch_shapes=[
            pltpu.VMEM((2, npages, IDX_DIM, page), F32),
            pltpu.VMEM((depth, B_WIDTH, page), F32),
            pltpu.VMEM((depth, B_WIDTH, page), F32),
            pltpu.VMEM((t, (npages + 1) * page), F32),
            pltpu.SemaphoreType.DMA((2 * npages,)),
            pltpu.SemaphoreType.DMA((depth,)),
            pltpu.SemaphoreType.DMA((depth,)),
        ],
    )
    return pl.pallas_call(
        functools.partial(_dsa_sample_kernel, t=t, npages=npages, topk=topk, depth=depth, group=group),
        grid_spec=grid_spec,
        out_shape=jax.ShapeDtypeStruct((bd, t, B_WIDTH), BF16),
        compiler_params=pltpu.CompilerParams(dimension_semantics=("arbitrary",), vmem_limit_bytes=VMEM_LIMIT),
        name="dsa_sample",
    )(page_table, qi_rows, wrep, qbd, kin_t, kn_t, vn_t, gbs, cache_kidx_t, cache_k_t, cache_v_t)


def _merge_kernel(x_ref, ma_ref, mb_ref, w_ref, g_ref, b_ref, y_ref, *, alpha):
    mix = jnp.concatenate([ma_ref[h] for h in range(A_HEADS)] + [mb_ref[...]], axis=1)
    y = alpha * x_ref[...] + _dot(mix, w_ref[...])
    mu = jnp.mean(y, axis=-1, keepdims=True)
    d = y - mu
    var = jnp.mean(d * d, axis=-1, keepdims=True)
    y_ref[...] = d * lax.rsqrt(var + LN_EPS) * g_ref[...] + b_ref[...]


def _merge(x, mix_a, mix_b, w_out, ln_g, ln_b, alpha, tm):
    n, dm = x.shape
    return pl.pallas_call(
        functools.partial(_merge_kernel, alpha=alpha), grid=(n // tm,),
        in_specs=[
            pl.BlockSpec((tm, dm), lambda i: (i, 0)),
            pl.BlockSpec((A_HEADS, tm, A_DV), lambda i: (0, i, 0)),
            pl.BlockSpec((tm, B_WIDTH), lambda i: (i, 0)),
            pl.BlockSpec((A_WIDTH + B_WIDTH, dm), lambda i: (0, 0)),
            pl.BlockSpec((1, dm), lambda i: (0, 0)),
            pl.BlockSpec((1, dm), lambda i: (0, 0)),
        ],
        out_specs=pl.BlockSpec((tm, dm), lambda i: (i, 0)),
        out_shape=jax.ShapeDtypeStruct((n, dm), F32),
        compiler_params=pltpu.CompilerParams(dimension_semantics=("parallel",), vmem_limit_bytes=VMEM_LIMIT),
        name="merge",
    )(x, mix_a, mix_b, w_out, ln_g, ln_b)


def _split_weights(w_in_l):
    offs = np.cumsum([0, 512, 512, 512, 512, 512, 512, 512, 512, IDX_HEADS * IDX_DIM, IDX_DIM, IDX_HEADS])
    col = lambda i: w_in_l[:, offs[i]:offs[i + 1]]
    qa, fa, ia, ga, qb, kb, vb, gb, qi, ki, wi = (col(i) for i in range(11))
    pad = jnp.zeros((w_in_l.shape[0], LANES - IDX_DIM), w_in_l.dtype)
    wn = jnp.concatenate([qa, fa, ia, ga, kb, gb, ki, pad], axis=1).astype(BF16)
    wt = jnp.concatenate([qb, kb, vb, qi, wi, ki], axis=1).T.astype(BF16)
    return wn, wt


def _layer(xp, xs, cache_k, cache_v, cache_kidx, s0_sample, page_table, w_in_l, lb_l, norm_g_l, kn_g_l, kn_b_l,
           w_out_l, ln_g_l, ln_b_l, alpha):
    b, l, dm = xp.shape
    bd, t, _ = xs.shape
    npages, page = page_table.shape[1], cache_k.shape[2]
    assert page == Q_BLOCK and l % page == 0
    past = npages * page
    wn, wt = _split_weights(w_in_l)
    lb = lb_l.reshape(1, A_WIDTH)
    ng = norm_g_l.reshape(1, A_DV)
    w_out_b = w_out_l.astype(BF16)
    lng, lnb = ln_g_l.reshape(1, dm), ln_b_l.reshape(1, dm)

    tm = 256
    xp2 = xp.reshape(b * l, dm)
    (hq, hk, hg, hv, hgate, kbf, gbs, kibf, qT, vT, qiT2, wT, k_p, v_p, ki_p) = _project(
        xp2, jnp.arange(l, dtype=I32), wn, wt, lb, kn_g_l, kn_b_l, tm)
    mix_a, s_p = _hgrn_prompt(hq, hk, hg, hv, hgate, ng, b, l, min(l, 512))
    mix_b = _dsa_prompt(kibf, qiT2, wT, kbf, qT, vT, gbs, b, l)
    y_p = _merge(xp2, mix_a, mix_b, w_out_b, lng, lnb, alpha, 512).reshape(b, l, dm)

    ns = bd * t
    xs2 = xs.reshape(ns, dm)
    pos_s = past + (jnp.arange(ns, dtype=I32) % t)
    (hq, hk, hg, hv, hgate, kbf, gbs, kibf, qT, vT, qiT2, wT, k_s, v_s, ki_s) = _project(
        xs2, pos_s, wn, wt, lb, kn_g_l, kn_b_l, ns)
    mix_a, s_s = _hgrn_sample(hq, hk, hg, hv, hgate, ng, s0_sample, t)
    qi_nat = qiT2.reshape(ns // Q_BLOCK, IDX_DIM, IDX_HEADS, Q_BLOCK).transpose(0, 3, 2, 1)
    qi_rows = qi_nat.reshape(bd, t, IDX_HEADS, IDX_DIM).transpose(0, 2, 1, 3).reshape(bd, IDX_HEADS * t, IDX_DIM)
    w_rows = wT.T.reshape(bd, t, IDX_HEADS).transpose(0, 2, 1).reshape(bd, IDX_HEADS * t, 1)
    wrep = jnp.broadcast_to(w_rows, (bd, IDX_HEADS * t, page))
    q_nat = qT.T.reshape(bd, 1, t, B_HEADS, B_DH)
    eye = jnp.eye(B_HEADS, dtype=BF16).reshape(1, B_HEADS, 1, B_HEADS, 1)
    qbd = (q_nat * eye).reshape(bd, B_HEADS * t, B_WIDTH)

    def per_seq(pages):
        feat = pages.shape[1]
        a = pages.transpose(1, 0, 2).reshape(feat, bd, t).transpose(1, 0, 2).astype(BF16)
        return jnp.concatenate([a, jnp.zeros((bd, feat, page - t), BF16)], axis=2)

    token_minor = lambda c: jnp.moveaxis(c, 2, -1).reshape(c.shape[0], c.shape[1], -1, page)
    mix_b = _dsa_sample(page_table, qi_rows, wrep, qbd, per_seq(ki_s), per_seq(k_s), per_seq(v_s),
                        gbs.reshape(bd, t, B_WIDTH), token_minor(cache_kidx), token_minor(cache_k),
                        token_minor(cache_v), t).reshape(ns, B_WIDTH)
    y_s = _merge(xs2, mix_a, mix_b, w_out_b, lng, lnb, alpha, ns).reshape(bd, t, dm)

    return (y_p, y_s, k_p, v_p, ki_p, s_p, k_s, v_s, ki_s, s_s)


def kernel(x_prompt, x_sample, cache_k, cache_v, cache_kidx, state_hgrn, page_table, w_in, hgrn_lb_logits,
           hgrn_norm_g, idx_norm_g, idx_norm_b, w_out, ln_g, ln_b):
    depth = w_in.shape[0]
    assert depth == 1, "one layer per step"
    b, l, _ = x_prompt.shape
    bd, t, _ = x_sample.shape
    page = cache_k.shape[2]
    alpha = (2.0 * depth) ** 0.25
    lbs = jnp.cumsum(jax.nn.softmax(hgrn_lb_logits.astype(F32), axis=0), axis=0)[:depth]
    (y_p, y_s, k_p, v_p, ki_p, s_p, k_s, v_s, ki_s, s_s) = _layer(
        x_prompt, x_sample, cache_k, cache_v, cache_kidx, state_hgrn[0], page_table, w_in[0], lbs[0],
        hgrn_norm_g[0], idx_norm_g[0], idx_norm_b[0], w_out[0], ln_g[0], ln_b[0], alpha)
    nat_p = lambda pg, *f: jnp.moveaxis(pg.reshape(b, l // page, 1, *f, page), -1, 3)
    nat_s = lambda pg, *f: pg.transpose(0, 2, 1).reshape(bd, 1, t, *f)
    return (
        y_p, y_s,
        nat_p(k_p, B_HEADS, B_DH), nat_p(v_p, B_HEADS, B_DH), nat_p(ki_p, IDX_DIM),
        s_p[None],
        nat_s(k_s, B_HEADS, B_DH), nat_s(v_s, B_HEADS, B_DH), nat_s(ki_s, IDX_DIM),
        s_s[None],
    )
```

```python
import functools
import math

import jax
import jax.numpy as jnp
import numpy as np
from jax import lax
from jax.experimental import pallas as pl
from jax.experimental.pallas import tpu as pltpu

F32 = jnp.float32
BF16 = jnp.bfloat16
I32 = jnp.int32

A_HEADS = 4
A_DK = 128
A_DV = 128
B_HEADS = 8
B_DH = 64
IDX_HEADS = 16
IDX_DIM = 64
TOPK_MAX = 256
ROPE_THETA = 500000.0
ROT = 16
ROT_HALF = ROT // 2
LN_EPS = 1e-5
Q_BLOCK = 128
K_TILE = 256
ATT_TILE = 512
HGRN_CHUNK = 128
HGRN_SUB = 16
LANES = 128
VMEM_LIMIT = 56 * 1024 * 1024
SAMPLE_DMA_DEPTH = 32
SAMPLE_PAGE_GROUP = 8

A_WIDTH = A_HEADS * A_DV
B_WIDTH = B_HEADS * B_DH
NAT_WIDTH = 6 * 512 + LANES
TR_ROWS = 3 * 512 + IDX_HEADS * IDX_DIM + IDX_HEADS + IDX_DIM
INT_MIN = -(2 ** 31)
KEY_NEG_INF = -0x7F800000


def _dot(a, b):
    return jnp.dot(a, b, preferred_element_type=F32)


def _dot_nt(a, b):
    return lax.dot_general(a, b, (((1,), (1,)), ((), ())), preferred_element_type=F32)


def _dot_tn(a, b):
    return lax.dot_general(a, b, (((0,), (0,)), ((), ())), preferred_element_type=F32)


def _silu(x):
    return x * jax.nn.sigmoid(x)


def _key_to_float(key):
    m = key >> 31
    mag = (key ^ m) - m
    return pltpu.bitcast(mag | (m & INT_MIN), F32)


def _proj_kernel(x_ref, wn_ref, wt_ref, lb_ref, kng_ref, knb_ref, kngc_ref, knbc_ref, cn_ref, sa_ref, sb_ref,
                 ct_ref, st_ref,
                 hq_ref, hk_ref, hg_ref, hv_ref, hgate_ref, kbf_ref, gb_ref, kibf_ref,
                 qT_ref, vT_ref, qiT_ref, wT_ref, kTp_ref, vTp_ref, kiTp_ref):
    tm = x_ref.shape[0]
    xb = x_ref[...].astype(BF16)

    def nat(col, width=512):
        return _dot(xb, wn_ref[:, col:col + width])

    lb = lb_ref[...]
    qa = nat(0)
    hq = _silu(qa)
    fa = nat(512)
    hg = jnp.log2(lb + (1.0 - lb) * jax.nn.sigmoid(fa))
    hk = (1.0 - lb) * jax.nn.sigmoid(-fa)
    ia = nat(1024)
    hgate = _silu(nat(1536))
    for h in range(A_HEADS):
        sl = slice(h * A_DK, (h + 1) * A_DK)
        hq_ref[h] = hq[:, sl].astype(BF16)
        hk_ref[h] = hk[:, sl]
        hg_ref[h] = hg[:, sl]
        hv_ref[h] = ia[:, sl].astype(BF16)
        hgate_ref[h] = hgate[:, sl].astype(BF16)

    cn, sa, sb = cn_ref[...], sa_ref[...], sb_ref[...]

    def rope_nat(xc):
        return xc * cn + pltpu.roll(xc, LANES - ROT_HALF, 1) * sa + pltpu.roll(xc, ROT_HALF, 1) * sb

    kb = nat(2048)
    kbf_ref[...] = jnp.concatenate(
        [rope_nat(kb[:, c * LANES:(c + 1) * LANES]) for c in range(B_WIDTH // LANES)], axis=1).astype(BF16)
    gb_ref[...] = _silu(nat(2560)).astype(BF16)

    kic = nat(3072, LANES)
    lane = lax.broadcasted_iota(I32, (tm, LANES), 1)
    inb = lane < IDX_DIM
    mu = jnp.sum(jnp.where(inb, kic, 0.0), axis=-1, keepdims=True) * (1.0 / IDX_DIM)
    d = jnp.where(inb, kic - mu, 0.0)
    var = jnp.sum(d * d, axis=-1, keepdims=True) * (1.0 / IDX_DIM)
    kin = d * lax.rsqrt(var + LN_EPS) * kng_ref[...] + knb_ref[...]
    kibf_ref[...] = rope_nat(kin)[:, :IDX_DIM].astype(BF16)

    ct, st = ct_ref[...], st_ref[...]
    npage = tm // Q_BLOCK

    def tr(row, height):
        return _dot_nt(wt_ref[row:row + height, :], xb)

    def rope_tr(blk):
        x1, x2 = blk[0:ROT_HALF], blk[ROT_HALF:ROT]
        return jnp.concatenate([x1 * ct - x2 * st, x1 * st + x2 * ct, blk[ROT:]], axis=0)

    qbT = tr(0, B_WIDTH)
    for h in range(B_HEADS):
        blk = rope_tr(qbT[h * B_DH:(h + 1) * B_DH])
        qT_ref[h * B_DH:(h + 1) * B_DH, :] = (blk * (B_DH ** -0.5)).astype(BF16)
    kbT = tr(B_WIDTH, B_WIDTH)
    for h in range(B_HEADS):
        blk = rope_tr(kbT[h * B_DH:(h + 1) * B_DH])
        for pg in range(npage):
            kTp_ref[pg, h * B_DH:(h + 1) * B_DH, :] = blk[:, pg * Q_BLOCK:(pg + 1) * Q_BLOCK]
    vbT = tr(2 * B_WIDTH, B_WIDTH)
    vT_ref[...] = vbT.astype(BF16)
    for pg in range(npage):
        vTp_ref[pg] = vbT[:, pg * Q_BLOCK:(pg + 1) * Q_BLOCK]
    row = 3 * B_WIDTH
    qiT = tr(row, IDX_HEADS * IDX_DIM)
    for h in range(IDX_HEADS):
        blk = rope_tr(qiT[h * IDX_DIM:(h + 1) * IDX_DIM]).astype(BF16)
        for pg in range(npage):
            qiT_ref[pg, :, h * Q_BLOCK:(h + 1) * Q_BLOCK] = blk[:, pg * Q_BLOCK:(pg + 1) * Q_BLOCK]
    row += IDX_HEADS * IDX_DIM
    wT_ref[...] = tr(row, IDX_HEADS) * (IDX_HEADS ** -0.5 * IDX_DIM ** -0.5)
    row += IDX_HEADS
    kiT = tr(row, IDX_DIM)
    muT = jnp.mean(kiT, axis=0, keepdims=True)
    dT = kiT - muT
    varT = jnp.mean(dT * dT, axis=0, keepdims=True)
    kiT = rope_tr(dT * lax.rsqrt(varT + LN_EPS) * kngc_ref[...] + knbc_ref[...])
    for pg in range(npage):
        kiTp_ref[pg] = kiT[:, pg * Q_BLOCK:(pg + 1) * Q_BLOCK]


def _rope_tables(pos):
    inv = ROPE_THETA ** (-jnp.arange(ROT_HALF, dtype=F32) / ROT_HALF)
    ang = pos.astype(F32)[:, None] * inv[None, :]
    cos, sin = jnp.cos(ang), jnp.sin(ang)
    p = pos.shape[0]
    one = jnp.ones((p, B_DH - ROT), F32)
    zero8 = jnp.zeros((p, ROT_HALF), F32)
    zero = jnp.zeros((p, B_DH - ROT), F32)
    cn = jnp.concatenate([cos, cos, one], axis=1)
    sa = jnp.concatenate([-sin, zero8, zero], axis=1)
    sb = jnp.concatenate([zero8, sin, zero], axis=1)
    tile2 = lambda a: jnp.concatenate([a, a], axis=1)
    return tile2(cn), tile2(sa), tile2(sb), cos.T, sin.T


def _project(x, pos, wn, wt, lb, kn_g, kn_b, tm):
    n, dm = x.shape
    p = pos.shape[0]
    nper = p // tm
    cn, sa, sb, ct, st = _rope_tables(pos)
    padl = lambda a: jnp.concatenate([a, jnp.zeros((LANES - IDX_DIM,), a.dtype)]).reshape(1, LANES)
    grid = (n // tm,)
    row = lambda i: (i, 0)
    full = lambda i: (0, 0)
    head = lambda i: (0, i, 0)
    page = lambda i: (i, 0, 0)
    per = lambda i: (i % nper, 0)
    perT = lambda i: (0, i % nper)
    colT = lambda i: (0, i)
    in_specs = [
        pl.BlockSpec((tm, dm), row),
        pl.BlockSpec((dm, NAT_WIDTH), full, pipeline_mode=pl.Buffered(1)),
        pl.BlockSpec((TR_ROWS, dm), full, pipeline_mode=pl.Buffered(1)),
        pl.BlockSpec((1, A_WIDTH), full),
        pl.BlockSpec((1, LANES), full),
        pl.BlockSpec((1, LANES), full),
        pl.BlockSpec((IDX_DIM, 1), full),
        pl.BlockSpec((IDX_DIM, 1), full),
        pl.BlockSpec((tm, LANES), per),
        pl.BlockSpec((tm, LANES), per),
        pl.BlockSpec((tm, LANES), per),
        pl.BlockSpec((ROT_HALF, tm), perT),
        pl.BlockSpec((ROT_HALF, tm), perT),
    ]
    npage = tm // Q_BLOCK
    hshape = lambda dt: jax.ShapeDtypeStruct((A_HEADS, n, A_DK), dt)
    out_shape = [
        hshape(BF16), hshape(F32), hshape(F32), hshape(BF16), hshape(BF16),
        jax.ShapeDtypeStruct((n, B_WIDTH), BF16), jax.ShapeDtypeStruct((n, B_WIDTH), BF16),
        jax.ShapeDtypeStruct((n, IDX_DIM), BF16),
        jax.ShapeDtypeStruct((B_WIDTH, n), BF16), jax.ShapeDtypeStruct((B_WIDTH, n), BF16),
        jax.ShapeDtypeStruct((n // Q_BLOCK, IDX_DIM, IDX_HEADS * Q_BLOCK), BF16),
        jax.ShapeDtypeStruct((IDX_HEADS, n), F32),
        jax.ShapeDtypeStruct((n // Q_BLOCK, B_WIDTH, Q_BLOCK), F32),
        jax.ShapeDtypeStruct((n // Q_BLOCK, B_WIDTH, Q_BLOCK), F32),
        jax.ShapeDtypeStruct((n // Q_BLOCK, IDX_DIM, Q_BLOCK), F32),
    ]
    hspec = pl.BlockSpec((A_HEADS, tm, A_DK), head)
    out_specs = [
        hspec, hspec, hspec, hspec, hspec,
        pl.BlockSpec((tm, B_WIDTH), row), pl.BlockSpec((tm, B_WIDTH), row),
        pl.BlockSpec((tm, IDX_DIM), row),
        pl.BlockSpec((B_WIDTH, tm), colT), pl.BlockSpec((B_WIDTH, tm), colT),
        pl.BlockSpec((npage, IDX_DIM, IDX_HEADS * Q_BLOCK), page),
        pl.BlockSpec((IDX_HEADS, tm), colT),
        pl.BlockSpec((npage, B_WIDTH, Q_BLOCK), page),
        pl.BlockSpec((npage, B_WIDTH, Q_BLOCK), page),
        pl.BlockSpec((npage, IDX_DIM, Q_BLOCK), page),
    ]
    return pl.pallas_call(
        _proj_kernel, grid=grid, in_specs=in_specs, out_specs=out_specs, out_shape=out_shape,
        compiler_params=pltpu.CompilerParams(dimension_semantics=("parallel",), vmem_limit_bytes=VMEM_LIMIT),
        name="proj",
    )(x, wn, wt, lb, padl(kn_g), padl(kn_b), kn_g.reshape(IDX_DIM, 1), kn_b.reshape(IDX_DIM, 1),
      cn, sa, sb, ct, st)


def _hgrn_cumsum(g, c):
    if c == LANES:
        ri = lax.broadcasted_iota(I32, (c, c), 0)
        ci = lax.broadcasted_iota(I32, (c, c), 1)
        tri = jnp.where(ci <= ri, 1.0, 0.0).astype(BF16)
        g1 = g.astype(BF16)
        e1 = g - g1.astype(F32)
        g2 = e1.astype(BF16)
        g3 = (e1 - g2.astype(F32)).astype(BF16)
        return _dot(tri, g1) + _dot(tri, g2) + _dot(tri, g3)
    ri = lax.broadcasted_iota(I32, (c, A_DK), 0)
    cum = jnp.zeros((c, A_DK), F32)
    for s in range(c):
        cum = cum + jnp.where(ri >= s, g[s:s + 1], 0.0)
    return cum


def _hgrn_chunks(chains, ng, c):
    r = HGRN_SUB
    nsub = c // r
    rowi = lax.broadcasted_iota(I32, (r, A_DK), 0)
    lane = lax.broadcasted_iota(I32, (r, LANES), 1)
    cums = [_hgrn_cumsum(g, c) for _, _, g, _, _, _ in chains]

    light = []
    for (q, k, _, v, _, st), cum in zip(chains, cums):
        o_state = _dot_nt((q * jnp.exp2(cum)).astype(BF16), st.astype(BF16))
        off = [None]
        for i in range(1, nsub):
            ref = cum[i * r - 1:i * r]
            qt = (q[i * r:(i + 1) * r] * jnp.exp2(cum[i * r:(i + 1) * r] - ref)).astype(BF16)
            kt = (k[:i * r] * jnp.exp2(ref - cum[:i * r])).astype(BF16)
            kt = jnp.concatenate([kt, jnp.zeros((LANES - i * r, A_DK), BF16)], axis=0)
            off.append(_dot_nt(qt, kt))
        vb = v.astype(BF16)
        last = cum[c - 1:c]
        kh = (k * jnp.exp2(last - cum)).astype(BF16)
        if c < LANES:
            zpad = jnp.zeros((LANES - c, A_DK), BF16)
            vb = jnp.concatenate([vb, zpad], axis=0)
            kh = jnp.concatenate([kh, zpad], axis=0)
        st_new = st * jnp.exp2(last) + _dot_tn(vb, kh)
        light.append((o_state, off, vb, st_new))

    rsums = []
    for (q, k, _, _, _, _), cum in zip(chains, cums):
        prods = []
        for i in range(nsub):
            qs, cs = q[i * r:(i + 1) * r], cum[i * r:(i + 1) * r]
            for s in range(r):
                row = i * r + s
                p = (qs * k[row:row + 1]) * jnp.exp2(cs - cum[row:row + 1])
                prods.append(jnp.where(rowi >= s, p, 0.0).astype(BF16))
        if nsub % 2 == 0:
            half = len(prods) // 2
            both = jnp.concatenate([jnp.concatenate(prods[:half], axis=0), jnp.concatenate(prods[half:], axis=0)],
                                   axis=1)
            wr = lax.broadcasted_iota(I32, (2 * A_DK, 2 * LANES), 0) < A_DK
            wc = lax.broadcasted_iota(I32, (2 * A_DK, 2 * LANES), 1) < LANES
            sums = _dot(both, jnp.where(wr == wc, 1.0, 0.0).astype(BF16))
            rsums.append(lambda row, sums=sums, half=half: (
                sums[row * r:(row + 1) * r, :LANES] if row < half
                else sums[(row - half) * r:(row - half + 1) * r, LANES:]))
        else:
            sums = _dot(jnp.concatenate(prods, axis=0), jnp.ones((A_DK, LANES), BF16))
            rsums.append(lambda row, sums=sums: sums[row * r:(row + 1) * r])

    results = []
    for (_, _, _, _, gate, _), (o_state, off, vb, st_new), rsum in zip(chains, light, rsums):
        sc_rows = []
        for i in range(nsub):
            sci = jnp.zeros((r, LANES), F32)
            for s in range(r):
                row = i * r + s
                sci = jnp.where(lane == row, rsum(row), sci)
            sc_rows.append(sci if i == 0 else sci + off[i])
        sc = jnp.concatenate(sc_rows, axis=0).astype(BF16)
        o = o_state + _dot(sc, vb)
        ms = jnp.mean(o * o, axis=-1, keepdims=True)
        results.append((o * lax.rsqrt(ms + LN_EPS) * ng * gate, st_new))
    return results


def _hgrn_prompt_kernel(q_ref, k_ref, g_ref, v_ref, gate_ref, ng_ref, o_ref, sf_ref, st_scr):
    c = HGRN_CHUNK
    nchunk = q_ref.shape[1] // c
    tb = pl.program_id(1)

    @pl.when(tb == 0)
    def _():
        st_scr[...] = jnp.zeros_like(st_scr)

    ng = ng_ref[...]

    def body(j, carry):
        sl = pl.ds(pl.multiple_of(j * c, c), c)
        loaded = [(q_ref[h, sl, :].astype(F32), k_ref[h, sl, :], g_ref[h, sl, :], v_ref[h, sl, :].astype(F32),
                   gate_ref[h, sl, :].astype(F32), st_scr[h]) for h in range(A_HEADS)]
        for h, (o, st_new) in enumerate(_hgrn_chunks(loaded, ng, c)):
            st_scr[h] = st_new
            o_ref[h, sl, :] = o.astype(BF16)
        return carry

    lax.fori_loop(0, nchunk, body, 0)

    @pl.when(tb == pl.num_programs(1) - 1)
    def _():
        for h in range(A_HEADS):
            sf_ref[0, h] = st_scr[h].T


def _hgrn_prompt(hq, hk, hg, hv, hgate, ng, batch, seq, tb):
    n = batch * seq
    nt = seq // tb
    blk = pl.BlockSpec((A_HEADS, tb, A_DK), lambda b, t: (0, b * nt + t, 0))
    return pl.pallas_call(
        _hgrn_prompt_kernel, grid=(batch, nt),
        in_specs=[blk, blk, blk, blk, blk, pl.BlockSpec((1, A_DV), lambda b, t: (0, 0))],
        out_specs=[blk, pl.BlockSpec((1, A_HEADS, A_DK, A_DV), lambda b, t: (b, 0, 0, 0))],
        out_shape=[jax.ShapeDtypeStruct((A_HEADS, n, A_DV), BF16),
                   jax.ShapeDtypeStruct((batch, A_HEADS, A_DK, A_DV), F32)],
        scratch_shapes=[pltpu.VMEM((A_HEADS, A_DV, A_DK), F32)],
        compiler_params=pltpu.CompilerParams(dimension_semantics=("parallel", "arbitrary"),
                                             vmem_limit_bytes=VMEM_LIMIT),
        name="hgrn_prompt",
    )(hq, hk, hg, hv, hgate, ng)


def _hgrn_sample_kernel(q_ref, k_ref, g_ref, v_ref, gate_ref, ng_ref, s0_ref, o_ref, sf_ref, *, t, nb):
    c = HGRN_SUB
    ng = ng_ref[...]
    zpad = jnp.zeros((c - t, A_DK), F32)
    pad = lambda a: jnp.concatenate([a, zpad], axis=0)
    chains = []
    for h in range(A_HEADS):
        q, k, g = q_ref[h].astype(F32), k_ref[h], g_ref[h]
        v, gate = v_ref[h].astype(F32), gate_ref[h].astype(F32)
        for b in range(nb):
            sl = slice(b * t, (b + 1) * t)
            chains.append((pad(q[sl]), pad(k[sl]), pad(g[sl]), pad(v[sl]), pad(gate[sl]), s0_ref[b, h].T))
    results = _hgrn_chunks(chains, ng, c)
    for h in range(A_HEADS):
        for b in range(nb):
            sf_ref[b, h] = results[h * nb + b][1].T
        o_ref[h] = jnp.concatenate([results[h * nb + b][0][:t] for b in range(nb)], axis=0).astype(BF16)


def _hgrn_sample(hq, hk, hg, hv, hgate, ng, s0, t):
    bd = s0.shape[0]
    nb = 16 // t
    blk = pl.BlockSpec((A_HEADS, nb * t, A_DK), lambda i: (0, i, 0))
    sblk = pl.BlockSpec((nb, A_HEADS, A_DK, A_DV), lambda i: (i, 0, 0, 0))
    return pl.pallas_call(
        functools.partial(_hgrn_sample_kernel, t=t, nb=nb), grid=(bd // nb,),
        in_specs=[blk, blk, blk, blk, blk, pl.BlockSpec((1, A_DV), lambda i: (0, 0)), sblk],
        out_specs=[blk, sblk],
        out_shape=[jax.ShapeDtypeStruct((A_HEADS, bd * t, A_DV), BF16),
                   jax.ShapeDtypeStruct((bd, A_HEADS, A_DK, A_DV), F32)],
        compiler_params=pltpu.CompilerParams(dimension_semantics=("parallel",), vmem_limit_bytes=VMEM_LIMIT),
        name="hgrn_sample",
    )(hq, hk, hg, hv, hgate, ng, s0)


def _count_tiles(score_ref, ntiles, pred):
    def body(j, acc):
        ks = pl.multiple_of(j * ATT_TILE, ATT_TILE)
        hit = jnp.where(pred(score_ref[pl.ds(ks, ATT_TILE), :], ks), 1, 0)
        return acc + jnp.sum(hit.reshape(ATT_TILE // 8, 8, Q_BLOCK), axis=0)
    acc = lax.fori_loop(0, ntiles, body, jnp.zeros((8, Q_BLOCK), I32))
    return jnp.sum(acc, axis=0, keepdims=True)


def _kth_largest(count_ge, topk, shape, two_bits=False):
    c0 = count_ge(jnp.zeros(shape, F32))
    ok = c0 >= topk
    thr = jnp.where(ok, 0, INT_MIN)
    nge = jnp.where(ok, c0, 0)

    def place(bit, carry):
        thr, nge = carry
        cand = thr | bit
        cnt = count_ge(_key_to_float(cand))
        ok = cnt >= topk
        return jnp.where(ok, cand, thr), jnp.where(ok, cnt, nge)

    if not two_bits:
        return lax.fori_loop(0, 31, lambda i, c: place(jnp.left_shift(jnp.int32(1), 30 - i), c), (thr, nge))

    def place_two(i, carry):
        thr, nge = carry
        hi = jnp.left_shift(jnp.int32(1), 30 - 2 * i)
        lo = jnp.left_shift(jnp.int32(1), 29 - 2 * i)
        n_hi, n_lo, n_both = (count_ge(_key_to_float(thr | bits)) for bits in (hi, lo, hi | lo))
        ok_hi = n_hi >= topk
        thr1, nge1 = jnp.where(ok_hi, thr | hi, thr), jnp.where(ok_hi, n_hi, nge)
        n2 = jnp.where(ok_hi, n_both, n_lo)
        ok2 = n2 >= topk
        return jnp.where(ok2, thr1 | lo, thr1), jnp.where(ok2, n2, nge1)

    return place(jnp.int32(1), lax.fori_loop(0, 15, place_two, (thr, nge)))


def _dsa_prompt_kernel(ki_ref, qiT_ref, wT_ref, k_ref, qT_ref, vT_ref, gb_ref, o_ref,
                       score_scr, bias_scr, lg_scr, oT_scr, *, topk):
    i = pl.program_id(1)
    ntiles = (i * Q_BLOCK + Q_BLOCK + K_TILE - 1) // K_TILE
    qpos = i * Q_BLOCK + lax.broadcasted_iota(I32, (K_TILE, Q_BLOCK), 1)
    krow = lax.broadcasted_iota(I32, (K_TILE, Q_BLOCK), 0)

    qi = qiT_ref[0]
    wT = wT_ref[...]

    def score_body(j, carry):
        ks = pl.multiple_of(j * K_TILE, K_TILE)
        x = _dot(ki_ref[pl.ds(ks, K_TILE), :], qi)
        sc = jnp.zeros((K_TILE, Q_BLOCK), F32)
        for h in range(IDX_HEADS):
            sc = sc + jnp.maximum(x[:, h * Q_BLOCK:(h + 1) * Q_BLOCK], 0.0) * wT[h:h + 1]
        score_scr[pl.ds(ks, K_TILE), :] = jnp.where(krow + ks <= qpos, sc, -jnp.inf)
        return carry

    lax.fori_loop(0, ntiles, score_body, 0)

    natt = (i * Q_BLOCK + Q_BLOCK + ATT_TILE - 1) // ATT_TILE

    @pl.when(natt * (ATT_TILE // K_TILE) > ntiles)
    def _():
        score_scr[pl.ds(pl.multiple_of(ntiles * K_TILE, K_TILE), K_TILE), :] = jnp.full((K_TILE, Q_BLOCK), -jnp.inf, F32)

    count = functools.partial(_count_tiles, score_scr, natt)
    arow = lax.broadcasted_iota(I32, (ATT_TILE, Q_BLOCK), 0)
    thr, nge = _kth_largest(lambda x: count(lambda tile, ks: tile >= x), topk, (1, Q_BLOCK))
    live = thr > KEY_NEG_INF
    thr = _key_to_float(jnp.maximum(thr, KEY_NEG_INF + 1))
    no_cut = jnp.full((1, Q_BLOCK), natt * ATT_TILE, I32)
    has_ties = jnp.max(jnp.where(live & (nge > topk), 1, 0)) > 0

    def tie_cut():
        need = topk - count(lambda tile, ks: tile > thr)
        nbits = max(1, int(math.ceil(math.log2(score_scr.shape[0] + 1))))

        def body(b, pos):
            cand = pos | jnp.left_shift(jnp.int32(1), nbits - 1 - b)
            below = count(lambda tile, ks: (tile == thr) & (arow + ks < cand))
            return jnp.where(below <= need, cand, pos)

        return lax.fori_loop(0, nbits, body, jnp.zeros((1, Q_BLOCK), I32))

    cut = lax.cond(has_ties, tie_cut, lambda: no_cut)

    def bias_body(j, carry):
        ks = pl.multiple_of(j * K_TILE, K_TILE)
        tile = score_scr[pl.ds(ks, K_TILE), :]
        sel = (tile > thr) | ((tile == thr) & (krow + ks < cut))
        bias_scr[pl.ds(ks, K_TILE), :] = jnp.where(sel, 0.0, -jnp.inf)
        return carry

    lax.fori_loop(0, natt * (ATT_TILE // K_TILE), bias_body, 0)

    npair = B_HEADS // 2
    z = jnp.zeros((B_DH, Q_BLOCK), BF16)
    rhs = []
    for p in range(npair):
        qp = qT_ref[p * LANES:(p + 1) * LANES, :]
        rhs.append(jnp.concatenate([jnp.concatenate([qp[:B_DH], z], axis=0),
                                    jnp.concatenate([z, qp[B_DH:]], axis=0)], axis=1))
    fold = lambda a: a.reshape(ATT_TILE // 8, 8, a.shape[1])

    def logit_body(j, mx):
        ks = pl.multiple_of(j * ATT_TILE, ATT_TILE)
        bias = bias_scr[pl.ds(ks, ATT_TILE), :]
        bias2 = jnp.concatenate([bias, bias], axis=1)
        lgs = [_dot(k_ref[pl.ds(ks, ATT_TILE), p * LANES:(p + 1) * LANES], rhs[p]) + bias2 for p in range(npair)]
        lg_scr[pl.ds(ks, ATT_TILE), :] = jnp.concatenate(lgs, axis=1)
        return tuple(jnp.maximum(m, jnp.max(fold(lg), axis=0)) for m, lg in zip(mx, lgs))

    mx = lax.fori_loop(0, natt, logit_body,
                       tuple(jnp.full((8, 2 * Q_BLOCK), -jnp.inf, F32) for _ in range(npair)))
    mx = jnp.concatenate([jnp.max(m, axis=0, keepdims=True) for m in mx], axis=1)
    oT_scr[...] = jnp.zeros_like(oT_scr)

    def value_body(j, den):
        ks = pl.multiple_of(j * ATT_TILE, ATT_TILE)
        new_den, new_acc = [], []
        for h in range(B_HEADS):
            cols = slice(h * Q_BLOCK, (h + 1) * Q_BLOCK)
            rows = slice(h * B_DH, (h + 1) * B_DH)
            e = jnp.exp(lg_scr[pl.ds(ks, ATT_TILE), cols] - mx[:, cols])
            new_den.append(den[h] + jnp.sum(fold(e), axis=0))
            new_acc.append(oT_scr[rows, :] + _dot(vT_ref[rows, pl.ds(ks, ATT_TILE)], e.astype(BF16)))
        for h in range(B_HEADS):
            oT_scr[h * B_DH:(h + 1) * B_DH, :] = new_acc[h]
        return tuple(new_den)

    den = lax.fori_loop(0, natt, value_body, tuple(jnp.zeros((8, Q_BLOCK), F32) for _ in range(B_HEADS)))
    for h in range(B_HEADS):
        rows = slice(h * B_DH, (h + 1) * B_DH)
        oT_scr[rows, :] = oT_scr[rows, :] / jnp.sum(den[h], axis=0, keepdims=True)

    o_ref[...] = (oT_scr[...].T * gb_ref[...].astype(F32)).astype(BF16)


def _dsa_prompt(kibf, qiT2, wT, kbf, qT, vT, gbs, batch, seq):
    n = batch * seq
    nq = seq // Q_BLOCK
    topk = min(TOPK_MAX, seq // 4)
    assert seq % ATT_TILE == 0 and topk <= K_TILE
    return pl.pallas_call(
        functools.partial(_dsa_prompt_kernel, topk=topk), grid=(batch, nq),
        in_specs=[
            pl.BlockSpec((seq, IDX_DIM), lambda b, i: (b, 0)),
            pl.BlockSpec((1, IDX_DIM, IDX_HEADS * Q_BLOCK), lambda b, i: (b * nq + i, 0, 0)),
            pl.BlockSpec((IDX_HEADS, Q_BLOCK), lambda b, i: (0, b * nq + i)),
            pl.BlockSpec((seq, B_WIDTH), lambda b, i: (b, 0)),
            pl.BlockSpec((B_WIDTH, Q_BLOCK), lambda b, i: (0, b * nq + i)),
            pl.BlockSpec((B_WIDTH, seq), lambda b, i: (0, b)),
            pl.BlockSpec((Q_BLOCK, B_WIDTH), lambda b, i: (b * nq + i, 0)),
        ],
        out_specs=pl.BlockSpec((Q_BLOCK, B_WIDTH), lambda b, i: (b * nq + i, 0)),
        out_shape=jax.ShapeDtypeStruct((n, B_WIDTH), BF16),
        scratch_shapes=[
            pltpu.VMEM((seq, Q_BLOCK), F32),
            pltpu.VMEM((seq, Q_BLOCK), F32),
            pltpu.VMEM((seq, B_HEADS * Q_BLOCK), F32),
            pltpu.VMEM((B_WIDTH, Q_BLOCK), F32),
        ],
        compiler_params=pltpu.CompilerParams(dimension_semantics=("parallel", "arbitrary"),
                                             vmem_limit_bytes=VMEM_LIMIT),
        name="dsa_prompt",
    )(kibf, qiT2, wT, kbf, qT, vT, gbs)


def _dsa_sample_kernel(pt_ref, qi_ref, wrep_ref, qbd_ref, kin_ref, kn_ref, vn_ref, gb_ref,
                       cki_hbm, ck_hbm, cv_hbm, o_ref, ki_buf, k_buf, v_buf, score_scr, sem_i, sem_k, sem_v,
                       *, t, npages, topk, depth, group):
    b = pl.program_id(0)
    nb = pl.num_programs(0)
    page = ki_buf.shape[3]
    total = (npages + 1) * page
    lane = lax.broadcasted_iota(I32, (t, page), 1)
    qrow = lax.broadcasted_iota(I32, (t, page), 0)

    def ki_copy(seq, p):
        half = seq % 2
        return pltpu.make_async_copy(cki_hbm.at[pt_ref[seq, p], 0], ki_buf.at[half, p], sem_i.at[half * npages + p])

    def k_copy(seq, p):
        return pltpu.make_async_copy(ck_hbm.at[pt_ref[seq, p], 0], k_buf.at[p % depth], sem_k.at[p % depth])

    def v_copy(seq, p):
        return pltpu.make_async_copy(cv_hbm.at[pt_ref[seq, p], 0], v_buf.at[p % depth], sem_v.at[p % depth])

    def start_ki(seq):
        def body(p, carry):
            ki_copy(seq, p).start()
            return carry
        lax.fori_loop(0, npages, body, 0)

    @pl.when(b == 0)
    def _():
        start_ki(b)
        for s in range(depth):
            k_copy(b, s).start()
            v_copy(b, s).start()

    @pl.when(b + 1 < nb)
    def _():
        start_ki(b + 1)

    def scores(kidx_t):
        x = _dot(qi_ref[0], kidx_t)
        xw = jnp.maximum(x, 0.0) * wrep_ref[0]
        sc = xw[0:t]
        for h in range(1, IDX_HEADS):
            sc = sc + xw[h * t:(h + 1) * t]
        return sc

    def score_pages(i, carry):
        for g in range(group):
            ki_copy(b, i * group + g).wait()
        sc = [scores(ki_buf[b % 2, i * group + g].astype(BF16)) for g in range(group)]
        score_scr[:, pl.ds(pl.multiple_of(i * group * page, group * page), group * page)] = jnp.concatenate(sc, axis=1)
        return carry

    lax.fori_loop(0, npages // group, score_pages, 0)

    score_scr[:, npages * page:] = jnp.where(lane <= qrow, scores(kin_ref[0]), -jnp.inf)

    def count(pred):
        hit = jnp.where(pred(score_scr[...]), 1, 0)
        part = hit[:, 0:page]
        for c in range(1, npages + 1):
            part = part + hit[:, c * page:(c + 1) * page]
        return jnp.sum(part, axis=1, keepdims=True)

    thr, nge = _kth_largest(lambda x: count(lambda s: s >= x), topk, (t, 1), two_bits=True)
    live = thr > KEY_NEG_INF
    thr = _key_to_float(jnp.maximum(thr, KEY_NEG_INF + 1))
    has_ties = jnp.max(jnp.where(live & (nge > topk), 1, 0)) > 0
    pos_all = lax.broadcasted_iota(I32, (t, total), 1)

    def tie_cut():
        need = topk - count(lambda s: s > thr)
        nbits = max(1, int(math.ceil(math.log2(total + 1))))

        def pos_step(j, pos):
            cand = pos | jnp.left_shift(jnp.int32(1), nbits - 1 - j)
            below = count(lambda s: (s == thr) & (pos_all < cand))
            return jnp.where(below <= need, cand, pos)

        return lax.fori_loop(0, nbits, pos_step, jnp.zeros((t, 1), I32))

    cut = lax.cond(has_ties, tie_cut, lambda: jnp.full((t, 1), total, I32))

    qbd = qbd_ref[0]

    def attend(kv_pages, carry):
        m_old, l_old, acc = carry
        lgs = []
        for k_t, _, ks in kv_pages:
            tile = score_scr[:, pl.ds(ks, page)]
            sel = (tile > thr) | ((tile == thr) & (lane + ks < cut))
            sel = jnp.concatenate([sel.astype(I32)] * B_HEADS, axis=0) > 0
            lgs.append(jnp.where(sel, _dot(qbd, k_t), -jnp.inf))
        lg = jnp.concatenate(lgs, axis=1)
        m_new = jnp.maximum(m_old, jnp.max(lg, axis=1, keepdims=True))
        m_safe = jnp.where(m_new == -jnp.inf, 0.0, m_new)
        alpha = jnp.exp(m_old - m_safe)
        e = jnp.exp(lg - m_safe)
        l_new = alpha * l_old + jnp.sum(e, axis=1, keepdims=True)
        acc = alpha * acc
        for g, (_, v_t, _) in enumerate(kv_pages):
            acc = acc + _dot_nt(e[:, g * page:(g + 1) * page].astype(BF16), v_t)
        return m_new, l_new, acc

    def attend_pages(i, carry):
        pages = []
        for g in range(group):
            p = i * group + g
            slot = p % depth
            k_copy(b, p).wait()
            v_copy(b, p).wait()
            pages.append((k_buf[slot].astype(BF16), v_buf[slot].astype(BF16), pl.multiple_of(p * page, page)))
        carry = attend(pages, carry)
        for g in range(group):
            p = i * group + g

            @pl.when(p + depth < npages)
            def _():
                k_copy(b, p + depth).start()
                v_copy(b, p + depth).start()

            @pl.when((p + depth >= npages) & (b + 1 < nb))
            def _():
                k_copy(b + 1, p + depth - npages).start()
                v_copy(b + 1, p + depth - npages).start()

        return carry

    nrow = B_HEADS * t
    carry = (jnp.full((nrow, 1), -jnp.inf, F32), jnp.zeros((nrow, 1), F32), jnp.zeros((nrow, B_WIDTH), F32))
    carry = lax.fori_loop(0, npages // group, attend_pages, carry)
    _, den, acc = attend([(kn_ref[0], vn_ref[0], npages * page)], carry)
    o = acc / den
    col = lax.broadcasted_iota(I32, (t, B_WIDTH), 1)
    out = jnp.zeros((t, B_WIDTH), F32)
    for h in range(B_HEADS):
        out = jnp.where((col >= h * B_DH) & (col < (h + 1) * B_DH), o[h * t:(h + 1) * t], out)
    o_ref[0] = (out * gb_ref[0].astype(F32)).astype(BF16)


def _dsa_sample(page_table, qi_rows, wrep, qbd, kin_t, kn_t, vn_t, gbs, cache_kidx_t, cache_k_t, cache_v_t, t):
    bd, npages = page_table.shape
    page = cache_kidx_t.shape[3]
    total = npages * page + t
    topk = min(TOPK_MAX, total // 4)
    group = max(g for g in range(1, SAMPLE_PAGE_GROUP + 1) if npages % g == 0)
    depth = max(d for d in range(group, min(SAMPLE_DMA_DEPTH, npages) + 1, group) if npages % d == 0)
    per_b = lambda b, pt: (b, 0, 0)
    hbm = pl.BlockSpec(memory_space=pl.ANY)
    grid_spec = pltpu.PrefetchScalarGridSpec(
        num_scalar_prefetch=1, grid=(bd,),
        in_specs=[
            pl.BlockSpec((1, IDX_HEADS * t, IDX_DIM), per_b),
            pl.BlockSpec((1, IDX_HEADS * t, page), per_b),
            pl.BlockSpec((1, B_HEADS * t, B_WIDTH), per_b),
            pl.BlockSpec((1, IDX_DIM, page), per_b),
            pl.BlockSpec((1, B_WIDTH, page), per_b),
            pl.BlockSpec((1, B_WIDTH, page), per_b),
            pl.BlockSpec((1, t, B_WIDTH), per_b),
            hbm, hbm, hbm,
        ],
        out_specs=pl.BlockSpec((1, t, B_WIDTH), per_b),
        scratch_shapes=[
            pltpu.VMEM((2, npages, IDX_DIM, page), F32),
            pltpu.VMEM((depth, B_WIDTH, page), F32),
            pltpu.VMEM((depth, B_WIDTH, page), F32),
            pltpu.VMEM((t, (npages + 1) * page), F32),
            pltpu.SemaphoreType.DMA((2 * npages,)),
            pltpu.SemaphoreType.DMA((depth,)),
            pltpu.SemaphoreType.DMA((depth,)),
        ],
    )
    return pl.pallas_call(
        functools.partial(_dsa_sample_kernel, t=t, npages=npages, topk=topk, depth=depth, group=group),
        grid_spec=grid_spec,
        out_shape=jax.ShapeDtypeStruct((bd, t, B_WIDTH), BF16),
        compiler_params=pltpu.CompilerParams(dimension_semantics=("arbitrary",), vmem_limit_bytes=VMEM_LIMIT),
        name="dsa_sample",
    )(page_table, qi_rows, wrep, qbd, kin_t, kn_t, vn_t, gbs, cache_kidx_t, cache_k_t, cache_v_t)


def _merge_kernel(x_ref, ma_ref, mb_ref, w_ref, g_ref, b_ref, y_ref, *, alpha):
    mix = jnp.concatenate([ma_ref[h] for h in range(A_HEADS)] + [mb_ref[...]], axis=1)
    y = alpha * x_ref[...] + _dot(mix, w_ref[...])
    mu = jnp.mean(y, axis=-1, keepdims=True)
    d = y - mu
    var = jnp.mean(d * d, axis=-1, keepdims=True)
    y_ref[...] = d * lax.rsqrt(var + LN_EPS) * g_ref[...] + b_ref[...]


def _merge(x, mix_a, mix_b, w_out, ln_g, ln_b, alpha, tm):
    n, dm = x.shape
    return pl.pallas_call(
        functools.partial(_merge_kernel, alpha=alpha), grid=(n // tm,),
        in_specs=[
            pl.BlockSpec((tm, dm), lambda i: (i, 0)),
            pl.BlockSpec((A_HEADS, tm, A_DV), lambda i: (0, i, 0)),
            pl.BlockSpec((tm, B_WIDTH), lambda i: (i, 0)),
            pl.BlockSpec((A_WIDTH + B_WIDTH, dm), lambda i: (0, 0)),
            pl.BlockSpec((1, dm), lambda i: (0, 0)),
            pl.BlockSpec((1, dm), lambda i: (0, 0)),
        ],
        out_specs=pl.BlockSpec((tm, dm), lambda i: (i, 0)),
        out_shape=jax.ShapeDtypeStruct((n, dm), F32),
        compiler_params=pltpu.CompilerParams(dimension_semantics=("parallel",), vmem_limit_bytes=VMEM_LIMIT),
        name="merge",
    )(x, mix_a, mix_b, w_out, ln_g, ln_b)


def _split_weights(w_in_l):
    offs = np.cumsum([0, 512, 512, 512, 512, 512, 512, 512, 512, IDX_HEADS * IDX_DIM, IDX_DIM, IDX_HEADS])
    col = lambda i: w_in_l[:, offs[i]:offs[i + 1]]
    qa, fa, ia, ga, qb, kb, vb, gb, qi, ki, wi = (col(i) for i in range(11))
    pad = jnp.zeros((w_in_l.shape[0], LANES - IDX_DIM), w_in_l.dtype)
    wn = jnp.concatenate([qa, fa, ia, ga, kb, gb, ki, pad], axis=1).astype(BF16)
    wt = jnp.concatenate([qb, kb, vb, qi, wi, ki], axis=1).T.astype(BF16)
    return wn, wt


def _layer(xp, xs, cache_k, cache_v, cache_kidx, s0_sample, page_table, w_in_l, lb_l, norm_g_l, kn_g_l, kn_b_l,
           w_out_l, ln_g_l, ln_b_l, alpha):
    b, l, dm = xp.shape
    bd, t, _ = xs.shape
    npages, page = page_table.shape[1], cache_k.shape[2]
    assert page == Q_BLOCK and l % page == 0
    past = npages * page
    wn, wt = _split_weights(w_in_l)
    lb = lb_l.reshape(1, A_WIDTH)
    ng = norm_g_l.reshape(1, A_DV)
    w_out_b = w_out_l.astype(BF16)
    lng, lnb = ln_g_l.reshape(1, dm), ln_b_l.reshape(1, dm)

    tm = 256
    xp2 = xp.reshape(b * l, dm)
    (hq, hk, hg, hv, hgate, kbf, gbs, kibf, qT, vT, qiT2, wT, k_p, v_p, ki_p) = _project(
        xp2, jnp.arange(l, dtype=I32), wn, wt, lb, kn_g_l, kn_b_l, tm)
    mix_a, s_p = _hgrn_prompt(hq, hk, hg, hv, hgate, ng, b, l, min(l, 512))
    mix_b = _dsa_prompt(kibf, qiT2, wT, kbf, qT, vT, gbs, b, l)
    y_p = _merge(xp2, mix_a, mix_b, w_out_b, lng, lnb, alpha, 512).reshape(b, l, dm)

    ns = bd * t
    xs2 = xs.reshape(ns, dm)
    pos_s = past + (jnp.arange(ns, dtype=I32) % t)
    (hq, hk, hg, hv, hgate, kbf, gbs, kibf, qT, vT, qiT2, wT, k_s, v_s, ki_s) = _project(
        xs2, pos_s, wn, wt, lb, kn_g_l, kn_b_l, ns)
    mix_a, s_s = _hgrn_sample(hq, hk, hg, hv, hgate, ng, s0_sample, t)
    qi_nat = qiT2.reshape(ns // Q_BLOCK, IDX_DIM, IDX_HEADS, Q_BLOCK).transpose(0, 3, 2, 1)
    qi_rows = qi_nat.reshape(bd, t, IDX_HEADS, IDX_DIM).transpose(0, 2, 1, 3).reshape(bd, IDX_HEADS * t, IDX_DIM)
    w_rows = wT.T.reshape(bd, t, IDX_HEADS).transpose(0, 2, 1).reshape(bd, IDX_HEADS * t, 1)
    wrep = jnp.broadcast_to(w_rows, (bd, IDX_HEADS * t, page))
    q_nat = qT.T.reshape(bd, 1, t, B_HEADS, B_DH)
    eye = jnp.eye(B_HEADS, dtype=BF16).reshape(1, B_HEADS, 1, B_HEADS, 1)
    qbd = (q_nat * eye).reshape(bd, B_HEADS * t, B_WIDTH)

    def per_seq(pages):
        feat = pages.shape[1]
        a = pages.transpose(1, 0, 2).reshape(feat, bd, t).transpose(1, 0, 2).astype(BF16)
        return jnp.concatenate([a, jnp.zeros((bd, feat, page - t), BF16)], axis=2)

    token_minor = lambda c: jnp.moveaxis(c, 2, -1).reshape(c.shape[0], c.shape[1], -1, page)
    mix_b = _dsa_sample(page_table, qi_rows, wrep, qbd, per_seq(ki_s), per_seq(k_s), per_seq(v_s),
                        gbs.reshape(bd, t, B_WIDTH), token_minor(cache_kidx), token_minor(cache_k),
                        token_minor(cache_v), t).reshape(ns, B_WIDTH)
    y_s = _merge(xs2, mix_a, mix_b, w_out_b, lng, lnb, alpha, ns).reshape(bd, t, dm)

    return (y_p, y_s, k_p, v_p, ki_p, s_p, k_s, v_s, ki_s, s_s)


def kernel(x_prompt, x_sample, cache_k, cache_v, cache_kidx, state_hgrn, page_table, w_in, hgrn_lb_logits,
           hgrn_norm_g, idx_norm_g, idx_norm_b, w_out, ln_g, ln_b):
    depth = w_in.shape[0]
    assert depth == 1, "one layer per step"
    b, l, _ = x_prompt.shape
    bd, t, _ = x_sample.shape
    page = cache_k.shape[2]
    alpha = (2.0 * depth) ** 0.25
    lbs = jnp.cumsum(jax.nn.softmax(hgrn_lb_logits.astype(F32), axis=0), axis=0)[:depth]
    (y_p, y_s, k_p, v_p, ki_p, s_p, k_s, v_s, ki_s, s_s) = _layer(
        x_prompt, x_sample, cache_k, cache_v, cache_kidx, state_hgrn[0], page_table, w_in[0], lbs[0],
        hgrn_norm_g[0], idx_norm_g[0], idx_norm_b[0], w_out[0], ln_g[0], ln_b[0], alpha)
    nat_p = lambda pg, *f: jnp.moveaxis(pg.reshape(b, l // page, 1, *f, page), -1, 3)
    nat_s = lambda pg, *f: pg.transpose(0, 2, 1).reshape(bd, 1, t, *f)
    return (
        y_p, y_s,
        nat_p(k_p, B_HEADS, B_DH), nat_p(v_p, B_HEADS, B_DH), nat_p(ki_p, IDX_DIM),
        s_p[None],
        nat_s(k_s, B_HEADS, B_DH), nat_s(v_s, B_HEADS, B_DH), nat_s(ki_s, IDX_DIM),
        s_s[None],
    )
```

```python
import functools
import math

import jax
import jax.numpy as jnp
import numpy as np
from jax import lax
from jax.experimental import pallas as pl
from jax.experimental.pallas import tpu as pltpu

F32 = jnp.float32
BF16 = jnp.bfloat16
I32 = jnp.int32

A_HEADS = 4
A_DK = 128
A_DV = 128
B_HEADS = 8
B_DH = 64
IDX_HEADS = 16
IDX_DIM = 64
TOPK_MAX = 256
ROPE_THETA = 500000.0
ROT = 16
ROT_HALF = ROT // 2
LN_EPS = 1e-5
Q_BLOCK = 128
K_TILE = 256
ATT_TILE = 512
HGRN_CHUNK = 128
HGRN_SUB = 16
LANES = 128
VMEM_LIMIT = 56 * 1024 * 1024
SAMPLE_DMA_DEPTH = 32
SAMPLE_PAGE_GROUP = 8

A_WIDTH = A_HEADS * A_DV
B_WIDTH = B_HEADS * B_DH
NAT_WIDTH = 6 * 512 + LANES
TR_ROWS = 3 * 512 + IDX_HEADS * IDX_DIM + IDX_HEADS + IDX_DIM
INT_MIN = -(2 ** 31)
KEY_NEG_INF = -0x7F800000


def _dot(a, b):
    return jnp.dot(a, b, preferred_element_type=F32)


def _dot_nt(a, b):
    return lax.dot_general(a, b, (((1,), (1,)), ((), ())), preferred_element_type=F32)


def _dot_tn(a, b):
    return lax.dot_general(a, b, (((0,), (0,)), ((), ())), preferred_element_type=F32)


def _silu(x):
    return x * jax.nn.sigmoid(x)


def _key_to_float(key):
    m = key >> 31
    mag = (key ^ m) - m
    return pltpu.bitcast(mag | (m & INT_MIN), F32)


def _proj_kernel(x_ref, wn_ref, wt_ref, lb_ref, kng_ref, knb_ref, kngc_ref, knbc_ref, cn_ref, sa_ref, sb_ref,
                 ct_ref, st_ref,
                 hq_ref, hk_ref, hg_ref, hv_ref, hgate_ref, kbf_ref, gb_ref, kibf_ref,
                 qT_ref, vT_ref, qiT_ref, wT_ref, kTp_ref, vTp_ref, kiTp_ref):
    tm = x_ref.shape[0]
    xb = x_ref[...].astype(BF16)

    def nat(col, width=512):
        return _dot(xb, wn_ref[:, col:col + width])

    lb = lb_ref[...]
    qa = nat(0)
    hq = _silu(qa)
    fa = nat(512)
    hg = jnp.log2(lb + (1.0 - lb) * jax.nn.sigmoid(fa))
    hk = (1.0 - lb) * jax.nn.sigmoid(-fa)
    ia = nat(1024)
    hgate = _silu(nat(1536))
    for h in range(A_HEADS):
        sl = slice(h * A_DK, (h + 1) * A_DK)
        hq_ref[h] = hq[:, sl].astype(BF16)
        hk_ref[h] = hk[:, sl]
        hg_ref[h] = hg[:, sl]
        hv_ref[h] = ia[:, sl].astype(BF16)
        hgate_ref[h] = hgate[:, sl].astype(BF16)

    cn, sa, sb = cn_ref[...], sa_ref[...], sb_ref[...]

    def rope_nat(xc):
        return xc * cn + pltpu.roll(xc, LANES - ROT_HALF, 1) * sa + pltpu.roll(xc, ROT_HALF, 1) * sb

    kb = nat(2048)
    kbf_ref[...] = jnp.concatenate(
        [rope_nat(kb[:, c * LANES:(c + 1) * LANES]) for c in range(B_WIDTH // LANES)], axis=1).astype(BF16)
    gb_ref[...] = _silu(nat(2560)).astype(BF16)

    kic = nat(3072, LANES)
    lane = lax.broadcasted_iota(I32, (tm, LANES), 1)
    inb = lane < IDX_DIM
    mu = jnp.sum(jnp.where(inb, kic, 0.0), axis=-1, keepdims=True) * (1.0 / IDX_DIM)
    d = jnp.where(inb, kic - mu, 0.0)
    var = jnp.sum(d * d, axis=-1, keepdims=True) * (1.0 / IDX_DIM)
    kin = d * lax.rsqrt(var + LN_EPS) * kng_ref[...] + knb_ref[...]
    kibf_ref[...] = rope_nat(kin)[:, :IDX_DIM].astype(BF16)

    ct, st = ct_ref[...], st_ref[...]
    npage = tm // Q_BLOCK

    def tr(row, height):
        return _dot_nt(wt_ref[row:row + height, :], xb)

    def rope_tr(blk):
        x1, x2 = blk[0:ROT_HALF], blk[ROT_HALF:ROT]
        return jnp.concatenate([x1 * ct - x2 * st, x1 * st + x2 * ct, blk[ROT:]], axis=0)

    qbT = tr(0, B_WIDTH)
    for h in range(B_HEADS):
        blk = rope_tr(qbT[h * B_DH:(h + 1) * B_DH])
        qT_ref[h * B_DH:(h + 1) * B_DH, :] = (blk * (B_DH ** -0.5)).astype(BF16)
    kbT = tr(B_WIDTH, B_WIDTH)
    for h in range(B_HEADS):
        blk = rope_tr(kbT[h * B_DH:(h + 1) * B_DH])
        for pg in range(npage):
            kTp_ref[pg, h * B_DH:(h + 1) * B_DH, :] = blk[:, pg * Q_BLOCK:(pg + 1) * Q_BLOCK]
    vbT = tr(2 * B_WIDTH, B_WIDTH)
    vT_ref[...] = vbT.astype(BF16)
    for pg in range(npage):
        vTp_ref[pg] = vbT[:, pg * Q_BLOCK:(pg + 1) * Q_BLOCK]
    row = 3 * B_WIDTH
    qiT = tr(row, IDX_HEADS * IDX_DIM)
    for h in range(IDX_HEADS):
        blk = rope_tr(qiT[h * IDX_DIM:(h + 1) * IDX_DIM]).astype(BF16)
        for pg in range(npage):
            qiT_ref[pg, :, h * Q_BLOCK:(h + 1) * Q_BLOCK] = blk[:, pg * Q_BLOCK:(pg + 1) * Q_BLOCK]
    row += IDX_HEADS * IDX_DIM
    wT_ref[...] = tr(row, IDX_HEADS) * (IDX_HEADS ** -0.5 * IDX_DIM ** -0.5)
    row += IDX_HEADS
    kiT = tr(row, IDX_DIM)
    muT = jnp.mean(kiT, axis=0, keepdims=True)
    dT = kiT - muT
    varT = jnp.mean(dT * dT, axis=0, keepdims=True)
    kiT = rope_tr(dT * lax.rsqrt(varT + LN_EPS) * kngc_ref[...] + knbc_ref[...])
    for pg in range(npage):
        kiTp_ref[pg] = kiT[:, pg * Q_BLOCK:(pg + 1) * Q_BLOCK]


def _rope_tables(pos):
    inv = ROPE_THETA ** (-jnp.arange(ROT_HALF, dtype=F32) / ROT_HALF)
    ang = pos.astype(F32)[:, None] * inv[None, :]
    cos, sin = jnp.cos(ang), jnp.sin(ang)
    p = pos.shape[0]
    one = jnp.ones((p, B_DH - ROT), F32)
    zero8 = jnp.zeros((p, ROT_HALF), F32)
    zero = jnp.zeros((p, B_DH - ROT), F32)
    cn = jnp.concatenate([cos, cos, one], axis=1)
    sa = jnp.concatenate([-sin, zero8, zero], axis=1)
    sb = jnp.concatenate([zero8, sin, zero], axis=1)
    tile2 = lambda a: jnp.concatenate([a, a], axis=1)
    return tile2(cn), tile2(sa), tile2(sb), cos.T, sin.T


def _project(x, pos, wn, wt, lb, kn_g, kn_b, tm):
    n, dm = x.shape
    p = pos.shape[0]
    nper = p // tm
    cn, sa, sb, ct, st = _rope_tables(pos)
    padl = lambda a: jnp.concatenate([a, jnp.zeros((LANES - IDX_DIM,), a.dtype)]).reshape(1, LANES)
    grid = (n // tm,)
    row = lambda i: (i, 0)
    full = lambda i: (0, 0)
    head = lambda i: (0, i, 0)
    page = lambda i: (i, 0, 0)
    per = lambda i: (i % nper, 0)
    perT = lambda i: (0, i % nper)
    colT = lambda i: (0, i)
    in_specs = [
        pl.BlockSpec((tm, dm), row),
        pl.BlockSpec((dm, NAT_WIDTH), full, pipeline_mode=pl.Buffered(1)),
        pl.BlockSpec((TR_ROWS, dm), full, pipeline_mode=pl.Buffered(1)),
        pl.BlockSpec((1, A_WIDTH), full),
        pl.BlockSpec((1, LANES), full),
        pl.BlockSpec((1, LANES), full),
        pl.BlockSpec((IDX_DIM, 1), full),
        pl.BlockSpec((IDX_DIM, 1), full),
        pl.BlockSpec((tm, LANES), per),
        pl.BlockSpec((tm, LANES), per),
        pl.BlockSpec((tm, LANES), per),
        pl.BlockSpec((ROT_HALF, tm), perT),
        pl.BlockSpec((ROT_HALF, tm), perT),
    ]
    npage = tm // Q_BLOCK
    hshape = lambda dt: jax.ShapeDtypeStruct((A_HEADS, n, A_DK), dt)
    out_shape = [
        hshape(BF16), hshape(F32), hshape(F32), hshape(BF16), hshape(BF16),
        jax.ShapeDtypeStruct((n, B_WIDTH), BF16), jax.ShapeDtypeStruct((n, B_WIDTH), BF16),
        jax.ShapeDtypeStruct((n, IDX_DIM), BF16),
        jax.ShapeDtypeStruct((B_WIDTH, n), BF16), jax.ShapeDtypeStruct((B_WIDTH, n), BF16),
        jax.ShapeDtypeStruct((n // Q_BLOCK, IDX_DIM, IDX_HEADS * Q_BLOCK), BF16),
        jax.ShapeDtypeStruct((IDX_HEADS, n), F32),
        jax.ShapeDtypeStruct((n // Q_BLOCK, B_WIDTH, Q_BLOCK), F32),
        jax.ShapeDtypeStruct((n // Q_BLOCK, B_WIDTH, Q_BLOCK), F32),
        jax.ShapeDtypeStruct((n // Q_BLOCK, IDX_DIM, Q_BLOCK), F32),
    ]
    hspec = pl.BlockSpec((A_HEADS, tm, A_DK), head)
    out_specs = [
        hspec, hspec, hspec, hspec, hspec,
        pl.BlockSpec((tm, B_WIDTH), row), pl.BlockSpec((tm, B_WIDTH), row),
        pl.BlockSpec((tm, IDX_DIM), row),
        pl.BlockSpec((B_WIDTH, tm), colT), pl.BlockSpec((B_WIDTH, tm), colT),
        pl.BlockSpec((npage, IDX_DIM, IDX_HEADS * Q_BLOCK), page),
        pl.BlockSpec((IDX_HEADS, tm), colT),
        pl.BlockSpec((npage, B_WIDTH, Q_BLOCK), page),
        pl.BlockSpec((npage, B_WIDTH, Q_BLOCK), page),
        pl.BlockSpec((npage, IDX_DIM, Q_BLOCK), page),
    ]
    return pl.pallas_call(
        _proj_kernel, grid=grid, in_specs=in_specs, out_specs=out_specs, out_shape=out_shape,
        compiler_params=pltpu.CompilerParams(dimension_semantics=("parallel",), vmem_limit_bytes=VMEM_LIMIT),
        name="proj",
    )(x, wn, wt, lb, padl(kn_g), padl(kn_b), kn_g.reshape(IDX_DIM, 1), kn_b.reshape(IDX_DIM, 1),
      cn, sa, sb, ct, st)


def _hgrn_cumsum(g, c):
    if c == LANES:
        ri = lax.broadcasted_iota(I32, (c, c), 0)
        ci = lax.broadcasted_iota(I32, (c, c), 1)
        tri = jnp.where(ci <= ri, 1.0, 0.0).astype(BF16)
        g1 = g.astype(BF16)
        e1 = g - g1.astype(F32)
        g2 = e1.astype(BF16)
        g3 = (e1 - g2.astype(F32)).astype(BF16)
        return _dot(tri, g1) + _dot(tri, g2) + _dot(tri, g3)
    ri = lax.broadcasted_iota(I32, (c, A_DK), 0)
    cum = jnp.zeros((c, A_DK), F32)
    for s in range(c):
        cum = cum + jnp.where(ri >= s, g[s:s + 1], 0.0)
    return cum


def _hgrn_chunks(chains, ng, c):
    r = HGRN_SUB
    nsub = c // r
    rowi = lax.broadcasted_iota(I32, (r, A_DK), 0)
    lane = lax.broadcasted_iota(I32, (r, LANES), 1)
    cums = [_hgrn_cumsum(g, c) for _, _, g, _, _, _ in chains]

    light = []
    for (q, k, _, v, _, st), cum in zip(chains, cums):
        o_state = _dot_nt((q * jnp.exp2(cum)).astype(BF16), st.astype(BF16))
        off = [None]
        for i in range(1, nsub):
            ref = cum[i * r - 1:i * r]
            qt = (q[i * r:(i + 1) * r] * jnp.exp2(cum[i * r:(i + 1) * r] - ref)).astype(BF16)
            kt = (k[:i * r] * jnp.exp2(ref - cum[:i * r])).astype(BF16)
            kt = jnp.concatenate([kt, jnp.zeros((LANES - i * r, A_DK), BF16)], axis=0)
            off.append(_dot_nt(qt, kt))
        vb = v.astype(BF16)
        last = cum[c - 1:c]
        kh = (k * jnp.exp2(last - cum)).astype(BF16)
        if c < LANES:
            zpad = jnp.zeros((LANES - c, A_DK), BF16)
            vb = jnp.concatenate([vb, zpad], axis=0)
            kh = jnp.concatenate([kh, zpad], axis=0)
        st_new = st * jnp.exp2(last) + _dot_tn(vb, kh)
        light.append((o_state, off, vb, st_new))

    rsums = []
    for (q, k, _, _, _, _), cum in zip(chains, cums):
        prods = []
        for i in range(nsub):
            qs, cs = q[i * r:(i + 1) * r], cum[i * r:(i + 1) * r]
            for s in range(r):
                row = i * r + s
                p = (qs * k[row:row + 1]) * jnp.exp2(cs - cum[row:row + 1])
                prods.append(jnp.where(rowi >= s, p, 0.0).astype(BF16))
        if nsub % 2 == 0:
            half = len(prods) // 2
            both = jnp.concatenate([jnp.concatenate(prods[:half], axis=0), jnp.concatenate(prods[half:], axis=0)],
                                   axis=1)
            wr = lax.broadcasted_iota(I32, (2 * A_DK, 2 * LANES), 0) < A_DK
            wc = lax.broadcasted_iota(I32, (2 * A_DK, 2 * LANES), 1) < LANES
            sums = _dot(both, jnp.where(wr == wc, 1.0, 0.0).astype(BF16))
            rsums.append(lambda row, sums=sums, half=half: (
                sums[row * r:(row + 1) * r, :LANES] if row < half
                else sums[(row - half) * r:(row - half + 1) * r, LANES:]))
        else:
            sums = _dot(jnp.concatenate(prods, axis=0), jnp.ones((A_DK, LANES), BF16))
            rsums.append(lambda row, sums=sums: sums[row * r:(row + 1) * r])

    results = []
    for (_, _, _, _, gate, _), (o_state, off, vb, st_new), rsum in zip(chains, light, rsums):
        sc_rows = []
        for i in range(nsub):
            sci = jnp.zeros((r, LANES), F32)
            for s in range(r):
                row = i * r + s
                sci = jnp.where(lane == row, rsum(row), sci)
            sc_rows.append(sci if i == 0 else sci + off[i])
        sc = jnp.concatenate(sc_rows, axis=0).astype(BF16)
        o = o_state + _dot(sc, vb)
        ms = jnp.mean(o * o, axis=-1, keepdims=True)
        results.append((o * lax.rsqrt(ms + LN_EPS) * ng * gate, st_new))
    return results


def _hgrn_prompt_kernel(q_ref, k_ref, g_ref, v_ref, gate_ref, ng_ref, o_ref, sf_ref, st_scr):
    c = HGRN_CHUNK
    nchunk = q_ref.shape[1] // c
    tb = pl.program_id(1)

    @pl.when(tb == 0)
    def _():
        st_scr[...] = jnp.zeros_like(st_scr)

    ng = ng_ref[...]

    def body(j, carry):
        sl = pl.ds(pl.multiple_of(j * c, c), c)
        loaded = [(q_ref[h, sl, :].astype(F32), k_ref[h, sl, :], g_ref[h, sl, :], v_ref[h, sl, :].astype(F32),
                   gate_ref[h, sl, :].astype(F32), st_scr[h]) for h in range(A_HEADS)]
        for h, (o, st_new) in enumerate(_hgrn_chunks(loaded, ng, c)):
            st_scr[h] = st_new
            o_ref[h, sl, :] = o.astype(BF16)
        return carry

    lax.fori_loop(0, nchunk, body, 0)

    @pl.when(tb == pl.num_programs(1) - 1)
    def _():
        for h in range(A_HEADS):
            sf_ref[0, h] = st_scr[h].T


def _hgrn_prompt(hq, hk, hg, hv, hgate, ng, batch, seq, tb):
    n = batch * seq
    nt = seq // tb
    blk = pl.BlockSpec((A_HEADS, tb, A_DK), lambda b, t: (0, b * nt + t, 0))
    return pl.pallas_call(
        _hgrn_prompt_kernel, grid=(batch, nt),
        in_specs=[blk, blk, blk, blk, blk, pl.BlockSpec((1, A_DV), lambda b, t: (0, 0))],
        out_specs=[blk, pl.BlockSpec((1, A_HEADS, A_DK, A_DV), lambda b, t: (b, 0, 0, 0))],
        out_shape=[jax.ShapeDtypeStruct((A_HEADS, n, A_DV), BF16),
                   jax.ShapeDtypeStruct((batch, A_HEADS, A_DK, A_DV), F32)],
        scratch_shapes=[pltpu.VMEM((A_HEADS, A_DV, A_DK), F32)],
        compiler_params=pltpu.CompilerParams(dimension_semantics=("parallel", "arbitrary"),
                                             vmem_limit_bytes=VMEM_LIMIT),
        name="hgrn_prompt",
    )(hq, hk, hg, hv, hgate, ng)


def _hgrn_sample_kernel(q_ref, k_ref, g_ref, v_ref, gate_ref, ng_ref, s0_ref, o_ref, sf_ref, *, t, nb):
    c = HGRN_SUB
    ng = ng_ref[...]
    zpad = jnp.zeros((c - t, A_DK), F32)
    pad = lambda a: jnp.concatenate([a, zpad], axis=0)
    chains = []
    for h in range(A_HEADS):
        q, k, g = q_ref[h].astype(F32), k_ref[h], g_ref[h]
        v, gate = v_ref[h].astype(F32), gate_ref[h].astype(F32)
        for b in range(nb):
            sl = slice(b * t, (b + 1) * t)
            chains.append((pad(q[sl]), pad(k[sl]), pad(g[sl]), pad(v[sl]), pad(gate[sl]), s0_ref[b, h].T))
    results = _hgrn_chunks(chains, ng, c)
    for h in range(A_HEADS):
        for b in range(nb):
            sf_ref[b, h] = results[h * nb + b][1].T
        o_ref[h] = jnp.concatenate([results[h * nb + b][0][:t] for b in range(nb)], axis=0).astype(BF16)


def _hgrn_sample(hq, hk, hg, hv, hgate, ng, s0, t):
    bd = s0.shape[0]
    nb = 16 // t
    blk = pl.BlockSpec((A_HEADS, nb * t, A_DK), lambda i: (0, i, 0))
    sblk = pl.BlockSpec((nb, A_HEADS, A_DK, A_DV), lambda i: (i, 0, 0, 0))
    return pl.pallas_call(
        functools.partial(_hgrn_sample_kernel, t=t, nb=nb), grid=(bd // nb,),
        in_specs=[blk, blk, blk, blk, blk, pl.BlockSpec((1, A_DV), lambda i: (0, 0)), sblk],
        out_specs=[blk, sblk],
        out_shape=[jax.ShapeDtypeStruct((A_HEADS, bd * t, A_DV), BF16),
                   jax.ShapeDtypeStruct((bd, A_HEADS, A_DK, A_DV), F32)],
        compiler_params=pltpu.CompilerParams(dimension_semantics=("parallel",), vmem_limit_bytes=VMEM_LIMIT),
        name="hgrn_sample",
    )(hq, hk, hg, hv, hgate, ng, s0)


def _count_tiles(score_ref, ntiles, pred):
    def body(j, acc):
        ks = pl.multiple_of(j * ATT_TILE, ATT_TILE)
        hit = jnp.where(pred(score_ref[pl.ds(ks, ATT_TILE), :], ks), 1, 0)
        return acc + jnp.sum(hit.reshape(ATT_TILE // 8, 8, Q_BLOCK), axis=0)
    acc = lax.fori_loop(0, ntiles, body, jnp.zeros((8, Q_BLOCK), I32))
    return jnp.sum(acc, axis=0, keepdims=True)


def _kth_largest(count_ge, topk, shape, two_bits=False):
    c0 = count_ge(jnp.zeros(shape, F32))
    ok = c0 >= topk
    thr = jnp.where(ok, 0, INT_MIN)
    nge = jnp.where(ok, c0, 0)

    def place(bit, carry):
        thr, nge = carry
        cand = thr | bit
        cnt = count_ge(_key_to_float(cand))
        ok = cnt >= topk
        return jnp.where(ok, cand, thr), jnp.where(ok, cnt, nge)

    if not two_bits:
        return lax.fori_loop(0, 31, lambda i, c: place(jnp.left_shift(jnp.int32(1), 30 - i), c), (thr, nge))

    def place_two(i, carry):
        thr, nge = carry
        hi = jnp.left_shift(jnp.int32(1), 30 - 2 * i)
        lo = jnp.left_shift(jnp.int32(1), 29 - 2 * i)
        n_hi, n_lo, n_both = (count_ge(_key_to_float(thr | bits)) for bits in (hi, lo, hi | lo))
        ok_hi = n_hi >= topk
        thr1, nge1 = jnp.where(ok_hi, thr | hi, thr), jnp.where(ok_hi, n_hi, nge)
        n2 = jnp.where(ok_hi, n_both, n_lo)
        ok2 = n2 >= topk
        return jnp.where(ok2, thr1 | lo, thr1), jnp.where(ok2, n2, nge1)

    return place(jnp.int32(1), lax.fori_loop(0, 15, place_two, (thr, nge)))


def _dsa_prompt_kernel(ki_ref, qiT_ref, wT_ref, k_ref, qT_ref, vT_ref, gb_ref, o_ref,
                       score_scr, bias_scr, lg_scr, oT_scr, *, topk):
    i = pl.program_id(1)
    ntiles = (i * Q_BLOCK + Q_BLOCK + K_TILE - 1) // K_TILE
    qpos = i * Q_BLOCK + lax.broadcasted_iota(I32, (K_TILE, Q_BLOCK), 1)
    krow = lax.broadcasted_iota(I32, (K_TILE, Q_BLOCK), 0)

    qi = qiT_ref[0]
    wT = wT_ref[...]

    def score_body(j, carry):
        ks = pl.multiple_of(j * K_TILE, K_TILE)
        x = _dot(ki_ref[pl.ds(ks, K_TILE), :], qi)
        sc = jnp.zeros((K_TILE, Q_BLOCK), F32)
        for h in range(IDX_HEADS):
            sc = sc + jnp.maximum(x[:, h * Q_BLOCK:(h + 1) * Q_BLOCK], 0.0) * wT[h:h + 1]
        score_scr[pl.ds(ks, K_TILE), :] = jnp.where(krow + ks <= qpos, sc, -jnp.inf)
        return carry

    lax.fori_loop(0, ntiles, score_body, 0)

    natt = (i * Q_BLOCK + Q_BLOCK + ATT_TILE - 1) // ATT_TILE

    @pl.when(natt * (ATT_TILE // K_TILE) > ntiles)
    def _():
        score_scr[pl.ds(pl.multiple_of(ntiles * K_TILE, K_TILE), K_TILE), :] = jnp.full((K_TILE, Q_BLOCK), -jnp.inf, F32)

    count = functools.partial(_count_tiles, score_scr, natt)
    arow = lax.broadcasted_iota(I32, (ATT_TILE, Q_BLOCK), 0)
    thr, nge = _kth_largest(lambda x: count(lambda tile, ks: tile >= x), topk, (1, Q_BLOCK))
    live = thr > KEY_NEG_INF
    thr = _key_to_float(jnp.maximum(thr, KEY_NEG_INF + 1))
    no_cut = jnp.full((1, Q_BLOCK), natt * ATT_TILE, I32)
    has_ties = jnp.max(jnp.where(live & (nge > topk), 1, 0)) > 0

    def tie_cut():
        need = topk - count(lambda tile, ks: tile > thr)
        nbits = max(1, int(math.ceil(math.log2(score_scr.shape[0] + 1))))

        def body(b, pos):
            cand = pos | jnp.left_shift(jnp.int32(1), nbits - 1 - b)
            below = count(lambda tile, ks: (tile == thr) & (arow + ks < cand))
            return jnp.where(below <= need, cand, pos)

        return lax.fori_loop(0, nbits, body, jnp.zeros((1, Q_BLOCK), I32))

    cut = lax.cond(has_ties, tie_cut, lambda: no_cut)

    def bias_body(j, carry):
        ks = pl.multiple_of(j * K_TILE, K_TILE)
        tile = score_scr[pl.ds(ks, K_TILE), :]
        sel = (tile > thr) | ((tile == thr) & (krow + ks < cut))
        bias_scr[pl.ds(ks, K_TILE), :] = jnp.where(sel, 0.0, -jnp.inf)
        return carry

    lax.fori_loop(0, natt * (ATT_TILE // K_TILE), bias_body, 0)

    npair = B_HEADS // 2
    z = jnp.zeros((B_DH, Q_BLOCK), BF16)
    rhs = []
    for p in range(npair):
        qp = qT_ref[p * LANES:(p + 1) * LANES, :]
        rhs.append(jnp.concatenate([jnp.concatenate([qp[:B_DH], z], axis=0),
                                    jnp.concatenate([z, qp[B_DH:]], axis=0)], axis=1))
    fold = lambda a: a.reshape(ATT_TILE // 8, 8, a.shape[1])

    def logit_body(j, mx):
        ks = pl.multiple_of(j * ATT_TILE, ATT_TILE)
        bias = bias_scr[pl.ds(ks, ATT_TILE), :]
        bias2 = jnp.concatenate([bias, bias], axis=1)
        lgs = [_dot(k_ref[pl.ds(ks, ATT_TILE), p * LANES:(p + 1) * LANES], rhs[p]) + bias2 for p in range(npair)]
        lg_scr[pl.ds(ks, ATT_TILE), :] = jnp.concatenate(lgs, axis=1)
        return tuple(jnp.maximum(m, jnp.max(fold(lg), axis=0)) for m, lg in zip(mx, lgs))

    mx = lax.fori_loop(0, natt, logit_body,
                       tuple(jnp.full((8, 2 * Q_BLOCK), -jnp.inf, F32) for _ in range(npair)))
    mx = jnp.concatenate([jnp.max(m, axis=0, keepdims=True) for m in mx], axis=1)
    oT_scr[...] = jnp.zeros_like(oT_scr)

    def value_body(j, den):
        ks = pl.multiple_of(j * ATT_TILE, ATT_TILE)
        new_den, new_acc = [], []
        for h in range(B_HEADS):
            cols = slice(h * Q_BLOCK, (h + 1) * Q_BLOCK)
            rows = slice(h * B_DH, (h + 1) * B_DH)
            e = jnp.exp(lg_scr[pl.ds(ks, ATT_TILE), cols] - mx[:, cols])
            new_den.append(den[h] + jnp.sum(fold(e), axis=0))
            new_acc.append(oT_scr[rows, :] + _dot(vT_ref[rows, pl.ds(ks, ATT_TILE)], e.astype(BF16)))
        for h in range(B_HEADS):
            oT_scr[h * B_DH:(h + 1) * B_DH, :] = new_acc[h]
        return tuple(new_den)

    den = lax.fori_loop(0, natt, value_body, tuple(jnp.zeros((8, Q_BLOCK), F32) for _ in range(B_HEADS)))
    for h in range(B_HEADS):
        rows = slice(h * B_DH, (h + 1) * B_DH)
        oT_scr[rows, :] = oT_scr[rows, :] / jnp.sum(den[h], axis=0, keepdims=True)

    o_ref[...] = (oT_scr[...].T * gb_ref[...].astype(F32)).astype(BF16)


def _dsa_prompt(kibf, qiT2, wT, kbf, qT, vT, gbs, batch, seq):
    n = batch * seq
    nq = seq // Q_BLOCK
    topk = min(TOPK_MAX, seq // 4)
    assert seq % ATT_TILE == 0 and topk <= K_TILE
    return pl.pallas_call(
        functools.partial(_dsa_prompt_kernel, topk=topk), grid=(batch, nq),
        in_specs=[
            pl.BlockSpec((seq, IDX_DIM), lambda b, i: (b, 0)),
            pl.BlockSpec((1, IDX_DIM, IDX_HEADS * Q_BLOCK), lambda b, i: (b * nq + i, 0, 0)),
            pl.BlockSpec((IDX_HEADS, Q_BLOCK), lambda b, i: (0, b * nq + i)),
            pl.BlockSpec((seq, B_WIDTH), lambda b, i: (b, 0)),
            pl.BlockSpec((B_WIDTH, Q_BLOCK), lambda b, i: (0, b * nq + i)),
            pl.BlockSpec((B_WIDTH, seq), lambda b, i: (0, b)),
            pl.BlockSpec((Q_BLOCK, B_WIDTH), lambda b, i: (b * nq + i, 0)),
        ],
        out_specs=pl.BlockSpec((Q_BLOCK, B_WIDTH), lambda b, i: (b * nq + i, 0)),
        out_shape=jax.ShapeDtypeStruct((n, B_WIDTH), BF16),
        scratch_shapes=[
            pltpu.VMEM((seq, Q_BLOCK), F32),
            pltpu.VMEM((seq, Q_BLOCK), F32),
            pltpu.VMEM((seq, B_HEADS * Q_BLOCK), F32),
            pltpu.VMEM((B_WIDTH, Q_BLOCK), F32),
        ],
        compiler_params=pltpu.CompilerParams(dimension_semantics=("parallel", "arbitrary"),
                                             vmem_limit_bytes=VMEM_LIMIT),
        name="dsa_prompt",
    )(kibf, qiT2, wT, kbf, qT, vT, gbs)


def _dsa_sample_kernel(pt_ref, qi_ref, wrep_ref, qbd_ref, kin_ref, kn_ref, vn_ref, gb_ref,
                       cki_hbm, ck_hbm, cv_hbm, o_ref, ki_buf, k_buf, v_buf, score_scr, sem_i, sem_k, sem_v,
                       *, t, npages, topk, depth, group):
    b = pl.program_id(0)
    nb = pl.num_programs(0)
    page = ki_buf.shape[3]
    total = (npages + 1) * page
    lane = lax.broadcasted_iota(I32, (t, page), 1)
    qrow = lax.broadcasted_iota(I32, (t, page), 0)

    def slot(p):
        return p % depth if isinstance(p, int) else lax.rem(p, depth)

    def ki_copy(seq, p):
        half = lax.rem(seq, 2)
        return pltpu.make_async_copy(cki_hbm.at[pt_ref[seq, p], 0], ki_buf.at[half, p], sem_i.at[half * npages + p])

    def k_copy(seq, p):
        return pltpu.make_async_copy(ck_hbm.at[pt_ref[seq, p], 0], k_buf.at[slot(p)], sem_k.at[slot(p)])

    def v_copy(seq, p):
        return pltpu.make_async_copy(cv_hbm.at[pt_ref[seq, p], 0], v_buf.at[slot(p)], sem_v.at[slot(p)])

    def start_ki(seq):
        def body(p, carry):
            ki_copy(seq, p).start()
            return carry
        lax.fori_loop(0, npages, body, 0)

    @pl.when(b == 0)
    def _():
        start_ki(b)
        for s in range(depth):
            k_copy(b, s).start()
            v_copy(b, s).start()

    @pl.when(b + 1 < nb)
    def _():
        start_ki(b + 1)

    def scores(kidx_t):
        x = _dot(qi_ref[0], kidx_t)
        xw = jnp.maximum(x, 0.0) * wrep_ref[0]
        sc = xw[0:t]
        for h in range(1, IDX_HEADS):
            sc = sc + xw[h * t:(h + 1) * t]
        return sc

    def score_pages(i, carry):
        for g in range(group):
            ki_copy(b, i * group + g).wait()
        sc = [scores(ki_buf[lax.rem(b, 2), i * group + g].astype(BF16)) for g in range(group)]
        score_scr[:, pl.ds(pl.multiple_of(i * group * page, group * page), group * page)] = jnp.concatenate(sc, axis=1)
        return carry

    lax.fori_loop(0, npages // group, score_pages, 0)

    score_scr[:, npages * page:] = jnp.where(lane <= qrow, scores(kin_ref[0]), -jnp.inf)

    def count(pred):
        hit = jnp.where(pred(score_scr[...]), 1, 0)
        part = hit[:, 0:page]
        for c in range(1, npages + 1):
            part = part + hit[:, c * page:(c + 1) * page]
        return jnp.sum(part, axis=1, keepdims=True)

    thr, nge = _kth_largest(lambda x: count(lambda s: s >= x), topk, (t, 1), two_bits=True)
    live = thr > KEY_NEG_INF
    thr = _key_to_float(jnp.maximum(thr, KEY_NEG_INF + 1))
    has_ties = jnp.max(jnp.where(live & (nge > topk), 1, 0)) > 0
    pos_all = lax.broadcasted_iota(I32, (t, total), 1)

    def tie_cut():
        need = topk - count(lambda s: s > thr)
        nbits = max(1, int(math.ceil(math.log2(total + 1))))

        def pos_step(j, pos):
            cand = pos | jnp.left_shift(jnp.int32(1), nbits - 1 - j)
            below = count(lambda s: (s == thr) & (pos_all < cand))
            return jnp.where(below <= need, cand, pos)

        return lax.fori_loop(0, nbits, pos_step, jnp.zeros((t, 1), I32))

    cut = lax.cond(has_ties, tie_cut, lambda: jnp.full((t, 1), total, I32))

    qbd = qbd_ref[0]

    def masked_logits(k_pages):
        lgs = []
        for k_t, ks in k_pages:
            tile = score_scr[:, pl.ds(ks, page)]
            sel = (tile > thr) | ((tile == thr) & (lane + ks < cut))
            sel = jnp.concatenate([sel.astype(I32)] * B_HEADS, axis=0) > 0
            lgs.append(jnp.where(sel, _dot(qbd, k_t), -jnp.inf))
        return jnp.concatenate(lgs, axis=1)

    def accumulate(lg, v_pages, carry):
        m_old, l_old, acc = carry
        m_new = jnp.maximum(m_old, jnp.max(lg, axis=1, keepdims=True))
        m_safe = jnp.where(m_new == -jnp.inf, 0.0, m_new)
        alpha = jnp.exp(m_old - m_safe)
        e = jnp.exp(lg - m_safe)
        l_new = alpha * l_old + jnp.sum(e, axis=1, keepdims=True)
        acc = alpha * acc
        for g, v_t in enumerate(v_pages):
            acc = acc + _dot_nt(e[:, g * page:(g + 1) * page].astype(BF16), v_t)
        return m_new, l_new, acc

    def group_pages(i):
        return [i * group + g for g in range(group)]

    def k_operands(i):
        return [(k_buf[slot(p)].astype(BF16), pl.multiple_of(p * page, page)) for p in group_pages(i)]

    def v_operands(i):
        return [v_buf[slot(p)].astype(BF16) for p in group_pages(i)]

    def refill(copy, i):
        for p in group_pages(i):
            @pl.when(p + depth < npages)
            def _():
                copy(b, p + depth).start()

            @pl.when((p + depth >= npages) & (b + 1 < nb))
            def _():
                copy(b + 1, p + depth - npages).start()

    ngroups = npages // group
    for p in group_pages(0):
        k_copy(b, p).wait()
    lg_first = masked_logits(k_operands(0))
    refill(k_copy, 0)

    def attend_groups(i, carry):
        lg, state = carry
        for p in group_pages(i + 1):
            k_copy(b, p).wait()
        for p in group_pages(i):
            v_copy(b, p).wait()
        lg_next = masked_logits(k_operands(i + 1))
        state = accumulate(lg, v_operands(i), state)
        refill(k_copy, i + 1)
        refill(v_copy, i)
        return lg_next, state

    nrow = B_HEADS * t
    state = (jnp.full((nrow, 1), -jnp.inf, F32), jnp.zeros((nrow, 1), F32), jnp.zeros((nrow, B_WIDTH), F32))
    lg_last, state = lax.fori_loop(0, ngroups - 1, attend_groups, (lg_first, state))
    for p in group_pages(ngroups - 1):
        v_copy(b, p).wait()
    state = accumulate(lg_last, v_operands(ngroups - 1), state)
    refill(v_copy, ngroups - 1)
    _, den, acc = accumulate(masked_logits([(kn_ref[0], npages * page)]), [vn_ref[0]], state)
    o = acc / den
    col = lax.broadcasted_iota(I32, (t, B_WIDTH), 1)
    out = jnp.zeros((t, B_WIDTH), F32)
    for h in range(B_HEADS):
        out = jnp.where((col >= h * B_DH) & (col < (h + 1) * B_DH), o[h * t:(h + 1) * t], out)
    o_ref[0] = (out * gb_ref[0].astype(F32)).astype(BF16)


def _dsa_sample(page_table, qi_rows, wrep, qbd, kin_t, kn_t, vn_t, gbs, cache_kidx_t, cache_k_t, cache_v_t, t):
    bd, npages = page_table.shape
    page = cache_kidx_t.shape[3]
    total = npages * page + t
    topk = min(TOPK_MAX, total // 4)
    group = max(g for g in range(1, SAMPLE_PAGE_GROUP + 1) if npages % g == 0)
    depth = max(d for d in range(group, min(SAMPLE_DMA_DEPTH, npages) + 1, group) if npages % d == 0)
    per_b = lambda b, pt: (b, 0, 0)
    hbm = pl.BlockSpec(memory_space=pl.ANY)
    grid_spec = pltpu.PrefetchScalarGridSpec(
        num_scalar_prefetch=1, grid=(bd,),
        in_specs=[
            pl.BlockSpec((1, IDX_HEADS * t, IDX_DIM), per_b),
            pl.BlockSpec((1, IDX_HEADS * t, page), per_b),
            pl.BlockSpec((1, B_HEADS * t, B_WIDTH), per_b),
            pl.BlockSpec((1, IDX_DIM, page), per_b),
            pl.BlockSpec((1, B_WIDTH, page), per_b),
            pl.BlockSpec((1, B_WIDTH, page), per_b),
            pl.BlockSpec((1, t, B_WIDTH), per_b),
            hbm, hbm, hbm,
        ],
        out_specs=pl.BlockSpec((1, t, B_WIDTH), per_b),
        scratch_shapes=[
            pltpu.VMEM((2, npages, IDX_DIM, page), F32),
            pltpu.VMEM((depth, B_WIDTH, page), F32),
            pltpu.VMEM((depth, B_WIDTH, page), F32),
            pltpu.VMEM((t, (npages + 1) * page), F32),
            pltpu.SemaphoreType.DMA((2 * npages,)),
            pltpu.SemaphoreType.DMA((depth,)),
            pltpu.SemaphoreType.DMA((depth,)),
        ],
    )
    return pl.pallas_call(
        functools.partial(_dsa_sample_kernel, t=t, npages=npages, topk=topk, depth=depth, group=group),
        grid_spec=grid_spec,
        out_shape=jax.ShapeDtypeStruct((bd, t, B_WIDTH), BF16),
        compiler_params=pltpu.CompilerParams(dimension_semantics=("arbitrary",), vmem_limit_bytes=VMEM_LIMIT),
        name="dsa_sample",
    )(page_table, qi_rows, wrep, qbd, kin_t, kn_t, vn_t, gbs, cache_kidx_t, cache_k_t, cache_v_t)


def _merge_kernel(x_ref, ma_ref, mb_ref, w_ref, g_ref, b_ref, y_ref, *, alpha):
    mix = jnp.concatenate([ma_ref[h] for h in range(A_HEADS)] + [mb_ref[...]], axis=1)
    y = alpha * x_ref[...] + _dot(mix, w_ref[...])
    mu = jnp.mean(y, axis=-1, keepdims=True)
    d = y - mu
    var = jnp.mean(d * d, axis=-1, keepdims=True)
    y_ref[...] = d * lax.rsqrt(var + LN_EPS) * g_ref[...] + b_ref[...]


def _merge(x, mix_a, mix_b, w_out, ln_g, ln_b, alpha, tm):
    n, dm = x.shape
    return pl.pallas_call(
        functools.partial(_merge_kernel, alpha=alpha), grid=(n // tm,),
        in_specs=[
            pl.BlockSpec((tm, dm), lambda i: (i, 0)),
            pl.BlockSpec((A_HEADS, tm, A_DV), lambda i: (0, i, 0)),
            pl.BlockSpec((tm, B_WIDTH), lambda i: (i, 0)),
            pl.BlockSpec((A_WIDTH + B_WIDTH, dm), lambda i: (0, 0)),
            pl.BlockSpec((1, dm), lambda i: (0, 0)),
            pl.BlockSpec((1, dm), lambda i: (0, 0)),
        ],
        out_specs=pl.BlockSpec((tm, dm), lambda i: (i, 0)),
        out_shape=jax.ShapeDtypeStruct((n, dm), F32),
        compiler_params=pltpu.CompilerParams(dimension_semantics=("parallel",), vmem_limit_bytes=VMEM_LIMIT),
        name="merge",
    )(x, mix_a, mix_b, w_out, ln_g, ln_b)


def _split_weights(w_in_l):
    offs = np.cumsum([0, 512, 512, 512, 512, 512, 512, 512, 512, IDX_HEADS * IDX_DIM, IDX_DIM, IDX_HEADS])
    col = lambda i: w_in_l[:, offs[i]:offs[i + 1]]
    qa, fa, ia, ga, qb, kb, vb, gb, qi, ki, wi = (col(i) for i in range(11))
    pad = jnp.zeros((w_in_l.shape[0], LANES - IDX_DIM), w_in_l.dtype)
    wn = jnp.concatenate([qa, fa, ia, ga, kb, gb, ki, pad], axis=1).astype(BF16)
    wt = jnp.concatenate([qb, kb, vb, qi, wi, ki], axis=1).T.astype(BF16)
    return wn, wt


def _layer(xp, xs, cache_k, cache_v, cache_kidx, s0_sample, page_table, w_in_l, lb_l, norm_g_l, kn_g_l, kn_b_l,
           w_out_l, ln_g_l, ln_b_l, alpha):
    b, l, dm = xp.shape
    bd, t, _ = xs.shape
    npages, page = page_table.shape[1], cache_k.shape[2]
    assert page == Q_BLOCK and l % page == 0
    past = npages * page
    wn, wt = _split_weights(w_in_l)
    lb = lb_l.reshape(1, A_WIDTH)
    ng = norm_g_l.reshape(1, A_DV)
    w_out_b = w_out_l.astype(BF16)
    lng, lnb = ln_g_l.reshape(1, dm), ln_b_l.reshape(1, dm)

    tm = 256
    xp2 = xp.reshape(b * l, dm)
    (hq, hk, hg, hv, hgate, kbf, gbs, kibf, qT, vT, qiT2, wT, k_p, v_p, ki_p) = _project(
        xp2, jnp.arange(l, dtype=I32), wn, wt, lb, kn_g_l, kn_b_l, tm)
    mix_a, s_p = _hgrn_prompt(hq, hk, hg, hv, hgate, ng, b, l, min(l, 512))
    mix_b = _dsa_prompt(kibf, qiT2, wT, kbf, qT, vT, gbs, b, l)
    y_p = _merge(xp2, mix_a, mix_b, w_out_b, lng, lnb, alpha, 512).reshape(b, l, dm)

    ns = bd * t
    xs2 = xs.reshape(ns, dm)
    pos_s = past + (jnp.arange(ns, dtype=I32) % t)
    (hq, hk, hg, hv, hgate, kbf, gbs, kibf, qT, vT, qiT2, wT, k_s, v_s, ki_s) = _project(
        xs2, pos_s, wn, wt, lb, kn_g_l, kn_b_l, ns)
    mix_a, s_s = _hgrn_sample(hq, hk, hg, hv, hgate, ng, s0_sample, t)
    qi_nat = qiT2.reshape(ns // Q_BLOCK, IDX_DIM, IDX_HEADS, Q_BLOCK).transpose(0, 3, 2, 1)
    qi_rows = qi_nat.reshape(bd, t, IDX_HEADS, IDX_DIM).transpose(0, 2, 1, 3).reshape(bd, IDX_HEADS * t, IDX_DIM)
    w_rows = wT.T.reshape(bd, t, IDX_HEADS).transpose(0, 2, 1).reshape(bd, IDX_HEADS * t, 1)
    wrep = jnp.broadcast_to(w_rows, (bd, IDX_HEADS * t, page))
    q_nat = qT.T.reshape(bd, 1, t, B_HEADS, B_DH)
    eye = jnp.eye(B_HEADS, dtype=BF16).reshape(1, B_HEADS, 1, B_HEADS, 1)
    qbd = (q_nat * eye).reshape(bd, B_HEADS * t, B_WIDTH)

    def per_seq(pages):
        feat = pages.shape[1]
        a = pages.transpose(1, 0, 2).reshape(feat, bd, t).transpose(1, 0, 2).astype(BF16)
        return jnp.concatenate([a, jnp.zeros((bd, feat, page - t), BF16)], axis=2)

    token_minor = lambda c: jnp.moveaxis(c, 2, -1).reshape(c.shape[0], c.shape[1], -1, page)
    mix_b = _dsa_sample(page_table, qi_rows, wrep, qbd, per_seq(ki_s), per_seq(k_s), per_seq(v_s),
                        gbs.reshape(bd, t, B_WIDTH), token_minor(cache_kidx), token_minor(cache_k),
                        token_minor(cache_v), t).reshape(ns, B_WIDTH)
    y_s = _merge(xs2, mix_a, mix_b, w_out_b, lng, lnb, alpha, ns).reshape(bd, t, dm)

    return (y_p, y_s, k_p, v_p, ki_p, s_p, k_s, v_s, ki_s, s_s)


def kernel(x_prompt, x_sample, cache_k, cache_v, cache_kidx, state_hgrn, page_table, w_in, hgrn_lb_logits,
           hgrn_norm_g, idx_norm_g, idx_norm_b, w_out, ln_g, ln_b):
    depth = w_in.shape[0]
    assert depth == 1, "one layer per step"
    b, l, _ = x_prompt.shape
    bd, t, _ = x_sample.shape
    page = cache_k.shape[2]
    alpha = (2.0 * depth) ** 0.25
    lbs = jnp.cumsum(jax.nn.softmax(hgrn_lb_logits.astype(F32), axis=0), axis=0)[:depth]
    (y_p, y_s, k_p, v_p, ki_p, s_p, k_s, v_s, ki_s, s_s) = _layer(
        x_prompt, x_sample, cache_k, cache_v, cache_kidx, state_hgrn[0], page_table, w_in[0], lbs[0],
        hgrn_norm_g[0], idx_norm_g[0], idx_norm_b[0], w_out[0], ln_g[0], ln_b[0], alpha)
    nat_p = lambda pg, *f: jnp.moveaxis(pg.reshape(b, l // page, 1, *f, page), -1, 3)
    nat_s = lambda pg, *f: pg.transpose(0, 2, 1).reshape(bd, 1, t, *f)
    return (
        y_p, y_s,
        nat_p(k_p, B_HEADS, B_DH), nat_p(v_p, B_HEADS, B_DH), nat_p(ki_p, IDX_DIM),
        s_p[None],
        nat_s(k_s, B_HEADS, B_DH), nat_s(v_s, B_HEADS, B_DH), nat_s(ki_s, IDX_DIM),
        s_s[None],
    )
```

```python
import functools
import math

import jax
import jax.numpy as jnp
import numpy as np
from jax import lax
from jax.experimental import pallas as pl
from jax.experimental.pallas import tpu as pltpu

F32 = jnp.float32
BF16 = jnp.bfloat16
I32 = jnp.int32

A_HEADS = 4
A_DK = 128
A_DV = 128
B_HEADS = 8
B_DH = 64
IDX_HEADS = 16
IDX_DIM = 64
TOPK_MAX = 256
ROPE_THETA = 500000.0
ROT = 16
ROT_HALF = ROT // 2
LN_EPS = 1e-5
Q_BLOCK = 128
K_TILE = 256
ATT_TILE = 512
HGRN_CHUNK = 128
HGRN_SUB = 16
LANES = 128
VMEM_LIMIT = 56 * 1024 * 1024
SAMPLE_DMA_DEPTH = 32
SAMPLE_PAGE_GROUP = 8
SAMPLE_SEQS_PER_STEP = 2

A_WIDTH = A_HEADS * A_DV
B_WIDTH = B_HEADS * B_DH
NAT_WIDTH = 6 * 512 + LANES
TR_ROWS = 3 * 512 + IDX_HEADS * IDX_DIM + IDX_HEADS + IDX_DIM
INT_MIN = -(2 ** 31)
KEY_NEG_INF = -0x7F800000


def _dot(a, b):
    return jnp.dot(a, b, preferred_element_type=F32)


def _dot_nt(a, b):
    return lax.dot_general(a, b, (((1,), (1,)), ((), ())), preferred_element_type=F32)


def _dot_tn(a, b):
    return lax.dot_general(a, b, (((0,), (0,)), ((), ())), preferred_element_type=F32)


def _silu(x):
    return x * jax.nn.sigmoid(x)


def _key_to_float(key):
    m = key >> 31
    mag = (key ^ m) - m
    return pltpu.bitcast(mag | (m & INT_MIN), F32)


def _proj_kernel(x_ref, wn_ref, wt_ref, lb_ref, kng_ref, knb_ref, kngc_ref, knbc_ref, cn_ref, sa_ref, sb_ref,
                 ct_ref, st_ref,
                 hq_ref, hk_ref, hg_ref, hv_ref, hgate_ref, kbf_ref, gb_ref, kibf_ref,
                 qT_ref, vT_ref, qiT_ref, wT_ref, kTp_ref, vTp_ref, kiTp_ref):
    tm = x_ref.shape[0]
    xb = x_ref[...].astype(BF16)

    def nat(col, width=512):
        return _dot(xb, wn_ref[:, col:col + width])

    lb = lb_ref[...]
    qa = nat(0)
    hq = _silu(qa)
    fa = nat(512)
    hg = jnp.log2(lb + (1.0 - lb) * jax.nn.sigmoid(fa))
    hk = (1.0 - lb) * jax.nn.sigmoid(-fa)
    ia = nat(1024)
    hgate = _silu(nat(1536))
    for h in range(A_HEADS):
        sl = slice(h * A_DK, (h + 1) * A_DK)
        hq_ref[h] = hq[:, sl].astype(BF16)
        hk_ref[h] = hk[:, sl]
        hg_ref[h] = hg[:, sl]
        hv_ref[h] = ia[:, sl].astype(BF16)
        hgate_ref[h] = hgate[:, sl].astype(BF16)

    cn, sa, sb = cn_ref[...], sa_ref[...], sb_ref[...]

    def rope_nat(xc):
        return xc * cn + pltpu.roll(xc, LANES - ROT_HALF, 1) * sa + pltpu.roll(xc, ROT_HALF, 1) * sb

    kb = nat(2048)
    kbf_ref[...] = jnp.concatenate(
        [rope_nat(kb[:, c * LANES:(c + 1) * LANES]) for c in range(B_WIDTH // LANES)], axis=1).astype(BF16)
    gb_ref[...] = _silu(nat(2560)).astype(BF16)

    kic = nat(3072, LANES)
    lane = lax.broadcasted_iota(I32, (tm, LANES), 1)
    inb = lane < IDX_DIM
    mu = jnp.sum(jnp.where(inb, kic, 0.0), axis=-1, keepdims=True) * (1.0 / IDX_DIM)
    d = jnp.where(inb, kic - mu, 0.0)
    var = jnp.sum(d * d, axis=-1, keepdims=True) * (1.0 / IDX_DIM)
    kin = d * lax.rsqrt(var + LN_EPS) * kng_ref[...] + knb_ref[...]
    kibf_ref[...] = rope_nat(kin)[:, :IDX_DIM].astype(BF16)

    ct, st = ct_ref[...], st_ref[...]
    npage = tm // Q_BLOCK

    def tr(row, height):
        return _dot_nt(wt_ref[row:row + height, :], xb)

    def rope_tr(blk):
        x1, x2 = blk[0:ROT_HALF], blk[ROT_HALF:ROT]
        return jnp.concatenate([x1 * ct - x2 * st, x1 * st + x2 * ct, blk[ROT:]], axis=0)

    qbT = tr(0, B_WIDTH)
    for h in range(B_HEADS):
        blk = rope_tr(qbT[h * B_DH:(h + 1) * B_DH])
        qT_ref[h * B_DH:(h + 1) * B_DH, :] = (blk * (B_DH ** -0.5)).astype(BF16)
    kbT = tr(B_WIDTH, B_WIDTH)
    for h in range(B_HEADS):
        blk = rope_tr(kbT[h * B_DH:(h + 1) * B_DH])
        for pg in range(npage):
            kTp_ref[pg, h * B_DH:(h + 1) * B_DH, :] = blk[:, pg * Q_BLOCK:(pg + 1) * Q_BLOCK]
    vbT = tr(2 * B_WIDTH, B_WIDTH)
    vT_ref[...] = vbT.astype(BF16)
    for pg in range(npage):
        vTp_ref[pg] = vbT[:, pg * Q_BLOCK:(pg + 1) * Q_BLOCK]
    row = 3 * B_WIDTH
    qiT = tr(row, IDX_HEADS * IDX_DIM)
    for h in range(IDX_HEADS):
        blk = rope_tr(qiT[h * IDX_DIM:(h + 1) * IDX_DIM]).astype(BF16)
        for pg in range(npage):
            qiT_ref[pg, :, h * Q_BLOCK:(h + 1) * Q_BLOCK] = blk[:, pg * Q_BLOCK:(pg + 1) * Q_BLOCK]
    row += IDX_HEADS * IDX_DIM
    wT_ref[...] = tr(row, IDX_HEADS) * (IDX_HEADS ** -0.5 * IDX_DIM ** -0.5)
    row += IDX_HEADS
    kiT = tr(row, IDX_DIM)
    muT = jnp.mean(kiT, axis=0, keepdims=True)
    dT = kiT - muT
    varT = jnp.mean(dT * dT, axis=0, keepdims=True)
    kiT = rope_tr(dT * lax.rsqrt(varT + LN_EPS) * kngc_ref[...] + knbc_ref[...])
    for pg in range(npage):
        kiTp_ref[pg] = kiT[:, pg * Q_BLOCK:(pg + 1) * Q_BLOCK]


def _rope_tables(pos):
    inv = ROPE_THETA ** (-jnp.arange(ROT_HALF, dtype=F32) / ROT_HALF)
    ang = pos.astype(F32)[:, None] * inv[None, :]
    cos, sin = jnp.cos(ang), jnp.sin(ang)
    p = pos.shape[0]
    one = jnp.ones((p, B_DH - ROT), F32)
    zero8 = jnp.zeros((p, ROT_HALF), F32)
    zero = jnp.zeros((p, B_DH - ROT), F32)
    cn = jnp.concatenate([cos, cos, one], axis=1)
    sa = jnp.concatenate([-sin, zero8, zero], axis=1)
    sb = jnp.concatenate([zero8, sin, zero], axis=1)
    tile2 = lambda a: jnp.concatenate([a, a], axis=1)
    return tile2(cn), tile2(sa), tile2(sb), cos.T, sin.T


def _project(x, pos, wn, wt, lb, kn_g, kn_b, tm):
    n, dm = x.shape
    p = pos.shape[0]
    nper = p // tm
    cn, sa, sb, ct, st = _rope_tables(pos)
    padl = lambda a: jnp.concatenate([a, jnp.zeros((LANES - IDX_DIM,), a.dtype)]).reshape(1, LANES)
    grid = (n // tm,)
    row = lambda i: (i, 0)
    full = lambda i: (0, 0)
    head = lambda i: (0, i, 0)
    page = lambda i: (i, 0, 0)
    per = lambda i: (i % nper, 0)
    perT = lambda i: (0, i % nper)
    colT = lambda i: (0, i)
    in_specs = [
        pl.BlockSpec((tm, dm), row),
        pl.BlockSpec((dm, NAT_WIDTH), full, pipeline_mode=pl.Buffered(1)),
        pl.BlockSpec((TR_ROWS, dm), full, pipeline_mode=pl.Buffered(1)),
        pl.BlockSpec((1, A_WIDTH), full),
        pl.BlockSpec((1, LANES), full),
        pl.BlockSpec((1, LANES), full),
        pl.BlockSpec((IDX_DIM, 1), full),
        pl.BlockSpec((IDX_DIM, 1), full),
        pl.BlockSpec((tm, LANES), per),
        pl.BlockSpec((tm, LANES), per),
        pl.BlockSpec((tm, LANES), per),
        pl.BlockSpec((ROT_HALF, tm), perT),
        pl.BlockSpec((ROT_HALF, tm), perT),
    ]
    npage = tm // Q_BLOCK
    hshape = lambda dt: jax.ShapeDtypeStruct((A_HEADS, n, A_DK), dt)
    out_shape = [
        hshape(BF16), hshape(F32), hshape(F32), hshape(BF16), hshape(BF16),
        jax.ShapeDtypeStruct((n, B_WIDTH), BF16), jax.ShapeDtypeStruct((n, B_WIDTH), BF16),
        jax.ShapeDtypeStruct((n, IDX_DIM), BF16),
        jax.ShapeDtypeStruct((B_WIDTH, n), BF16), jax.ShapeDtypeStruct((B_WIDTH, n), BF16),
        jax.ShapeDtypeStruct((n // Q_BLOCK, IDX_DIM, IDX_HEADS * Q_BLOCK), BF16),
        jax.ShapeDtypeStruct((IDX_HEADS, n), F32),
        jax.ShapeDtypeStruct((n // Q_BLOCK, B_WIDTH, Q_BLOCK), F32),
        jax.ShapeDtypeStruct((n // Q_BLOCK, B_WIDTH, Q_BLOCK), F32),
        jax.ShapeDtypeStruct((n // Q_BLOCK, IDX_DIM, Q_BLOCK), F32),
    ]
    hspec = pl.BlockSpec((A_HEADS, tm, A_DK), head)
    out_specs = [
        hspec, hspec, hspec, hspec, hspec,
        pl.BlockSpec((tm, B_WIDTH), row), pl.BlockSpec((tm, B_WIDTH), row),
        pl.BlockSpec((tm, IDX_DIM), row),
        pl.BlockSpec((B_WIDTH, tm), colT), pl.BlockSpec((B_WIDTH, tm), colT),
        pl.BlockSpec((npage, IDX_DIM, IDX_HEADS * Q_BLOCK), page),
        pl.BlockSpec((IDX_HEADS, tm), colT),
        pl.BlockSpec((npage, B_WIDTH, Q_BLOCK), page),
        pl.BlockSpec((npage, B_WIDTH, Q_BLOCK), page),
        pl.BlockSpec((npage, IDX_DIM, Q_BLOCK), page),
    ]
    return pl.pallas_call(
        _proj_kernel, grid=grid, in_specs=in_specs, out_specs=out_specs, out_shape=out_shape,
        compiler_params=pltpu.CompilerParams(dimension_semantics=("parallel",), vmem_limit_bytes=VMEM_LIMIT),
        name="proj",
    )(x, wn, wt, lb, padl(kn_g), padl(kn_b), kn_g.reshape(IDX_DIM, 1), kn_b.reshape(IDX_DIM, 1),
      cn, sa, sb, ct, st)


def _hgrn_cumsum(g, c):
    if c == LANES:
        ri = lax.broadcasted_iota(I32, (c, c), 0)
        ci = lax.broadcasted_iota(I32, (c, c), 1)
        tri = jnp.where(ci <= ri, 1.0, 0.0).astype(BF16)
        g1 = g.astype(BF16)
        e1 = g - g1.astype(F32)
        g2 = e1.astype(BF16)
        g3 = (e1 - g2.astype(F32)).astype(BF16)
        return _dot(tri, g1) + _dot(tri, g2) + _dot(tri, g3)
    ri = lax.broadcasted_iota(I32, (c, A_DK), 0)
    cum = jnp.zeros((c, A_DK), F32)
    for s in range(c):
        cum = cum + jnp.where(ri >= s, g[s:s + 1], 0.0)
    return cum


def _hgrn_chunks(chains, ng, c):
    r = HGRN_SUB
    nsub = c // r
    rowi = lax.broadcasted_iota(I32, (r, A_DK), 0)
    lane = lax.broadcasted_iota(I32, (r, LANES), 1)
    cums = [_hgrn_cumsum(g, c) for _, _, g, _, _, _ in chains]

    light = []
    for (q, k, _, v, _, st), cum in zip(chains, cums):
        o_state = _dot_nt((q * jnp.exp2(cum)).astype(BF16), st.astype(BF16))
        off = [None]
        for i in range(1, nsub):
            ref = cum[i * r - 1:i * r]
            qt = (q[i * r:(i + 1) * r] * jnp.exp2(cum[i * r:(i + 1) * r] - ref)).astype(BF16)
            kt = (k[:i * r] * jnp.exp2(ref - cum[:i * r])).astype(BF16)
            kt = jnp.concatenate([kt, jnp.zeros((LANES - i * r, A_DK), BF16)], axis=0)
            off.append(_dot_nt(qt, kt))
        vb = v.astype(BF16)
        last = cum[c - 1:c]
        kh = (k * jnp.exp2(last - cum)).astype(BF16)
        if c < LANES:
            zpad = jnp.zeros((LANES - c, A_DK), BF16)
            vb = jnp.concatenate([vb, zpad], axis=0)
            kh = jnp.concatenate([kh, zpad], axis=0)
        st_new = st * jnp.exp2(last) + _dot_tn(vb, kh)
        light.append((o_state, off, vb, st_new))

    rsums = []
    for (q, k, _, _, _, _), cum in zip(chains, cums):
        prods = []
        for i in range(nsub):
            qs, cs = q[i * r:(i + 1) * r], cum[i * r:(i + 1) * r]
            for s in range(r):
                row = i * r + s
                p = (qs * k[row:row + 1]) * jnp.exp2(cs - cum[row:row + 1])
                prods.append(jnp.where(rowi >= s, p, 0.0).astype(BF16))
        if nsub % 2 == 0:
            half = len(prods) // 2
            both = jnp.concatenate([jnp.concatenate(prods[:half], axis=0), jnp.concatenate(prods[half:], axis=0)],
                                   axis=1)
            wr = lax.broadcasted_iota(I32, (2 * A_DK, 2 * LANES), 0) < A_DK
            wc = lax.broadcasted_iota(I32, (2 * A_DK, 2 * LANES), 1) < LANES
            sums = _dot(both, jnp.where(wr == wc, 1.0, 0.0).astype(BF16))
            rsums.append(lambda row, sums=sums, half=half: (
                sums[row * r:(row + 1) * r, :LANES] if row < half
                else sums[(row - half) * r:(row - half + 1) * r, LANES:]))
        else:
            sums = _dot(jnp.concatenate(prods, axis=0), jnp.ones((A_DK, LANES), BF16))
            rsums.append(lambda row, sums=sums: sums[row * r:(row + 1) * r])

    results = []
    for (_, _, _, _, gate, _), (o_state, off, vb, st_new), rsum in zip(chains, light, rsums):
        sc_rows = []
        for i in range(nsub):
            sci = jnp.zeros((r, LANES), F32)
            for s in range(r):
                row = i * r + s
                sci = jnp.where(lane == row, rsum(row), sci)
            sc_rows.append(sci if i == 0 else sci + off[i])
        sc = jnp.concatenate(sc_rows, axis=0).astype(BF16)
        o = o_state + _dot(sc, vb)
        ms = jnp.mean(o * o, axis=-1, keepdims=True)
        results.append((o * lax.rsqrt(ms + LN_EPS) * ng * gate, st_new))
    return results


def _hgrn_prompt_kernel(q_ref, k_ref, g_ref, v_ref, gate_ref, ng_ref, o_ref, sf_ref, st_scr):
    c = HGRN_CHUNK
    nchunk = q_ref.shape[1] // c
    tb = pl.program_id(1)

    @pl.when(tb == 0)
    def _():
        st_scr[...] = jnp.zeros_like(st_scr)

    ng = ng_ref[...]

    def body(j, carry):
        sl = pl.ds(pl.multiple_of(j * c, c), c)
        loaded = [(q_ref[h, sl, :].astype(F32), k_ref[h, sl, :], g_ref[h, sl, :], v_ref[h, sl, :].astype(F32),
                   gate_ref[h, sl, :].astype(F32), st_scr[h]) for h in range(A_HEADS)]
        for h, (o, st_new) in enumerate(_hgrn_chunks(loaded, ng, c)):
            st_scr[h] = st_new
            o_ref[h, sl, :] = o.astype(BF16)
        return carry

    lax.fori_loop(0, nchunk, body, 0)

    @pl.when(tb == pl.num_programs(1) - 1)
    def _():
        for h in range(A_HEADS):
            sf_ref[0, h] = st_scr[h].T


def _hgrn_prompt(hq, hk, hg, hv, hgate, ng, batch, seq, tb):
    n = batch * seq
    nt = seq // tb
    blk = pl.BlockSpec((A_HEADS, tb, A_DK), lambda b, t: (0, b * nt + t, 0))
    return pl.pallas_call(
        _hgrn_prompt_kernel, grid=(batch, nt),
        in_specs=[blk, blk, blk, blk, blk, pl.BlockSpec((1, A_DV), lambda b, t: (0, 0))],
        out_specs=[blk, pl.BlockSpec((1, A_HEADS, A_DK, A_DV), lambda b, t: (b, 0, 0, 0))],
        out_shape=[jax.ShapeDtypeStruct((A_HEADS, n, A_DV), BF16),
                   jax.ShapeDtypeStruct((batch, A_HEADS, A_DK, A_DV), F32)],
        scratch_shapes=[pltpu.VMEM((A_HEADS, A_DV, A_DK), F32)],
        compiler_params=pltpu.CompilerParams(dimension_semantics=("parallel", "arbitrary"),
                                             vmem_limit_bytes=VMEM_LIMIT),
        name="hgrn_prompt",
    )(hq, hk, hg, hv, hgate, ng)


def _hgrn_sample_kernel(q_ref, k_ref, g_ref, v_ref, gate_ref, ng_ref, s0_ref, o_ref, sf_ref, *, t, nb):
    c = HGRN_SUB
    ng = ng_ref[...]
    zpad = jnp.zeros((c - t, A_DK), F32)
    pad = lambda a: jnp.concatenate([a, zpad], axis=0)
    chains = []
    for h in range(A_HEADS):
        q, k, g = q_ref[h].astype(F32), k_ref[h], g_ref[h]
        v, gate = v_ref[h].astype(F32), gate_ref[h].astype(F32)
        for b in range(nb):
            sl = slice(b * t, (b + 1) * t)
            chains.append((pad(q[sl]), pad(k[sl]), pad(g[sl]), pad(v[sl]), pad(gate[sl]), s0_ref[b, h].T))
    results = _hgrn_chunks(chains, ng, c)
    for h in range(A_HEADS):
        for b in range(nb):
            sf_ref[b, h] = results[h * nb + b][1].T
        o_ref[h] = jnp.concatenate([results[h * nb + b][0][:t] for b in range(nb)], axis=0).astype(BF16)


def _hgrn_sample(hq, hk, hg, hv, hgate, ng, s0, t):
    bd = s0.shape[0]
    nb = 16 // t
    blk = pl.BlockSpec((A_HEADS, nb * t, A_DK), lambda i: (0, i, 0))
    sblk = pl.BlockSpec((nb, A_HEADS, A_DK, A_DV), lambda i: (i, 0, 0, 0))
    return pl.pallas_call(
        functools.partial(_hgrn_sample_kernel, t=t, nb=nb), grid=(bd // nb,),
        in_specs=[blk, blk, blk, blk, blk, pl.BlockSpec((1, A_DV), lambda i: (0, 0)), sblk],
        out_specs=[blk, sblk],
        out_shape=[jax.ShapeDtypeStruct((A_HEADS, bd * t, A_DV), BF16),
                   jax.ShapeDtypeStruct((bd, A_HEADS, A_DK, A_DV), F32)],
        compiler_params=pltpu.CompilerParams(dimension_semantics=("parallel",), vmem_limit_bytes=VMEM_LIMIT),
        name="hgrn_sample",
    )(hq, hk, hg, hv, hgate, ng, s0)


def _count_tiles(score_ref, ntiles, pred):
    def body(j, acc):
        ks = pl.multiple_of(j * ATT_TILE, ATT_TILE)
        hit = jnp.where(pred(score_ref[pl.ds(ks, ATT_TILE), :], ks), 1, 0)
        return acc + jnp.sum(hit.reshape(ATT_TILE // 8, 8, Q_BLOCK), axis=0)
    acc = lax.fori_loop(0, ntiles, body, jnp.zeros((8, Q_BLOCK), I32))
    return jnp.sum(acc, axis=0, keepdims=True)


def _kth_largest(count_ge, topk, shape, two_bits=False):
    c0 = count_ge(jnp.zeros(shape, F32))
    ok = c0 >= topk
    thr = jnp.where(ok, 0, INT_MIN)
    nge = jnp.where(ok, c0, 0)

    def place(bit, carry):
        thr, nge = carry
        cand = thr | bit
        cnt = count_ge(_key_to_float(cand))
        ok = cnt >= topk
        return jnp.where(ok, cand, thr), jnp.where(ok, cnt, nge)

    if not two_bits:
        return lax.fori_loop(0, 31, lambda i, c: place(jnp.left_shift(jnp.int32(1), 30 - i), c), (thr, nge))

    def place_two(i, carry):
        thr, nge = carry
        hi = jnp.left_shift(jnp.int32(1), 30 - 2 * i)
        lo = jnp.left_shift(jnp.int32(1), 29 - 2 * i)
        n_hi, n_lo, n_both = (count_ge(_key_to_float(thr | bits)) for bits in (hi, lo, hi | lo))
        ok_hi = n_hi >= topk
        thr1, nge1 = jnp.where(ok_hi, thr | hi, thr), jnp.where(ok_hi, n_hi, nge)
        n2 = jnp.where(ok_hi, n_both, n_lo)
        ok2 = n2 >= topk
        return jnp.where(ok2, thr1 | lo, thr1), jnp.where(ok2, n2, nge1)

    return place(jnp.int32(1), lax.fori_loop(0, 15, place_two, (thr, nge)))


def _dsa_prompt_kernel(ki_ref, qiT_ref, wT_ref, k_ref, qT_ref, vT_ref, gb_ref, o_ref,
                       score_scr, bias_scr, lg_scr, oT_scr, *, topk):
    i = pl.program_id(1)
    ntiles = (i * Q_BLOCK + Q_BLOCK + K_TILE - 1) // K_TILE
    qpos = i * Q_BLOCK + lax.broadcasted_iota(I32, (K_TILE, Q_BLOCK), 1)
    krow = lax.broadcasted_iota(I32, (K_TILE, Q_BLOCK), 0)

    qi = qiT_ref[0]
    wT = wT_ref[...]

    def score_body(j, carry):
        ks = pl.multiple_of(j * K_TILE, K_TILE)
        x = _dot(ki_ref[pl.ds(ks, K_TILE), :], qi)
        sc = jnp.zeros((K_TILE, Q_BLOCK), F32)
        for h in range(IDX_HEADS):
            sc = sc + jnp.maximum(x[:, h * Q_BLOCK:(h + 1) * Q_BLOCK], 0.0) * wT[h:h + 1]
        score_scr[pl.ds(ks, K_TILE), :] = jnp.where(krow + ks <= qpos, sc, -jnp.inf)
        return carry

    lax.fori_loop(0, ntiles, score_body, 0)

    natt = (i * Q_BLOCK + Q_BLOCK + ATT_TILE - 1) // ATT_TILE

    @pl.when(natt * (ATT_TILE // K_TILE) > ntiles)
    def _():
        score_scr[pl.ds(pl.multiple_of(ntiles * K_TILE, K_TILE), K_TILE), :] = jnp.full((K_TILE, Q_BLOCK), -jnp.inf, F32)

    count = functools.partial(_count_tiles, score_scr, natt)
    arow = lax.broadcasted_iota(I32, (ATT_TILE, Q_BLOCK), 0)
    thr, nge = lax.cond(
        (i + 1) * Q_BLOCK > topk,
        lambda: _kth_largest(lambda x: count(lambda tile, ks: tile >= x), topk, (1, Q_BLOCK)),
        lambda: (jnp.full((1, Q_BLOCK), KEY_NEG_INF, I32), jnp.zeros((1, Q_BLOCK), I32)))
    live = thr > KEY_NEG_INF
    thr = _key_to_float(jnp.maximum(thr, KEY_NEG_INF + 1))
    no_cut = jnp.full((1, Q_BLOCK), natt * ATT_TILE, I32)
    has_ties = jnp.max(jnp.where(live & (nge > topk), 1, 0)) > 0

    def tie_cut():
        need = topk - count(lambda tile, ks: tile > thr)
        nbits = max(1, int(math.ceil(math.log2(score_scr.shape[0] + 1))))

        def body(b, pos):
            cand = pos | jnp.left_shift(jnp.int32(1), nbits - 1 - b)
            below = count(lambda tile, ks: (tile == thr) & (arow + ks < cand))
            return jnp.where(below <= need, cand, pos)

        return lax.fori_loop(0, nbits, body, jnp.zeros((1, Q_BLOCK), I32))

    cut = lax.cond(has_ties, tie_cut, lambda: no_cut)

    def bias_body(j, carry):
        ks = pl.multiple_of(j * K_TILE, K_TILE)
        tile = score_scr[pl.ds(ks, K_TILE), :]
        sel = (tile > thr) | ((tile == thr) & (krow + ks < cut))
        bias_scr[pl.ds(ks, K_TILE), :] = jnp.where(sel, 0.0, -jnp.inf)
        return carry

    lax.fori_loop(0, natt * (ATT_TILE // K_TILE), bias_body, 0)

    npair = B_HEADS // 2
    z = jnp.zeros((B_DH, Q_BLOCK), BF16)
    rhs = []
    for p in range(npair):
        qp = qT_ref[p * LANES:(p + 1) * LANES, :]
        rhs.append(jnp.concatenate([jnp.concatenate([qp[:B_DH], z], axis=0),
                                    jnp.concatenate([z, qp[B_DH:]], axis=0)], axis=1))
    fold = lambda a: a.reshape(ATT_TILE // 8, 8, a.shape[1])

    def logit_body(j, mx):
        ks = pl.multiple_of(j * ATT_TILE, ATT_TILE)
        bias = bias_scr[pl.ds(ks, ATT_TILE), :]
        bias2 = jnp.concatenate([bias, bias], axis=1)
        lgs = [_dot(k_ref[pl.ds(ks, ATT_TILE), p * LANES:(p + 1) * LANES], rhs[p]) + bias2 for p in range(npair)]
        lg_scr[pl.ds(ks, ATT_TILE), :] = jnp.concatenate(lgs, axis=1)
        return tuple(jnp.maximum(m, jnp.max(fold(lg), axis=0)) for m, lg in zip(mx, lgs))

    mx = lax.fori_loop(0, natt, logit_body,
                       tuple(jnp.full((8, 2 * Q_BLOCK), -jnp.inf, F32) for _ in range(npair)))
    mx = jnp.concatenate([jnp.max(m, axis=0, keepdims=True) for m in mx], axis=1)
    oT_scr[...] = jnp.zeros_like(oT_scr)

    def value_body(j, den):
        ks = pl.multiple_of(j * ATT_TILE, ATT_TILE)
        new_den, new_acc = [], []
        for h in range(B_HEADS):
            cols = slice(h * Q_BLOCK, (h + 1) * Q_BLOCK)
            rows = slice(h * B_DH, (h + 1) * B_DH)
            e = jnp.exp(lg_scr[pl.ds(ks, ATT_TILE), cols] - mx[:, cols])
            new_den.append(den[h] + jnp.sum(fold(e), axis=0))
            new_acc.append(oT_scr[rows, :] + _dot(vT_ref[rows, pl.ds(ks, ATT_TILE)], e.astype(BF16)))
        for h in range(B_HEADS):
            oT_scr[h * B_DH:(h + 1) * B_DH, :] = new_acc[h]
        return tuple(new_den)

    den = lax.fori_loop(0, natt, value_body, tuple(jnp.zeros((8, Q_BLOCK), F32) for _ in range(B_HEADS)))
    for h in range(B_HEADS):
        rows = slice(h * B_DH, (h + 1) * B_DH)
        oT_scr[rows, :] = oT_scr[rows, :] / jnp.sum(den[h], axis=0, keepdims=True)

    o_ref[...] = (oT_scr[...].T * gb_ref[...].astype(F32)).astype(BF16)


def _dsa_prompt(kibf, qiT2, wT, kbf, qT, vT, gbs, batch, seq):
    n = batch * seq
    nq = seq // Q_BLOCK
    topk = min(TOPK_MAX, seq // 4)
    assert seq % ATT_TILE == 0 and topk <= K_TILE
    return pl.pallas_call(
        functools.partial(_dsa_prompt_kernel, topk=topk), grid=(batch, nq),
        in_specs=[
            pl.BlockSpec((seq, IDX_DIM), lambda b, i: (b, 0)),
            pl.BlockSpec((1, IDX_DIM, IDX_HEADS * Q_BLOCK), lambda b, i: (b * nq + i, 0, 0)),
            pl.BlockSpec((IDX_HEADS, Q_BLOCK), lambda b, i: (0, b * nq + i)),
            pl.BlockSpec((seq, B_WIDTH), lambda b, i: (b, 0)),
            pl.BlockSpec((B_WIDTH, Q_BLOCK), lambda b, i: (0, b * nq + i)),
            pl.BlockSpec((B_WIDTH, seq), lambda b, i: (0, b)),
            pl.BlockSpec((Q_BLOCK, B_WIDTH), lambda b, i: (b * nq + i, 0)),
        ],
        out_specs=pl.BlockSpec((Q_BLOCK, B_WIDTH), lambda b, i: (b * nq + i, 0)),
        out_shape=jax.ShapeDtypeStruct((n, B_WIDTH), BF16),
        scratch_shapes=[
            pltpu.VMEM((seq, Q_BLOCK), F32),
            pltpu.VMEM((seq, Q_BLOCK), F32),
            pltpu.VMEM((seq, B_HEADS * Q_BLOCK), F32),
            pltpu.VMEM((B_WIDTH, Q_BLOCK), F32),
        ],
        compiler_params=pltpu.CompilerParams(dimension_semantics=("parallel", "arbitrary"),
                                             vmem_limit_bytes=VMEM_LIMIT),
        name="dsa_prompt",
    )(kibf, qiT2, wT, kbf, qT, vT, gbs)


def _dsa_sample_kernel(pt_ref, qi_ref, wrep_ref, qbd_ref, kin_ref, kn_ref, vn_ref, gb_ref,
                       cki_hbm, ck_hbm, cv_hbm, o_ref, ki_buf, k_buf, v_buf, score_scr, sem_i, sem_k, sem_v,
                       *, t, npages, topk, depth, group, nseq):
    step = pl.program_id(0)
    nsteps = pl.num_programs(0)
    page = ki_buf.shape[3]
    total = (npages + 1) * page
    rows = nseq * t
    lane = lax.broadcasted_iota(I32, (t, page), 1)
    qrow = lax.broadcasted_iota(I32, (t, page), 0)

    def slot(p):
        return p % depth if isinstance(p, int) else lax.rem(p, depth)

    def ki_copy(seq, p):
        half = lax.rem(seq, 2 * nseq)
        return pltpu.make_async_copy(cki_hbm.at[pt_ref[seq, p], 0], ki_buf.at[half, p], sem_i.at[half * npages + p])

    def k_copy(seq, p):
        return pltpu.make_async_copy(ck_hbm.at[pt_ref[seq, p], 0], k_buf.at[slot(p)], sem_k.at[slot(p)])

    def v_copy(seq, p):
        return pltpu.make_async_copy(cv_hbm.at[pt_ref[seq, p], 0], v_buf.at[slot(p)], sem_v.at[slot(p)])

    def start_ki(seq):
        def body(p, carry):
            ki_copy(seq, p).start()
            return carry
        lax.fori_loop(0, npages, body, 0)

    @pl.when(step == 0)
    def _():
        for s in range(nseq):
            start_ki(jnp.int32(s))
        for p in range(depth):
            k_copy(0, p).start()
            v_copy(0, p).start()

    @pl.when(step + 1 < nsteps)
    def _():
        for s in range(nseq):
            start_ki((step + 1) * nseq + s)

    def scores(s, kidx_t):
        x = _dot(qi_ref[s], kidx_t)
        xw = jnp.maximum(x, 0.0) * wrep_ref[s]
        sc = xw[0:t]
        for h in range(1, IDX_HEADS):
            sc = sc + xw[h * t:(h + 1) * t]
        return sc

    for s in range(nseq):
        sq = step * nseq + s

        def score_pages(i, carry, s=s, sq=sq):
            for g in range(group):
                ki_copy(sq, i * group + g).wait()
            sc = [scores(s, ki_buf[lax.rem(sq, 2 * nseq), i * group + g].astype(BF16)) for g in range(group)]
            score_scr[s * t:(s + 1) * t, pl.ds(pl.multiple_of(i * group * page, group * page), group * page)] = (
                jnp.concatenate(sc, axis=1))
            return carry

        lax.fori_loop(0, npages // group, score_pages, 0)
        score_scr[s * t:(s + 1) * t, npages * page:] = jnp.where(lane <= qrow, scores(s, kin_ref[s]), -jnp.inf)

    def count(pred):
        hit = jnp.where(pred(score_scr[...]), 1, 0)
        part = hit[:, 0:page]
        for c in range(1, npages + 1):
            part = part + hit[:, c * page:(c + 1) * page]
        return jnp.sum(part, axis=1, keepdims=True)

    thr_all, nge = _kth_largest(lambda x: count(lambda sc: sc >= x), topk, (rows, 1), two_bits=True)
    live = thr_all > KEY_NEG_INF
    thr_all = _key_to_float(jnp.maximum(thr_all, KEY_NEG_INF + 1))
    has_ties = jnp.max(jnp.where(live & (nge > topk), 1, 0)) > 0
    pos_all = lax.broadcasted_iota(I32, (rows, total), 1)

    def tie_cut():
        need = topk - count(lambda sc: sc > thr_all)
        nbits = max(1, int(math.ceil(math.log2(total + 1))))

        def pos_step(j, pos):
            cand = pos | jnp.left_shift(jnp.int32(1), nbits - 1 - j)
            below = count(lambda sc: (sc == thr_all) & (pos_all < cand))
            return jnp.where(below <= need, cand, pos)

        return lax.fori_loop(0, nbits, pos_step, jnp.zeros((rows, 1), I32))

    cut_all = lax.cond(has_ties, tie_cut, lambda: jnp.full((rows, 1), total, I32))

    for s in range(nseq):
        _dsa_sample_attend(s, step * nseq + s, nsteps * nseq, thr_all[s * t:(s + 1) * t], cut_all[s * t:(s + 1) * t],
                           qbd_ref, kn_ref, vn_ref, gb_ref, o_ref, k_buf, v_buf, score_scr, k_copy, v_copy, slot,
                           t=t, npages=npages, depth=depth, group=group, page=page)


def _dsa_sample_attend(s, sq, nseqs, thr, cut, qbd_ref, kn_ref, vn_ref, gb_ref, o_ref, k_buf, v_buf, score_scr,
                       k_copy, v_copy, slot, *, t, npages, depth, group, page):
    lane = lax.broadcasted_iota(I32, (t, page), 1)
    qbd = qbd_ref[s]
    b, nb = sq, nseqs

    def masked_logits(k_pages):
        lgs = []
        for k_t, ks in k_pages:
            tile = score_scr[s * t:(s + 1) * t, pl.ds(ks, page)]
            sel = (tile > thr) | ((tile == thr) & (lane + ks < cut))
            sel = jnp.concatenate([sel.astype(I32)] * B_HEADS, axis=0) > 0
            lgs.append(jnp.where(sel, _dot(qbd, k_t), -jnp.inf))
        return jnp.concatenate(lgs, axis=1)

    def accumulate(lg, v_pages, carry):
        m_old, l_old, acc = carry
        m_new = jnp.maximum(m_old, jnp.max(lg, axis=1, keepdims=True))
        m_safe = jnp.where(m_new == -jnp.inf, 0.0, m_new)
        alpha = jnp.exp(m_old - m_safe)
        e = jnp.exp(lg - m_safe)
        l_new = alpha * l_old + jnp.sum(e, axis=1, keepdims=True)
        acc = alpha * acc
        for g, v_t in enumerate(v_pages):
            acc = acc + _dot_nt(e[:, g * page:(g + 1) * page].astype(BF16), v_t)
        return m_new, l_new, acc

    def group_pages(i):
        return [i * group + g for g in range(group)]

    def k_operands(i):
        return [(k_buf[slot(p)].astype(BF16), pl.multiple_of(p * page, page)) for p in group_pages(i)]

    def v_operands(i):
        return [v_buf[slot(p)].astype(BF16) for p in group_pages(i)]

    def refill(copy, i):
        for p in group_pages(i):
            @pl.when(p + depth < npages)
            def _():
                copy(b, p + depth).start()

            @pl.when((p + depth >= npages) & (b + 1 < nb))
            def _():
                copy(b + 1, p + depth - npages).start()

    ngroups = npages // group
    for p in group_pages(0):
        k_copy(b, p).wait()
    lg_first = masked_logits(k_operands(0))
    refill(k_copy, 0)

    def attend_groups(i, carry):
        lg, state = carry
        for p in group_pages(i + 1):
            k_copy(b, p).wait()
        for p in group_pages(i):
            v_copy(b, p).wait()
        lg_next = masked_logits(k_operands(i + 1))
        state = accumulate(lg, v_operands(i), state)
        refill(k_copy, i + 1)
        refill(v_copy, i)
        return lg_next, state

    nrow = B_HEADS * t
    state = (jnp.full((nrow, 1), -jnp.inf, F32), jnp.zeros((nrow, 1), F32), jnp.zeros((nrow, B_WIDTH), F32))
    lg_last, state = lax.fori_loop(0, ngroups - 1, attend_groups, (lg_first, state))
    for p in group_pages(ngroups - 1):
        v_copy(b, p).wait()
    state = accumulate(lg_last, v_operands(ngroups - 1), state)
    refill(v_copy, ngroups - 1)
    _, den, acc = accumulate(masked_logits([(kn_ref[s], npages * page)]), [vn_ref[s]], state)
    o = acc / den
    col = lax.broadcasted_iota(I32, (t, B_WIDTH), 1)
    out = jnp.zeros((t, B_WIDTH), F32)
    for h in range(B_HEADS):
        out = jnp.where((col >= h * B_DH) & (col < (h + 1) * B_DH), o[h * t:(h + 1) * t], out)
    o_ref[s] = (out * gb_ref[s].astype(F32)).astype(BF16)


def _dsa_sample(page_table, qi_rows, wrep, qbd, kin_t, kn_t, vn_t, gbs, cache_kidx_t, cache_k_t, cache_v_t, t):
    bd, npages = page_table.shape
    page = cache_kidx_t.shape[3]
    total = npages * page + t
    topk = min(TOPK_MAX, total // 4)
    group = max(g for g in range(1, SAMPLE_PAGE_GROUP + 1) if npages % g == 0)
    depth = max(d for d in range(group, min(SAMPLE_DMA_DEPTH, npages) + 1, group) if npages % d == 0)
    nseq = SAMPLE_SEQS_PER_STEP if bd % SAMPLE_SEQS_PER_STEP == 0 else 1
    per_b = lambda b, pt: (b, 0, 0)
    hbm = pl.BlockSpec(memory_space=pl.ANY)
    grid_spec = pltpu.PrefetchScalarGridSpec(
        num_scalar_prefetch=1, grid=(bd // nseq,),
        in_specs=[
            pl.BlockSpec((nseq, IDX_HEADS * t, IDX_DIM), per_b),
            pl.BlockSpec((nseq, IDX_HEADS * t, page), per_b),
            pl.BlockSpec((nseq, B_HEADS * t, B_WIDTH), per_b),
            pl.BlockSpec((nseq, IDX_DIM, page), per_b),
            pl.BlockSpec((nseq, B_WIDTH, page), per_b),
            pl.BlockSpec((nseq, B_WIDTH, page), per_b),
            pl.BlockSpec((nseq, t, B_WIDTH), per_b),
            hbm, hbm, hbm,
        ],
        out_specs=pl.BlockSpec((nseq, t, B_WIDTH), per_b),
        scratch_shapes=[
            pltpu.VMEM((2 * nseq, npages, IDX_DIM, page), F32),
            pltpu.VMEM((depth, B_WIDTH, page), F32),
            pltpu.VMEM((depth, B_WIDTH, page), F32),
            pltpu.VMEM((nseq * t, (npages + 1) * page), F32),
            pltpu.SemaphoreType.DMA((2 * nseq * npages,)),
            pltpu.SemaphoreType.DMA((depth,)),
            pltpu.SemaphoreType.DMA((depth,)),
        ],
    )
    return pl.pallas_call(
        functools.partial(_dsa_sample_kernel, t=t, npages=npages, topk=topk, depth=depth, group=group, nseq=nseq),
        grid_spec=grid_spec,
        out_shape=jax.ShapeDtypeStruct((bd, t, B_WIDTH), BF16),
        compiler_params=pltpu.CompilerParams(dimension_semantics=("arbitrary",), vmem_limit_bytes=VMEM_LIMIT),
        name="dsa_sample",
    )(page_table, qi_rows, wrep, qbd, kin_t, kn_t, vn_t, gbs, cache_kidx_t, cache_k_t, cache_v_t)


def _merge_kernel(x_ref, ma_ref, mb_ref, w_ref, g_ref, b_ref, y_ref, *, alpha):
    mix = jnp.concatenate([ma_ref[h] for h in range(A_HEADS)] + [mb_ref[...]], axis=1)
    y = alpha * x_ref[...] + _dot(mix, w_ref[...])
    mu = jnp.mean(y, axis=-1, keepdims=True)
    d = y - mu
    var = jnp.mean(d * d, axis=-1, keepdims=True)
    y_ref[...] = d * lax.rsqrt(var + LN_EPS) * g_ref[...] + b_ref[...]


def _merge(x, mix_a, mix_b, w_out, ln_g, ln_b, alpha, tm):
    n, dm = x.shape
    return pl.pallas_call(
        functools.partial(_merge_kernel, alpha=alpha), grid=(n // tm,),
        in_specs=[
            pl.BlockSpec((tm, dm), lambda i: (i, 0)),
            pl.BlockSpec((A_HEADS, tm, A_DV), lambda i: (0, i, 0)),
            pl.BlockSpec((tm, B_WIDTH), lambda i: (i, 0)),
            pl.BlockSpec((A_WIDTH + B_WIDTH, dm), lambda i: (0, 0)),
            pl.BlockSpec((1, dm), lambda i: (0, 0)),
            pl.BlockSpec((1, dm), lambda i: (0, 0)),
        ],
        out_specs=pl.BlockSpec((tm, dm), lambda i: (i, 0)),
        out_shape=jax.ShapeDtypeStruct((n, dm), F32),
        compiler_params=pltpu.CompilerParams(dimension_semantics=("parallel",), vmem_limit_bytes=VMEM_LIMIT),
        name="merge",
    )(x, mix_a, mix_b, w_out, ln_g, ln_b)


def _split_weights(w_in_l):
    offs = np.cumsum([0, 512, 512, 512, 512, 512, 512, 512, 512, IDX_HEADS * IDX_DIM, IDX_DIM, IDX_HEADS])
    col = lambda i: w_in_l[:, offs[i]:offs[i + 1]]
    qa, fa, ia, ga, qb, kb, vb, gb, qi, ki, wi = (col(i) for i in range(11))
    pad = jnp.zeros((w_in_l.shape[0], LANES - IDX_DIM), w_in_l.dtype)
    wn = jnp.concatenate([qa, fa, ia, ga, kb, gb, ki, pad], axis=1).astype(BF16)
    wt = jnp.concatenate([qb, kb, vb, qi, wi, ki], axis=1).T.astype(BF16)
    return wn, wt


def _layer(xp, xs, cache_k, cache_v, cache_kidx, s0_sample, page_table, w_in_l, lb_l, norm_g_l, kn_g_l, kn_b_l,
           w_out_l, ln_g_l, ln_b_l, alpha):
    b, l, dm = xp.shape
    bd, t, _ = xs.shape
    npages, page = page_table.shape[1], cache_k.shape[2]
    assert page == Q_BLOCK and l % page == 0
    past = npages * page
    wn, wt = _split_weights(w_in_l)
    lb = lb_l.reshape(1, A_WIDTH)
    ng = norm_g_l.reshape(1, A_DV)
    w_out_b = w_out_l.astype(BF16)
    lng, lnb = ln_g_l.reshape(1, dm), ln_b_l.reshape(1, dm)

    tm = 256
    xp2 = xp.reshape(b * l, dm)
    (hq, hk, hg, hv, hgate, kbf, gbs, kibf, qT, vT, qiT2, wT, k_p, v_p, ki_p) = _project(
        xp2, jnp.arange(l, dtype=I32), wn, wt, lb, kn_g_l, kn_b_l, tm)
    mix_a, s_p = _hgrn_prompt(hq, hk, hg, hv, hgate, ng, b, l, min(l, 512))
    mix_b = _dsa_prompt(kibf, qiT2, wT, kbf, qT, vT, gbs, b, l)
    y_p = _merge(xp2, mix_a, mix_b, w_out_b, lng, lnb, alpha, 512).reshape(b, l, dm)

    ns = bd * t
    xs2 = xs.reshape(ns, dm)
    pos_s = past + (jnp.arange(ns, dtype=I32) % t)
    (hq, hk, hg, hv, hgate, kbf, gbs, kibf, qT, vT, qiT2, wT, k_s, v_s, ki_s) = _project(
        xs2, pos_s, wn, wt, lb, kn_g_l, kn_b_l, ns)
    mix_a, s_s = _hgrn_sample(hq, hk, hg, hv, hgate, ng, s0_sample, t)
    qi_nat = qiT2.reshape(ns // Q_BLOCK, IDX_DIM, IDX_HEADS, Q_BLOCK).transpose(0, 3, 2, 1)
    qi_rows = qi_nat.reshape(bd, t, IDX_HEADS, IDX_DIM).transpose(0, 2, 1, 3).reshape(bd, IDX_HEADS * t, IDX_DIM)
    w_rows = wT.T.reshape(bd, t, IDX_HEADS).transpose(0, 2, 1).reshape(bd, IDX_HEADS * t, 1)
    wrep = jnp.broadcast_to(w_rows, (bd, IDX_HEADS * t, page))
    q_nat = qT.T.reshape(bd, 1, t, B_HEADS, B_DH)
    eye = jnp.eye(B_HEADS, dtype=BF16).reshape(1, B_HEADS, 1, B_HEADS, 1)
    qbd = (q_nat * eye).reshape(bd, B_HEADS * t, B_WIDTH)

    def per_seq(pages):
        feat = pages.shape[1]
        a = pages.transpose(1, 0, 2).reshape(feat, bd, t).transpose(1, 0, 2).astype(BF16)
        return jnp.concatenate([a, jnp.zeros((bd, feat, page - t), BF16)], axis=2)

    token_minor = lambda c: jnp.moveaxis(c, 2, -1).reshape(c.shape[0], c.shape[1], -1, page)
    mix_b = _dsa_sample(page_table, qi_rows, wrep, qbd, per_seq(ki_s), per_seq(k_s), per_seq(v_s),
                        gbs.reshape(bd, t, B_WIDTH), token_minor(cache_kidx), token_minor(cache_k),
                        token_minor(cache_v), t).reshape(ns, B_WIDTH)
    y_s = _merge(xs2, mix_a, mix_b, w_out_b, lng, lnb, alpha, ns).reshape(bd, t, dm)

    return (y_p, y_s, k_p, v_p, ki_p, s_p, k_s, v_s, ki_s, s_s)


def kernel(x_prompt, x_sample, cache_k, cache_v, cache_kidx, state_hgrn, page_table, w_in, hgrn_lb_logits,
           hgrn_norm_g, idx_norm_g, idx_norm_b, w_out, ln_g, ln_b):
    depth = w_in.shape[0]
    assert depth == 1, "one layer per step"
    b, l, _ = x_prompt.shape
    bd, t, _ = x_sample.shape
    page = cache_k.shape[2]
    alpha = (2.0 * depth) ** 0.25
    lbs = jnp.cumsum(jax.nn.softmax(hgrn_lb_logits.astype(F32), axis=0), axis=0)[:depth]
    (y_p, y_s, k_p, v_p, ki_p, s_p, k_s, v_s, ki_s, s_s) = _layer(
        x_prompt, x_sample, cache_k, cache_v, cache_kidx, state_hgrn[0], page_table, w_in[0], lbs[0],
        hgrn_norm_g[0], idx_norm_g[0], idx_norm_b[0], w_out[0], ln_g[0], ln_b[0], alpha)
    nat_p = lambda pg, *f: jnp.moveaxis(pg.reshape(b, l // page, 1, *f, page), -1, 3)
    nat_s = lambda pg, *f: pg.transpose(0, 2, 1).reshape(bd, 1, t, *f)
    return (
        y_p, y_s,
        nat_p(k_p, B_HEADS, B_DH), nat_p(v_p, B_HEADS, B_DH), nat_p(ki_p, IDX_DIM),
        s_p[None],
        nat_s(k_s, B_HEADS, B_DH), nat_s(v_s, B_HEADS, B_DH), nat_s(ki_s, IDX_DIM),
        s_s[None],
    )
```

```python
import functools
import math

import jax
import jax.numpy as jnp
import numpy as np
from jax import lax
from jax.experimental import pallas as pl
from jax.experimental.pallas import tpu as pltpu

F32 = jnp.float32
BF16 = jnp.bfloat16
I32 = jnp.int32

A_HEADS = 4
A_DK = 128
A_DV = 128
B_HEADS = 8
B_DH = 64
IDX_HEADS = 16
IDX_DIM = 64
TOPK_MAX = 256
ROPE_THETA = 500000.0
ROT = 16
ROT_HALF = ROT // 2
LN_EPS = 1e-5
Q_BLOCK = 128
K_TILE = 256
ATT_TILE = 512
HGRN_CHUNK = 128
HGRN_SUB = 16
LANES = 128
VMEM_LIMIT = 56 * 1024 * 1024
SAMPLE_DMA_DEPTH = 32
SAMPLE_PAGE_GROUP = 8
SAMPLE_SEQS_PER_STEP = 1

A_WIDTH = A_HEADS * A_DV
B_WIDTH = B_HEADS * B_DH
NAT_WIDTH = 6 * 512 + LANES
TR_ROWS = 3 * 512 + IDX_HEADS * IDX_DIM + IDX_HEADS + IDX_DIM
INT_MIN = -(2 ** 31)
KEY_NEG_INF = -0x7F800000


def _dot(a, b):
    return jnp.dot(a, b, preferred_element_type=F32)


def _dot_nt(a, b):
    return lax.dot_general(a, b, (((1,), (1,)), ((), ())), preferred_element_type=F32)


def _dot_tn(a, b):
    return lax.dot_general(a, b, (((0,), (0,)), ((), ())), preferred_element_type=F32)


def _silu(x):
    return x * jax.nn.sigmoid(x)


def _key_to_float(key):
    m = key >> 31
    mag = (key ^ m) - m
    return pltpu.bitcast(mag | (m & INT_MIN), F32)


def _proj_kernel(x_ref, wn_ref, wt_ref, lb_ref, kng_ref, knb_ref, kngc_ref, knbc_ref, cn_ref, sa_ref, sb_ref,
                 ct_ref, st_ref,
                 hq_ref, hk_ref, hg_ref, hv_ref, hgate_ref, kbf_ref, gb_ref, kibf_ref,
                 qT_ref, vT_ref, qiT_ref, wT_ref, kTp_ref, vTp_ref, kiTp_ref):
    tm = x_ref.shape[0]
    xb = x_ref[...].astype(BF16)

    def nat(col, width=512):
        return _dot(xb, wn_ref[:, col:col + width])

    lb = lb_ref[...]
    qa = nat(0)
    hq = _silu(qa)
    fa = nat(512)
    hg = jnp.log2(lb + (1.0 - lb) * jax.nn.sigmoid(fa))
    hk = (1.0 - lb) * jax.nn.sigmoid(-fa)
    ia = nat(1024)
    hgate = _silu(nat(1536))
    for h in range(A_HEADS):
        sl = slice(h * A_DK, (h + 1) * A_DK)
        hq_ref[h] = hq[:, sl].astype(BF16)
        hk_ref[h] = hk[:, sl]
        hg_ref[h] = hg[:, sl]
        hv_ref[h] = ia[:, sl].astype(BF16)
        hgate_ref[h] = hgate[:, sl].astype(BF16)

    cn, sa, sb = cn_ref[...], sa_ref[...], sb_ref[...]

    def rope_nat(xc):
        return xc * cn + pltpu.roll(xc, LANES - ROT_HALF, 1) * sa + pltpu.roll(xc, ROT_HALF, 1) * sb

    kb = nat(2048)
    kbf_ref[...] = jnp.concatenate(
        [rope_nat(kb[:, c * LANES:(c + 1) * LANES]) for c in range(B_WIDTH // LANES)], axis=1).astype(BF16)
    gb_ref[...] = _silu(nat(2560)).astype(BF16)

    kic = nat(3072, LANES)
    lane = lax.broadcasted_iota(I32, (tm, LANES), 1)
    inb = lane < IDX_DIM
    mu = jnp.sum(jnp.where(inb, kic, 0.0), axis=-1, keepdims=True) * (1.0 / IDX_DIM)
    d = jnp.where(inb, kic - mu, 0.0)
    var = jnp.sum(d * d, axis=-1, keepdims=True) * (1.0 / IDX_DIM)
    kin = d * lax.rsqrt(var + LN_EPS) * kng_ref[...] + knb_ref[...]
    kibf_ref[...] = rope_nat(kin)[:, :IDX_DIM].astype(BF16)

    ct, st = ct_ref[...], st_ref[...]
    npage = tm // Q_BLOCK

    def tr(row, height):
        return _dot_nt(wt_ref[row:row + height, :], xb)

    def rope_tr(blk):
        x1, x2 = blk[0:ROT_HALF], blk[ROT_HALF:ROT]
        return jnp.concatenate([x1 * ct - x2 * st, x1 * st + x2 * ct, blk[ROT:]], axis=0)

    qbT = tr(0, B_WIDTH)
    for h in range(B_HEADS):
        blk = rope_tr(qbT[h * B_DH:(h + 1) * B_DH])
        qT_ref[h * B_DH:(h + 1) * B_DH, :] = (blk * (B_DH ** -0.5)).astype(BF16)
    kbT = tr(B_WIDTH, B_WIDTH)
    for h in range(B_HEADS):
        blk = rope_tr(kbT[h * B_DH:(h + 1) * B_DH])
        for pg in range(npage):
            kTp_ref[pg, h * B_DH:(h + 1) * B_DH, :] = blk[:, pg * Q_BLOCK:(pg + 1) * Q_BLOCK]
    vbT = tr(2 * B_WIDTH, B_WIDTH)
    vT_ref[...] = vbT.astype(BF16)
    for pg in range(npage):
        vTp_ref[pg] = vbT[:, pg * Q_BLOCK:(pg + 1) * Q_BLOCK]
    row = 3 * B_WIDTH
    qiT = tr(row, IDX_HEADS * IDX_DIM)
    for h in range(IDX_HEADS):
        blk = rope_tr(qiT[h * IDX_DIM:(h + 1) * IDX_DIM]).astype(BF16)
        for pg in range(npage):
            qiT_ref[pg, :, h * Q_BLOCK:(h + 1) * Q_BLOCK] = blk[:, pg * Q_BLOCK:(pg + 1) * Q_BLOCK]
    row += IDX_HEADS * IDX_DIM
    wT_ref[...] = tr(row, IDX_HEADS) * (IDX_HEADS ** -0.5 * IDX_DIM ** -0.5)
    row += IDX_HEADS
    kiT = tr(row, IDX_DIM)
    muT = jnp.mean(kiT, axis=0, keepdims=True)
    dT = kiT - muT
    varT = jnp.mean(dT * dT, axis=0, keepdims=True)
    kiT = rope_tr(dT * lax.rsqrt(varT + LN_EPS) * kngc_ref[...] + knbc_ref[...])
    for pg in range(npage):
        kiTp_ref[pg] = kiT[:, pg * Q_BLOCK:(pg + 1) * Q_BLOCK]


def _rope_tables(pos):
    inv = ROPE_THETA ** (-jnp.arange(ROT_HALF, dtype=F32) / ROT_HALF)
    ang = pos.astype(F32)[:, None] * inv[None, :]
    cos, sin = jnp.cos(ang), jnp.sin(ang)
    p = pos.shape[0]
    one = jnp.ones((p, B_DH - ROT), F32)
    zero8 = jnp.zeros((p, ROT_HALF), F32)
    zero = jnp.zeros((p, B_DH - ROT), F32)
    cn = jnp.concatenate([cos, cos, one], axis=1)
    sa = jnp.concatenate([-sin, zero8, zero], axis=1)
    sb = jnp.concatenate([zero8, sin, zero], axis=1)
    tile2 = lambda a: jnp.concatenate([a, a], axis=1)
    return tile2(cn), tile2(sa), tile2(sb), cos.T, sin.T


def _project(x, pos, wn, wt, lb, kn_g, kn_b, tm):
    n, dm = x.shape
    p = pos.shape[0]
    nper = p // tm
    cn, sa, sb, ct, st = _rope_tables(pos)
    padl = lambda a: jnp.concatenate([a, jnp.zeros((LANES - IDX_DIM,), a.dtype)]).reshape(1, LANES)
    grid = (n // tm,)
    row = lambda i: (i, 0)
    full = lambda i: (0, 0)
    head = lambda i: (0, i, 0)
    page = lambda i: (i, 0, 0)
    per = lambda i: (i % nper, 0)
    perT = lambda i: (0, i % nper)
    colT = lambda i: (0, i)
    in_specs = [
        pl.BlockSpec((tm, dm), row),
        pl.BlockSpec((dm, NAT_WIDTH), full, pipeline_mode=pl.Buffered(1)),
        pl.BlockSpec((TR_ROWS, dm), full, pipeline_mode=pl.Buffered(1)),
        pl.BlockSpec((1, A_WIDTH), full),
        pl.BlockSpec((1, LANES), full),
        pl.BlockSpec((1, LANES), full),
        pl.BlockSpec((IDX_DIM, 1), full),
        pl.BlockSpec((IDX_DIM, 1), full),
        pl.BlockSpec((tm, LANES), per),
        pl.BlockSpec((tm, LANES), per),
        pl.BlockSpec((tm, LANES), per),
        pl.BlockSpec((ROT_HALF, tm), perT),
        pl.BlockSpec((ROT_HALF, tm), perT),
    ]
    npage = tm // Q_BLOCK
    hshape = lambda dt: jax.ShapeDtypeStruct((A_HEADS, n, A_DK), dt)
    out_shape = [
        hshape(BF16), hshape(F32), hshape(F32), hshape(BF16), hshape(BF16),
        jax.ShapeDtypeStruct((n, B_WIDTH), BF16), jax.ShapeDtypeStruct((n, B_WIDTH), BF16),
        jax.ShapeDtypeStruct((n, IDX_DIM), BF16),
        jax.ShapeDtypeStruct((B_WIDTH, n), BF16), jax.ShapeDtypeStruct((B_WIDTH, n), BF16),
        jax.ShapeDtypeStruct((n // Q_BLOCK, IDX_DIM, IDX_HEADS * Q_BLOCK), BF16),
        jax.ShapeDtypeStruct((IDX_HEADS, n), F32),
        jax.ShapeDtypeStruct((n // Q_BLOCK, B_WIDTH, Q_BLOCK), F32),
        jax.ShapeDtypeStruct((n // Q_BLOCK, B_WIDTH, Q_BLOCK), F32),
        jax.ShapeDtypeStruct((n // Q_BLOCK, IDX_DIM, Q_BLOCK), F32),
    ]
    hspec = pl.BlockSpec((A_HEADS, tm, A_DK), head)
    out_specs = [
        hspec, hspec, hspec, hspec, hspec,
        pl.BlockSpec((tm, B_WIDTH), row), pl.BlockSpec((tm, B_WIDTH), row),
        pl.BlockSpec((tm, IDX_DIM), row),
        pl.BlockSpec((B_WIDTH, tm), colT), pl.BlockSpec((B_WIDTH, tm), colT),
        pl.BlockSpec((npage, IDX_DIM, IDX_HEADS * Q_BLOCK), page),
        pl.BlockSpec((IDX_HEADS, tm), colT),
        pl.BlockSpec((npage, B_WIDTH, Q_BLOCK), page),
        pl.BlockSpec((npage, B_WIDTH, Q_BLOCK), page),
        pl.BlockSpec((npage, IDX_DIM, Q_BLOCK), page),
    ]
    return pl.pallas_call(
        _proj_kernel, grid=grid, in_specs=in_specs, out_specs=out_specs, out_shape=out_shape,
        compiler_params=pltpu.CompilerParams(dimension_semantics=("parallel",), vmem_limit_bytes=VMEM_LIMIT),
        name="proj",
    )(x, wn, wt, lb, padl(kn_g), padl(kn_b), kn_g.reshape(IDX_DIM, 1), kn_b.reshape(IDX_DIM, 1),
      cn, sa, sb, ct, st)


def _hgrn_cumsum(g, c):
    if c == LANES:
        ri = lax.broadcasted_iota(I32, (c, c), 0)
        ci = lax.broadcasted_iota(I32, (c, c), 1)
        tri = jnp.where(ci <= ri, 1.0, 0.0).astype(BF16)
        g1 = g.astype(BF16)
        e1 = g - g1.astype(F32)
        g2 = e1.astype(BF16)
        g3 = (e1 - g2.astype(F32)).astype(BF16)
        return _dot(tri, g1) + _dot(tri, g2) + _dot(tri, g3)
    ri = lax.broadcasted_iota(I32, (c, A_DK), 0)
    cum = jnp.zeros((c, A_DK), F32)
    for s in range(c):
        cum = cum + jnp.where(ri >= s, g[s:s + 1], 0.0)
    return cum


def _hgrn_chunks(chains, ng, c):
    r = HGRN_SUB
    nsub = c // r
    rowi = lax.broadcasted_iota(I32, (r, A_DK), 0)
    lane = lax.broadcasted_iota(I32, (r, LANES), 1)
    cums = [_hgrn_cumsum(g, c) for _, _, g, _, _, _ in chains]

    light = []
    for (q, k, _, v, _, st), cum in zip(chains, cums):
        o_state = _dot_nt((q * jnp.exp2(cum)).astype(BF16), st.astype(BF16))
        off = [None]
        for i in range(1, nsub):
            ref = cum[i * r - 1:i * r]
            qt = (q[i * r:(i + 1) * r] * jnp.exp2(cum[i * r:(i + 1) * r] - ref)).astype(BF16)
            kt = (k[:i * r] * jnp.exp2(ref - cum[:i * r])).astype(BF16)
            kt = jnp.concatenate([kt, jnp.zeros((LANES - i * r, A_DK), BF16)], axis=0)
            off.append(_dot_nt(qt, kt))
        vb = v.astype(BF16)
        last = cum[c - 1:c]
        kh = (k * jnp.exp2(last - cum)).astype(BF16)
        if c < LANES:
            zpad = jnp.zeros((LANES - c, A_DK), BF16)
            vb = jnp.concatenate([vb, zpad], axis=0)
            kh = jnp.concatenate([kh, zpad], axis=0)
        st_new = st * jnp.exp2(last) + _dot_tn(vb, kh)
        light.append((o_state, off, vb, st_new))

    rsums = []
    for (q, k, _, _, _, _), cum in zip(chains, cums):
        prods = []
        for i in range(nsub):
            qs, cs = q[i * r:(i + 1) * r], cum[i * r:(i + 1) * r]
            for s in range(r):
                row = i * r + s
                p = (qs * k[row:row + 1]) * jnp.exp2(cs - cum[row:row + 1])
                prods.append(jnp.where(rowi >= s, p, 0.0).astype(BF16))
        if nsub % 2 == 0:
            half = len(prods) // 2
            both = jnp.concatenate([jnp.concatenate(prods[:half], axis=0), jnp.concatenate(prods[half:], axis=0)],
                                   axis=1)
            wr = lax.broadcasted_iota(I32, (2 * A_DK, 2 * LANES), 0) < A_DK
            wc = lax.broadcasted_iota(I32, (2 * A_DK, 2 * LANES), 1) < LANES
            sums = _dot(both, jnp.where(wr == wc, 1.0, 0.0).astype(BF16))
            rsums.append(lambda row, sums=sums, half=half: (
                sums[row * r:(row + 1) * r, :LANES] if row < half
                else sums[(row - half) * r:(row - half + 1) * r, LANES:]))
        else:
            sums = _dot(jnp.concatenate(prods, axis=0), jnp.ones((A_DK, LANES), BF16))
            rsums.append(lambda row, sums=sums: sums[row * r:(row + 1) * r])

    results = []
    for (_, _, _, _, gate, _), (o_state, off, vb, st_new), rsum in zip(chains, light, rsums):
        sc_rows = []
        for i in range(nsub):
            sci = jnp.zeros((r, LANES), F32)
            for s in range(r):
                row = i * r + s
                sci = jnp.where(lane == row, rsum(row), sci)
            sc_rows.append(sci if i == 0 else sci + off[i])
        sc = jnp.concatenate(sc_rows, axis=0).astype(BF16)
        o = o_state + _dot(sc, vb)
        ms = jnp.mean(o * o, axis=-1, keepdims=True)
        results.append((o * lax.rsqrt(ms + LN_EPS) * ng * gate, st_new))
    return results


def _hgrn_prompt_kernel(q_ref, k_ref, g_ref, v_ref, gate_ref, ng_ref, o_ref, sf_ref, st_scr):
    c = HGRN_CHUNK
    nchunk = q_ref.shape[1] // c
    tb = pl.program_id(1)

    @pl.when(tb == 0)
    def _():
        st_scr[...] = jnp.zeros_like(st_scr)

    ng = ng_ref[...]

    def body(j, carry):
        sl = pl.ds(pl.multiple_of(j * c, c), c)
        loaded = [(q_ref[h, sl, :].astype(F32), k_ref[h, sl, :], g_ref[h, sl, :], v_ref[h, sl, :].astype(F32),
                   gate_ref[h, sl, :].astype(F32), st_scr[h]) for h in range(A_HEADS)]
        for h, (o, st_new) in enumerate(_hgrn_chunks(loaded, ng, c)):
            st_scr[h] = st_new
            o_ref[h, sl, :] = o.astype(BF16)
        return carry

    lax.fori_loop(0, nchunk, body, 0)

    @pl.when(tb == pl.num_programs(1) - 1)
    def _():
        for h in range(A_HEADS):
            sf_ref[0, h] = st_scr[h].T


def _hgrn_prompt(hq, hk, hg, hv, hgate, ng, batch, seq, tb):
    n = batch * seq
    nt = seq // tb
    blk = pl.BlockSpec((A_HEADS, tb, A_DK), lambda b, t: (0, b * nt + t, 0))
    return pl.pallas_call(
        _hgrn_prompt_kernel, grid=(batch, nt),
        in_specs=[blk, blk, blk, blk, blk, pl.BlockSpec((1, A_DV), lambda b, t: (0, 0))],
        out_specs=[blk, pl.BlockSpec((1, A_HEADS, A_DK, A_DV), lambda b, t: (b, 0, 0, 0))],
        out_shape=[jax.ShapeDtypeStruct((A_HEADS, n, A_DV), BF16),
                   jax.ShapeDtypeStruct((batch, A_HEADS, A_DK, A_DV), F32)],
        scratch_shapes=[pltpu.VMEM((A_HEADS, A_DV, A_DK), F32)],
        compiler_params=pltpu.CompilerParams(dimension_semantics=("parallel", "arbitrary"),
                                             vmem_limit_bytes=VMEM_LIMIT),
        name="hgrn_prompt",
    )(hq, hk, hg, hv, hgate, ng)


def _hgrn_sample_kernel(q_ref, k_ref, g_ref, v_ref, gate_ref, ng_ref, s0_ref, o_ref, sf_ref, *, t, nb):
    c = HGRN_SUB
    ng = ng_ref[...]
    zpad = jnp.zeros((c - t, A_DK), F32)
    pad = lambda a: jnp.concatenate([a, zpad], axis=0)
    chains = []
    for h in range(A_HEADS):
        q, k, g = q_ref[h].astype(F32), k_ref[h], g_ref[h]
        v, gate = v_ref[h].astype(F32), gate_ref[h].astype(F32)
        for b in range(nb):
            sl = slice(b * t, (b + 1) * t)
            chains.append((pad(q[sl]), pad(k[sl]), pad(g[sl]), pad(v[sl]), pad(gate[sl]), s0_ref[b, h].T))
    results = _hgrn_chunks(chains, ng, c)
    for h in range(A_HEADS):
        for b in range(nb):
            sf_ref[b, h] = results[h * nb + b][1].T
        o_ref[h] = jnp.concatenate([results[h * nb + b][0][:t] for b in range(nb)], axis=0).astype(BF16)


def _hgrn_sample(hq, hk, hg, hv, hgate, ng, s0, t):
    bd = s0.shape[0]
    nb = 16 // t
    blk = pl.BlockSpec((A_HEADS, nb * t, A_DK), lambda i: (0, i, 0))
    sblk = pl.BlockSpec((nb, A_HEADS, A_DK, A_DV), lambda i: (i, 0, 0, 0))
    return pl.pallas_call(
        functools.partial(_hgrn_sample_kernel, t=t, nb=nb), grid=(bd // nb,),
        in_specs=[blk, blk, blk, blk, blk, pl.BlockSpec((1, A_DV), lambda i: (0, 0)), sblk],
        out_specs=[blk, sblk],
        out_shape=[jax.ShapeDtypeStruct((A_HEADS, bd * t, A_DV), BF16),
                   jax.ShapeDtypeStruct((bd, A_HEADS, A_DK, A_DV), F32)],
        compiler_params=pltpu.CompilerParams(dimension_semantics=("parallel",), vmem_limit_bytes=VMEM_LIMIT),
        name="hgrn_sample",
    )(hq, hk, hg, hv, hgate, ng, s0)


def _count_tiles(score_ref, ntiles, pred):
    def body(j, acc):
        ks = pl.multiple_of(j * ATT_TILE, ATT_TILE)
        hit = jnp.where(pred(score_ref[pl.ds(ks, ATT_TILE), :], ks), 1, 0)
        return acc + jnp.sum(hit.reshape(ATT_TILE // 8, 8, Q_BLOCK), axis=0)
    acc = lax.fori_loop(0, ntiles, body, jnp.zeros((8, Q_BLOCK), I32))
    return jnp.sum(acc, axis=0, keepdims=True)


def _kth_largest(count_ge, topk, shape, two_bits=False):
    c0 = count_ge(jnp.zeros(shape, F32))
    ok = c0 >= topk
    thr = jnp.where(ok, 0, INT_MIN)
    nge = jnp.where(ok, c0, 0)

    def place(bit, carry):
        thr, nge = carry
        cand = thr | bit
        cnt = count_ge(_key_to_float(cand))
        ok = cnt >= topk
        return jnp.where(ok, cand, thr), jnp.where(ok, cnt, nge)

    if not two_bits:
        return lax.fori_loop(0, 31, lambda i, c: place(jnp.left_shift(jnp.int32(1), 30 - i), c), (thr, nge))

    def place_two(i, carry):
        thr, nge = carry
        hi = jnp.left_shift(jnp.int32(1), 30 - 2 * i)
        lo = jnp.left_shift(jnp.int32(1), 29 - 2 * i)
        n_hi, n_lo, n_both = (count_ge(_key_to_float(thr | bits)) for bits in (hi, lo, hi | lo))
        ok_hi = n_hi >= topk
        thr1, nge1 = jnp.where(ok_hi, thr | hi, thr), jnp.where(ok_hi, n_hi, nge)
        n2 = jnp.where(ok_hi, n_both, n_lo)
        ok2 = n2 >= topk
        return jnp.where(ok2, thr1 | lo, thr1), jnp.where(ok2, n2, nge1)

    return place(jnp.int32(1), lax.fori_loop(0, 15, place_two, (thr, nge)))


def _dsa_prompt_kernel(ki_ref, qiT_ref, wT_ref, k_ref, qT_ref, vT_ref, gb_ref, o_ref,
                       score_scr, bias_scr, lg_scr, oT_scr, *, topk):
    i = pl.program_id(1)
    ntiles = (i * Q_BLOCK + Q_BLOCK + K_TILE - 1) // K_TILE
    qpos = i * Q_BLOCK + lax.broadcasted_iota(I32, (K_TILE, Q_BLOCK), 1)
    krow = lax.broadcasted_iota(I32, (K_TILE, Q_BLOCK), 0)

    qi = qiT_ref[0]
    wT = wT_ref[...]

    def score_body(j, carry):
        ks = pl.multiple_of(j * K_TILE, K_TILE)
        x = _dot(ki_ref[pl.ds(ks, K_TILE), :], qi)
        sc = jnp.zeros((K_TILE, Q_BLOCK), F32)
        for h in range(IDX_HEADS):
            sc = sc + jnp.maximum(x[:, h * Q_BLOCK:(h + 1) * Q_BLOCK], 0.0) * wT[h:h + 1]
        score_scr[pl.ds(ks, K_TILE), :] = jnp.where(krow + ks <= qpos, sc, -jnp.inf)
        return carry

    lax.fori_loop(0, ntiles, score_body, 0)

    natt = (i * Q_BLOCK + Q_BLOCK + ATT_TILE - 1) // ATT_TILE

    @pl.when(natt * (ATT_TILE // K_TILE) > ntiles)
    def _():
        score_scr[pl.ds(pl.multiple_of(ntiles * K_TILE, K_TILE), K_TILE), :] = jnp.full((K_TILE, Q_BLOCK), -jnp.inf, F32)

    count = functools.partial(_count_tiles, score_scr, natt)
    arow = lax.broadcasted_iota(I32, (ATT_TILE, Q_BLOCK), 0)
    thr, nge = lax.cond(
        (i + 1) * Q_BLOCK > topk,
        lambda: _kth_largest(lambda x: count(lambda tile, ks: tile >= x), topk, (1, Q_BLOCK)),
        lambda: (jnp.full((1, Q_BLOCK), KEY_NEG_INF, I32), jnp.zeros((1, Q_BLOCK), I32)))
    live = thr > KEY_NEG_INF
    thr = _key_to_float(jnp.maximum(thr, KEY_NEG_INF + 1))
    no_cut = jnp.full((1, Q_BLOCK), natt * ATT_TILE, I32)
    has_ties = jnp.max(jnp.where(live & (nge > topk), 1, 0)) > 0

    def tie_cut():
        need = topk - count(lambda tile, ks: tile > thr)
        nbits = max(1, int(math.ceil(math.log2(score_scr.shape[0] + 1))))

        def body(b, pos):
            cand = pos | jnp.left_shift(jnp.int32(1), nbits - 1 - b)
            below = count(lambda tile, ks: (tile == thr) & (arow + ks < cand))
            return jnp.where(below <= need, cand, pos)

        return lax.fori_loop(0, nbits, body, jnp.zeros((1, Q_BLOCK), I32))

    cut = lax.cond(has_ties, tie_cut, lambda: no_cut)

    def bias_body(j, carry):
        ks = pl.multiple_of(j * K_TILE, K_TILE)
        tile = score_scr[pl.ds(ks, K_TILE), :]
        sel = (tile > thr) | ((tile == thr) & (krow + ks < cut))
        bias_scr[pl.ds(ks, K_TILE), :] = jnp.where(sel, 0.0, -jnp.inf)
        return carry

    lax.fori_loop(0, natt * (ATT_TILE // K_TILE), bias_body, 0)

    npair = B_HEADS // 2
    z = jnp.zeros((B_DH, Q_BLOCK), BF16)
    rhs = []
    for p in range(npair):
        qp = qT_ref[p * LANES:(p + 1) * LANES, :]
        rhs.append(jnp.concatenate([jnp.concatenate([qp[:B_DH], z], axis=0),
                                    jnp.concatenate([z, qp[B_DH:]], axis=0)], axis=1))
    fold = lambda a: a.reshape(ATT_TILE // 8, 8, a.shape[1])

    def logit_body(j, mx):
        ks = pl.multiple_of(j * ATT_TILE, ATT_TILE)
        bias = bias_scr[pl.ds(ks, ATT_TILE), :]
        bias2 = jnp.concatenate([bias, bias], axis=1)
        lgs = [_dot(k_ref[pl.ds(ks, ATT_TILE), p * LANES:(p + 1) * LANES], rhs[p]) + bias2 for p in range(npair)]
        lg_scr[pl.ds(ks, ATT_TILE), :] = jnp.concatenate(lgs, axis=1)
        return tuple(jnp.maximum(m, jnp.max(fold(lg), axis=0)) for m, lg in zip(mx, lgs))

    mx = lax.fori_loop(0, natt, logit_body,
                       tuple(jnp.full((8, 2 * Q_BLOCK), -jnp.inf, F32) for _ in range(npair)))
    mx = jnp.concatenate([jnp.max(m, axis=0, keepdims=True) for m in mx], axis=1)
    oT_scr[...] = jnp.zeros_like(oT_scr)

    def value_body(j, den):
        ks = pl.multiple_of(j * ATT_TILE, ATT_TILE)
        new_den, new_acc = [], []
        for h in range(B_HEADS):
            cols = slice(h * Q_BLOCK, (h + 1) * Q_BLOCK)
            rows = slice(h * B_DH, (h + 1) * B_DH)
            e = jnp.exp(lg_scr[pl.ds(ks, ATT_TILE), cols] - mx[:, cols])
            new_den.append(den[h] + jnp.sum(fold(e), axis=0))
            new_acc.append(oT_scr[rows, :] + _dot(vT_ref[rows, pl.ds(ks, ATT_TILE)], e.astype(BF16)))
        for h in range(B_HEADS):
            oT_scr[h * B_DH:(h + 1) * B_DH, :] = new_acc[h]
        return tuple(new_den)

    den = lax.fori_loop(0, natt, value_body, tuple(jnp.zeros((8, Q_BLOCK), F32) for _ in range(B_HEADS)))
    for h in range(B_HEADS):
        rows = slice(h * B_DH, (h + 1) * B_DH)
        oT_scr[rows, :] = oT_scr[rows, :] / jnp.sum(den[h], axis=0, keepdims=True)

    o_ref[...] = (oT_scr[...].T * gb_ref[...].astype(F32)).astype(BF16)


def _dsa_prompt(kibf, qiT2, wT, kbf, qT, vT, gbs, batch, seq):
    n = batch * seq
    nq = seq // Q_BLOCK
    topk = min(TOPK_MAX, seq // 4)
    assert seq % ATT_TILE == 0 and topk <= K_TILE
    return pl.pallas_call(
        functools.partial(_dsa_prompt_kernel, topk=topk), grid=(batch, nq),
        in_specs=[
            pl.BlockSpec((seq, IDX_DIM), lambda b, i: (b, 0)),
            pl.BlockSpec((1, IDX_DIM, IDX_HEADS * Q_BLOCK), lambda b, i: (b * nq + i, 0, 0)),
            pl.BlockSpec((IDX_HEADS, Q_BLOCK), lambda b, i: (0, b * nq + i)),
            pl.BlockSpec((seq, B_WIDTH), lambda b, i: (b, 0)),
            pl.BlockSpec((B_WIDTH, Q_BLOCK), lambda b, i: (0, b * nq + i)),
            pl.BlockSpec((B_WIDTH, seq), lambda b, i: (0, b)),
            pl.BlockSpec((Q_BLOCK, B_WIDTH), lambda b, i: (b * nq + i, 0)),
        ],
        out_specs=pl.BlockSpec((Q_BLOCK, B_WIDTH), lambda b, i: (b * nq + i, 0)),
        out_shape=jax.ShapeDtypeStruct((n, B_WIDTH), BF16),
        scratch_shapes=[
            pltpu.VMEM((seq, Q_BLOCK), F32),
            pltpu.VMEM((seq, Q_BLOCK), F32),
            pltpu.VMEM((seq, B_HEADS * Q_BLOCK), F32),
            pltpu.VMEM((B_WIDTH, Q_BLOCK), F32),
        ],
        compiler_params=pltpu.CompilerParams(dimension_semantics=("parallel", "arbitrary"),
                                             vmem_limit_bytes=VMEM_LIMIT),
        name="dsa_prompt",
    )(kibf, qiT2, wT, kbf, qT, vT, gbs)


def _dsa_sample_kernel(pt_ref, qi_ref, wrep_ref, qbd_ref, kin_ref, kn_ref, vn_ref, gb_ref,
                       cki_hbm, ck_hbm, cv_hbm, o_ref, ki_buf, k_buf, v_buf, score_scr, sem_i, sem_k, sem_v,
                       *, t, npages, topk, depth, group, nseq):
    step = pl.program_id(0)
    nsteps = pl.num_programs(0)
    page = ki_buf.shape[3]
    total = (npages + 1) * page
    rows = nseq * t
    lane = lax.broadcasted_iota(I32, (t, page), 1)
    qrow = lax.broadcasted_iota(I32, (t, page), 0)

    def slot(p):
        return p % depth if isinstance(p, int) else lax.rem(p, depth)

    def ki_copy(seq, p):
        half = lax.rem(seq, 2 * nseq)
        return pltpu.make_async_copy(cki_hbm.at[pt_ref[seq, p], 0], ki_buf.at[half, p], sem_i.at[half * npages + p])

    def k_copy(seq, p):
        return pltpu.make_async_copy(ck_hbm.at[pt_ref[seq, p], 0], k_buf.at[slot(p)], sem_k.at[slot(p)])

    def v_copy(seq, p):
        return pltpu.make_async_copy(cv_hbm.at[pt_ref[seq, p], 0], v_buf.at[slot(p)], sem_v.at[slot(p)])

    def start_ki(seq):
        def body(p, carry):
            ki_copy(seq, p).start()
            return carry
        lax.fori_loop(0, npages, body, 0, unroll=math.gcd(npages, 8))

    @pl.when(step == 0)
    def _():
        for s in range(nseq):
            start_ki(jnp.int32(s))
        for p in range(depth):
            k_copy(0, p).start()
            v_copy(0, p).start()

    @pl.when(step + 1 < nsteps)
    def _():
        for s in range(nseq):
            start_ki((step + 1) * nseq + s)

    def scores(s, kidx_t):
        x = _dot(qi_ref[s], kidx_t)
        xw = jnp.maximum(x, 0.0) * wrep_ref[s]
        sc = xw[0:t]
        for h in range(1, IDX_HEADS):
            sc = sc + xw[h * t:(h + 1) * t]
        return sc

    for s in range(nseq):
        sq = step * nseq + s

        def score_pages(i, carry, s=s, sq=sq):
            for g in range(group):
                ki_copy(sq, i * group + g).wait()
            sc = [scores(s, ki_buf[lax.rem(sq, 2 * nseq), i * group + g].astype(BF16)) for g in range(group)]
            score_scr[s * t:(s + 1) * t, pl.ds(pl.multiple_of(i * group * page, group * page), group * page)] = (
                jnp.concatenate(sc, axis=1))
            return carry

        lax.fori_loop(0, npages // group, score_pages, 0)
        score_scr[s * t:(s + 1) * t, npages * page:] = jnp.where(lane <= qrow, scores(s, kin_ref[s]), -jnp.inf)

    def count(pred):
        hit = jnp.where(pred(score_scr[...]), 1, 0)
        part = hit[:, 0:page]
        for c in range(1, npages + 1):
            part = part + hit[:, c * page:(c + 1) * page]
        return jnp.sum(part, axis=1, keepdims=True)

    thr_all, nge = _kth_largest(lambda x: count(lambda sc: sc >= x), topk, (rows, 1), two_bits=True)
    live = thr_all > KEY_NEG_INF
    thr_all = _key_to_float(jnp.maximum(thr_all, KEY_NEG_INF + 1))
    has_ties = jnp.max(jnp.where(live & (nge > topk), 1, 0)) > 0
    pos_all = lax.broadcasted_iota(I32, (rows, total), 1)

    def tie_cut():
        need = topk - count(lambda sc: sc > thr_all)
        nbits = max(1, int(math.ceil(math.log2(total + 1))))

        def pos_step(j, pos):
            cand = pos | jnp.left_shift(jnp.int32(1), nbits - 1 - j)
            below = count(lambda sc: (sc == thr_all) & (pos_all < cand))
            return jnp.where(below <= need, cand, pos)

        return lax.fori_loop(0, nbits, pos_step, jnp.zeros((rows, 1), I32))

    cut_all = lax.cond(has_ties, tie_cut, lambda: jnp.full((rows, 1), total, I32))

    for s in range(nseq):
        _dsa_sample_attend(s, step * nseq + s, nsteps * nseq, thr_all[s * t:(s + 1) * t], cut_all[s * t:(s + 1) * t],
                           qbd_ref, kn_ref, vn_ref, gb_ref, o_ref, k_buf, v_buf, score_scr, k_copy, v_copy, slot,
                           t=t, npages=npages, depth=depth, group=group, page=page)


def _dsa_sample_attend(s, sq, nseqs, thr, cut, qbd_ref, kn_ref, vn_ref, gb_ref, o_ref, k_buf, v_buf, score_scr,
                       k_copy, v_copy, slot, *, t, npages, depth, group, page):
    lane = lax.broadcasted_iota(I32, (t, page), 1)
    qbd = qbd_ref[s]
    b, nb = sq, nseqs

    def masked_logits(k_pages):
        lgs = []
        for k_t, ks in k_pages:
            tile = score_scr[s * t:(s + 1) * t, pl.ds(ks, page)]
            sel = (tile > thr) | ((tile == thr) & (lane + ks < cut))
            sel = jnp.concatenate([sel.astype(I32)] * B_HEADS, axis=0) > 0
            lgs.append(jnp.where(sel, _dot(qbd, k_t), -jnp.inf))
        return jnp.concatenate(lgs, axis=1)

    def accumulate(lg, v_pages, carry):
        m_old, l_old, acc = carry
        m_new = jnp.maximum(m_old, jnp.max(lg, axis=1, keepdims=True))
        m_safe = jnp.where(m_new == -jnp.inf, 0.0, m_new)
        alpha = jnp.exp(m_old - m_safe)
        e = jnp.exp(lg - m_safe)
        l_new = alpha * l_old + jnp.sum(e, axis=1, keepdims=True)
        acc = alpha * acc
        for g, v_t in enumerate(v_pages):
            acc = acc + _dot_nt(e[:, g * page:(g + 1) * page].astype(BF16), v_t)
        return m_new, l_new, acc

    def group_pages(i):
        return [i * group + g for g in range(group)]

    def k_operands(i):
        return [(k_buf[slot(p)].astype(BF16), pl.multiple_of(p * page, page)) for p in group_pages(i)]

    def v_operands(i):
        return [v_buf[slot(p)].astype(BF16) for p in group_pages(i)]

    def refill(copy, i):
        for p in group_pages(i):
            @pl.when(p + depth < npages)
            def _():
                copy(b, p + depth).start()

            @pl.when((p + depth >= npages) & (b + 1 < nb))
            def _():
                copy(b + 1, p + depth - npages).start()

    ngroups = npages // group
    for p in group_pages(0):
        k_copy(b, p).wait()
    lg_first = masked_logits(k_operands(0))
    refill(k_copy, 0)

    def attend_groups(i, carry):
        lg, state = carry
        for p in group_pages(i + 1):
            k_copy(b, p).wait()
        for p in group_pages(i):
            v_copy(b, p).wait()
        lg_next = masked_logits(k_operands(i + 1))
        state = accumulate(lg, v_operands(i), state)
        refill(k_copy, i + 1)
        refill(v_copy, i)
        return lg_next, state

    nrow = B_HEADS * t
    state = (jnp.full((nrow, 1), -jnp.inf, F32), jnp.zeros((nrow, 1), F32), jnp.zeros((nrow, B_WIDTH), F32))
    lg_last, state = lax.fori_loop(0, ngroups - 1, attend_groups, (lg_first, state))
    for p in group_pages(ngroups - 1):
        v_copy(b, p).wait()
    state = accumulate(lg_last, v_operands(ngroups - 1), state)
    refill(v_copy, ngroups - 1)
    _, den, acc = accumulate(masked_logits([(kn_ref[s], npages * page)]), [vn_ref[s]], state)
    o = acc / den
    col = lax.broadcasted_iota(I32, (t, B_WIDTH), 1)
    out = jnp.zeros((t, B_WIDTH), F32)
    for h in range(B_HEADS):
        out = jnp.where((col >= h * B_DH) & (col < (h + 1) * B_DH), o[h * t:(h + 1) * t], out)
    o_ref[s] = (out * gb_ref[s].astype(F32)).astype(BF16)


def _dsa_sample(page_table, qi_rows, wrep, qbd, kin_t, kn_t, vn_t, gbs, cache_kidx_t, cache_k_t, cache_v_t, t):
    bd, npages = page_table.shape
    page = cache_kidx_t.shape[3]
    total = npages * page + t
    topk = min(TOPK_MAX, total // 4)
    group = max(g for g in range(1, SAMPLE_PAGE_GROUP + 1) if npages % g == 0)
    depth = max(d for d in range(group, min(SAMPLE_DMA_DEPTH, npages) + 1, group) if npages % d == 0)
    nseq = SAMPLE_SEQS_PER_STEP if bd % SAMPLE_SEQS_PER_STEP == 0 else 1
    per_b = lambda b, pt: (b, 0, 0)
    hbm = pl.BlockSpec(memory_space=pl.ANY)
    grid_spec = pltpu.PrefetchScalarGridSpec(
        num_scalar_prefetch=1, grid=(bd // nseq,),
        in_specs=[
            pl.BlockSpec((nseq, IDX_HEADS * t, IDX_DIM), per_b),
            pl.BlockSpec((nseq, IDX_HEADS * t, page), per_b),
            pl.BlockSpec((nseq, B_HEADS * t, B_WIDTH), per_b),
            pl.BlockSpec((nseq, IDX_DIM, page), per_b),
            pl.BlockSpec((nseq, B_WIDTH, page), per_b),
            pl.BlockSpec((nseq, B_WIDTH, page), per_b),
            pl.BlockSpec((nseq, t, B_WIDTH), per_b),
            hbm, hbm, hbm,
        ],
        out_specs=pl.BlockSpec((nseq, t, B_WIDTH), per_b),
        scratch_shapes=[
            pltpu.VMEM((2 * nseq, npages, IDX_DIM, page), F32),
            pltpu.VMEM((depth, B_WIDTH, page), F32),
            pltpu.VMEM((depth, B_WIDTH, page), F32),
            pltpu.VMEM((nseq * t, (npages + 1) * page), F32),
            pltpu.SemaphoreType.DMA((2 * nseq * npages,)),
            pltpu.SemaphoreType.DMA((depth,)),
            pltpu.SemaphoreType.DMA((depth,)),
        ],
    )
    return pl.pallas_call(
        functools.partial(_dsa_sample_kernel, t=t, npages=npages, topk=topk, depth=depth, group=group, nseq=nseq),
        grid_spec=grid_spec,
        out_shape=jax.ShapeDtypeStruct((bd, t, B_WIDTH), BF16),
        compiler_params=pltpu.CompilerParams(dimension_semantics=("arbitrary",), vmem_limit_bytes=VMEM_LIMIT),
        name="dsa_sample",
    )(page_table, qi_rows, wrep, qbd, kin_t, kn_t, vn_t, gbs, cache_kidx_t, cache_k_t, cache_v_t)


def _merge_kernel(x_ref, ma_ref, mb_ref, w_ref, g_ref, b_ref, y_ref, *, alpha):
    mix = jnp.concatenate([ma_ref[h] for h in range(A_HEADS)] + [mb_ref[...]], axis=1)
    y = alpha * x_ref[...] + _dot(mix, w_ref[...])
    mu = jnp.mean(y, axis=-1, keepdims=True)
    d = y - mu
    var = jnp.mean(d * d, axis=-1, keepdims=True)
    y_ref[...] = d * lax.rsqrt(var + LN_EPS) * g_ref[...] + b_ref[...]


def _merge(x, mix_a, mix_b, w_out, ln_g, ln_b, alpha, tm):
    n, dm = x.shape
    return pl.pallas_call(
        functools.partial(_merge_kernel, alpha=alpha), grid=(n // tm,),
        in_specs=[
            pl.BlockSpec((tm, dm), lambda i: (i, 0)),
            pl.BlockSpec((A_HEADS, tm, A_DV), lambda i: (0, i, 0)),
            pl.BlockSpec((tm, B_WIDTH), lambda i: (i, 0)),
            pl.BlockSpec((A_WIDTH + B_WIDTH, dm), lambda i: (0, 0)),
            pl.BlockSpec((1, dm), lambda i: (0, 0)),
            pl.BlockSpec((1, dm), lambda i: (0, 0)),
        ],
        out_specs=pl.BlockSpec((tm, dm), lambda i: (i, 0)),
        out_shape=jax.ShapeDtypeStruct((n, dm), F32),
        compiler_params=pltpu.CompilerParams(dimension_semantics=("parallel",), vmem_limit_bytes=VMEM_LIMIT),
        name="merge",
    )(x, mix_a, mix_b, w_out, ln_g, ln_b)


def _split_weights(w_in_l):
    offs = np.cumsum([0, 512, 512, 512, 512, 512, 512, 512, 512, IDX_HEADS * IDX_DIM, IDX_DIM, IDX_HEADS])
    col = lambda i: w_in_l[:, offs[i]:offs[i + 1]]
    qa, fa, ia, ga, qb, kb, vb, gb, qi, ki, wi = (col(i) for i in range(11))
    pad = jnp.zeros((w_in_l.shape[0], LANES - IDX_DIM), w_in_l.dtype)
    wn = jnp.concatenate([qa, fa, ia, ga, kb, gb, ki, pad], axis=1).astype(BF16)
    wt = jnp.concatenate([qb, kb, vb, qi, wi, ki], axis=1).T.astype(BF16)
    return wn, wt


def _layer(xp, xs, cache_k, cache_v, cache_kidx, s0_sample, page_table, w_in_l, lb_l, norm_g_l, kn_g_l, kn_b_l,
           w_out_l, ln_g_l, ln_b_l, alpha):
    b, l, dm = xp.shape
    bd, t, _ = xs.shape
    npages, page = page_table.shape[1], cache_k.shape[2]
    assert page == Q_BLOCK and l % page == 0
    past = npages * page
    wn, wt = _split_weights(w_in_l)
    lb = lb_l.reshape(1, A_WIDTH)
    ng = norm_g_l.reshape(1, A_DV)
    w_out_b = w_out_l.astype(BF16)
    lng, lnb = ln_g_l.reshape(1, dm), ln_b_l.reshape(1, dm)

    tm = 256
    xp2 = xp.reshape(b * l, dm)
    (hq, hk, hg, hv, hgate, kbf, gbs, kibf, qT, vT, qiT2, wT, k_p, v_p, ki_p) = _project(
        xp2, jnp.arange(l, dtype=I32), wn, wt, lb, kn_g_l, kn_b_l, tm)
    mix_a, s_p = _hgrn_prompt(hq, hk, hg, hv, hgate, ng, b, l, min(l, 512))
    mix_b = _dsa_prompt(kibf, qiT2, wT, kbf, qT, vT, gbs, b, l)
    y_p = _merge(xp2, mix_a, mix_b, w_out_b, lng, lnb, alpha, 512).reshape(b, l, dm)

    ns = bd * t
    xs2 = xs.reshape(ns, dm)
    pos_s = past + (jnp.arange(ns, dtype=I32) % t)
    (hq, hk, hg, hv, hgate, kbf, gbs, kibf, qT, vT, qiT2, wT, k_s, v_s, ki_s) = _project(
        xs2, pos_s, wn, wt, lb, kn_g_l, kn_b_l, ns)
    mix_a, s_s = _hgrn_sample(hq, hk, hg, hv, hgate, ng, s0_sample, t)
    qi_nat = qiT2.reshape(ns // Q_BLOCK, IDX_DIM, IDX_HEADS, Q_BLOCK).transpose(0, 3, 2, 1)
    qi_rows = qi_nat.reshape(bd, t, IDX_HEADS, IDX_DIM).transpose(0, 2, 1, 3).reshape(bd, IDX_HEADS * t, IDX_DIM)
    w_rows = wT.T.reshape(bd, t, IDX_HEADS).transpose(0, 2, 1).reshape(bd, IDX_HEADS * t, 1)
    wrep = jnp.broadcast_to(w_rows, (bd, IDX_HEADS * t, page))
    q_nat = qT.T.reshape(bd, 1, t, B_HEADS, B_DH)
    eye = jnp.eye(B_HEADS, dtype=BF16).reshape(1, B_HEADS, 1, B_HEADS, 1)
    qbd = (q_nat * eye).reshape(bd, B_HEADS * t, B_WIDTH)

    def per_seq(pages):
        feat = pages.shape[1]
        a = pages.transpose(1, 0, 2).reshape(feat, bd, t).transpose(1, 0, 2).astype(BF16)
        return jnp.concatenate([a, jnp.zeros((bd, feat, page - t), BF16)], axis=2)

    token_minor = lambda c: jnp.moveaxis(c, 2, -1).reshape(c.shape[0], c.shape[1], -1, page)
    mix_b = _dsa_sample(page_table, qi_rows, wrep, qbd, per_seq(ki_s), per_seq(k_s), per_seq(v_s),
                        gbs.reshape(bd, t, B_WIDTH), token_minor(cache_kidx), token_minor(cache_k),
                        token_minor(cache_v), t).reshape(ns, B_WIDTH)
    y_s = _merge(xs2, mix_a, mix_b, w_out_b, lng, lnb, alpha, ns).reshape(bd, t, dm)

    return (y_p, y_s, k_p, v_p, ki_p, s_p, k_s, v_s, ki_s, s_s)


def kernel(x_prompt, x_sample, cache_k, cache_v, cache_kidx, state_hgrn, page_table, w_in, hgrn_lb_logits,
           hgrn_norm_g, idx_norm_g, idx_norm_b, w_out, ln_g, ln_b):
    depth = w_in.shape[0]
    assert depth == 1, "one layer per step"
    b, l, _ = x_prompt.shape
    bd, t, _ = x_sample.shape
    page = cache_k.shape[2]
    alpha = (2.0 * depth) ** 0.25
    lbs = jnp.cumsum(jax.nn.softmax(hgrn_lb_logits.astype(F32), axis=0), axis=0)[:depth]
    (y_p, y_s, k_p, v_p, ki_p, s_p, k_s, v_s, ki_s, s_s) = _layer(
        x_prompt, x_sample, cache_k, cache_v, cache_kidx, state_hgrn[0], page_table, w_in[0], lbs[0],
        hgrn_norm_g[0], idx_norm_g[0], idx_norm_b[0], w_out[0], ln_g[0], ln_b[0], alpha)
    nat_p = lambda pg, *f: jnp.moveaxis(pg.reshape(b, l // page, 1, *f, page), -1, 3)
    nat_s = lambda pg, *f: pg.transpose(0, 2, 1).reshape(bd, 1, t, *f)
    return (
        y_p, y_s,
        nat_p(k_p, B_HEADS, B_DH), nat_p(v_p, B_HEADS, B_DH), nat_p(ki_p, IDX_DIM),
        s_p[None],
        nat_s(k_s, B_HEADS, B_DH), nat_s(v_s, B_HEADS, B_DH), nat_s(ki_s, IDX_DIM),
        s_s[None],
    )
```

```python
import functools
import math

import jax
import jax.numpy as jnp
import numpy as np
from jax import lax
from jax.experimental import pallas as pl
from jax.experimental.pallas import tpu as pltpu

F32 = jnp.float32
BF16 = jnp.bfloat16
I32 = jnp.int32

A_HEADS = 4
A_DK = 128
A_DV = 128
B_HEADS = 8
B_DH = 64
IDX_HEADS = 16
IDX_DIM = 64
TOPK_MAX = 256
ROPE_THETA = 500000.0
ROT = 16
ROT_HALF = ROT // 2
LN_EPS = 1e-5
Q_BLOCK = 128
K_TILE = 256
ATT_TILE = 512
HGRN_CHUNK = 128
HGRN_SUB = 16
LANES = 128
VMEM_LIMIT = 56 * 1024 * 1024
SAMPLE_DMA_DEPTH = 32
SAMPLE_PAGE_GROUP = 8
SAMPLE_SEQS_PER_STEP = 1

A_WIDTH = A_HEADS * A_DV
B_WIDTH = B_HEADS * B_DH
NAT_WIDTH = 6 * 512 + LANES
TR_ROWS = 3 * 512 + IDX_HEADS * IDX_DIM + IDX_HEADS + IDX_DIM
INT_MIN = -(2 ** 31)
KEY_NEG_INF = -0x7F800000


def _dot(a, b):
    return jnp.dot(a, b, preferred_element_type=F32)


def _dot_nt(a, b):
    return lax.dot_general(a, b, (((1,), (1,)), ((), ())), preferred_element_type=F32)


def _dot_tn(a, b):
    return lax.dot_general(a, b, (((0,), (0,)), ((), ())), preferred_element_type=F32)


def _silu(x):
    return x * jax.nn.sigmoid(x)


def _key_to_float(key):
    m = key >> 31
    mag = (key ^ m) - m
    return pltpu.bitcast(mag | (m & INT_MIN), F32)


def _proj_kernel(x_ref, wn_ref, wt_ref, lb_ref, kng_ref, knb_ref, kngc_ref, knbc_ref, cn_ref, sa_ref, sb_ref,
                 ct_ref, st_ref,
                 hq_ref, hk_ref, hg_ref, hv_ref, hgate_ref, kbf_ref, gb_ref, kibf_ref,
                 qT_ref, vT_ref, qiT_ref, wT_ref, kTp_ref, vTp_ref, kiTp_ref):
    tm = x_ref.shape[0]
    xb = x_ref[...].astype(BF16)

    def nat(col, width=512):
        return _dot(xb, wn_ref[:, col:col + width])

    lb = lb_ref[...]
    qa = nat(0)
    hq = _silu(qa)
    fa = nat(512)
    hg = jnp.log2(lb + (1.0 - lb) * jax.nn.sigmoid(fa))
    hk = (1.0 - lb) * jax.nn.sigmoid(-fa)
    ia = nat(1024)
    hgate = _silu(nat(1536))
    for h in range(A_HEADS):
        sl = slice(h * A_DK, (h + 1) * A_DK)
        hq_ref[h] = hq[:, sl].astype(BF16)
        hk_ref[h] = hk[:, sl]
        hg_ref[h] = hg[:, sl]
        hv_ref[h] = ia[:, sl].astype(BF16)
        hgate_ref[h] = hgate[:, sl].astype(BF16)

    cn, sa, sb = cn_ref[...], sa_ref[...], sb_ref[...]

    def rope_nat(xc):
        return xc * cn + pltpu.roll(xc, LANES - ROT_HALF, 1) * sa + pltpu.roll(xc, ROT_HALF, 1) * sb

    kb = nat(2048)
    kbf_ref[...] = jnp.concatenate(
        [rope_nat(kb[:, c * LANES:(c + 1) * LANES]) for c in range(B_WIDTH // LANES)], axis=1).astype(BF16)
    gb_ref[...] = _silu(nat(2560)).astype(BF16)

    kic = nat(3072, LANES)
    lane = lax.broadcasted_iota(I32, (tm, LANES), 1)
    inb = lane < IDX_DIM
    mu = jnp.sum(jnp.where(inb, kic, 0.0), axis=-1, keepdims=True) * (1.0 / IDX_DIM)
    d = jnp.where(inb, kic - mu, 0.0)
    var = jnp.sum(d * d, axis=-1, keepdims=True) * (1.0 / IDX_DIM)
    kin = d * lax.rsqrt(var + LN_EPS) * kng_ref[...] + knb_ref[...]
    kibf_ref[...] = rope_nat(kin)[:, :IDX_DIM].astype(BF16)

    ct, st = ct_ref[...], st_ref[...]
    npage = tm // Q_BLOCK

    def tr(row, height):
        return _dot_nt(wt_ref[row:row + height, :], xb)

    def rope_tr(blk):
        x1, x2 = blk[0:ROT_HALF], blk[ROT_HALF:ROT]
        return jnp.concatenate([x1 * ct - x2 * st, x1 * st + x2 * ct, blk[ROT:]], axis=0)

    qbT = tr(0, B_WIDTH)
    for h in range(B_HEADS):
        blk = rope_tr(qbT[h * B_DH:(h + 1) * B_DH])
        qT_ref[h * B_DH:(h + 1) * B_DH, :] = (blk * (B_DH ** -0.5)).astype(BF16)
    kbT = tr(B_WIDTH, B_WIDTH)
    for h in range(B_HEADS):
        blk = rope_tr(kbT[h * B_DH:(h + 1) * B_DH])
        for pg in range(npage):
            kTp_ref[pg, h * B_DH:(h + 1) * B_DH, :] = blk[:, pg * Q_BLOCK:(pg + 1) * Q_BLOCK]
    vbT = tr(2 * B_WIDTH, B_WIDTH)
    vT_ref[...] = vbT.astype(BF16)
    for pg in range(npage):
        vTp_ref[pg] = vbT[:, pg * Q_BLOCK:(pg + 1) * Q_BLOCK]
    row = 3 * B_WIDTH
    qiT = tr(row, IDX_HEADS * IDX_DIM)
    for h in range(IDX_HEADS):
        blk = rope_tr(qiT[h * IDX_DIM:(h + 1) * IDX_DIM]).astype(BF16)
        for pg in range(npage):
            qiT_ref[pg, :, h * Q_BLOCK:(h + 1) * Q_BLOCK] = blk[:, pg * Q_BLOCK:(pg + 1) * Q_BLOCK]
    row += IDX_HEADS * IDX_DIM
    wT_ref[...] = tr(row, IDX_HEADS) * (IDX_HEADS ** -0.5 * IDX_DIM ** -0.5)
    row += IDX_HEADS
    kiT = tr(row, IDX_DIM)
    muT = jnp.mean(kiT, axis=0, keepdims=True)
    dT = kiT - muT
    varT = jnp.mean(dT * dT, axis=0, keepdims=True)
    kiT = rope_tr(dT * lax.rsqrt(varT + LN_EPS) * kngc_ref[...] + knbc_ref[...])
    for pg in range(npage):
        kiTp_ref[pg] = kiT[:, pg * Q_BLOCK:(pg + 1) * Q_BLOCK]


def _rope_tables(pos):
    inv = ROPE_THETA ** (-jnp.arange(ROT_HALF, dtype=F32) / ROT_HALF)
    ang = pos.astype(F32)[:, None] * inv[None, :]
    cos, sin = jnp.cos(ang), jnp.sin(ang)
    p = pos.shape[0]
    one = jnp.ones((p, B_DH - ROT), F32)
    zero8 = jnp.zeros((p, ROT_HALF), F32)
    zero = jnp.zeros((p, B_DH - ROT), F32)
    cn = jnp.concatenate([cos, cos, one], axis=1)
    sa = jnp.concatenate([-sin, zero8, zero], axis=1)
    sb = jnp.concatenate([zero8, sin, zero], axis=1)
    tile2 = lambda a: jnp.concatenate([a, a], axis=1)
    return tile2(cn), tile2(sa), tile2(sb), cos.T, sin.T


def _project(x, pos, wn, wt, lb, kn_g, kn_b, tm):
    n, dm = x.shape
    p = pos.shape[0]
    nper = p // tm
    cn, sa, sb, ct, st = _rope_tables(pos)
    padl = lambda a: jnp.concatenate([a, jnp.zeros((LANES - IDX_DIM,), a.dtype)]).reshape(1, LANES)
    grid = (n // tm,)
    row = lambda i: (i, 0)
    full = lambda i: (0, 0)
    head = lambda i: (0, i, 0)
    page = lambda i: (i, 0, 0)
    per = lambda i: (i % nper, 0)
    perT = lambda i: (0, i % nper)
    colT = lambda i: (0, i)
    in_specs = [
        pl.BlockSpec((tm, dm), row),
        pl.BlockSpec((dm, NAT_WIDTH), full, pipeline_mode=pl.Buffered(1)),
        pl.BlockSpec((TR_ROWS, dm), full, pipeline_mode=pl.Buffered(1)),
        pl.BlockSpec((1, A_WIDTH), full),
        pl.BlockSpec((1, LANES), full),
        pl.BlockSpec((1, LANES), full),
        pl.BlockSpec((IDX_DIM, 1), full),
        pl.BlockSpec((IDX_DIM, 1), full),
        pl.BlockSpec((tm, LANES), per),
        pl.BlockSpec((tm, LANES), per),
        pl.BlockSpec((tm, LANES), per),
        pl.BlockSpec((ROT_HALF, tm), perT),
        pl.BlockSpec((ROT_HALF, tm), perT),
    ]
    npage = tm // Q_BLOCK
    hshape = lambda dt: jax.ShapeDtypeStruct((A_HEADS, n, A_DK), dt)
    out_shape = [
        hshape(BF16), hshape(F32), hshape(F32), hshape(BF16), hshape(BF16),
        jax.ShapeDtypeStruct((n, B_WIDTH), BF16), jax.ShapeDtypeStruct((n, B_WIDTH), BF16),
        jax.ShapeDtypeStruct((n, IDX_DIM), BF16),
        jax.ShapeDtypeStruct((B_WIDTH, n), BF16), jax.ShapeDtypeStruct((B_WIDTH, n), BF16),
        jax.ShapeDtypeStruct((n // Q_BLOCK, IDX_DIM, IDX_HEADS * Q_BLOCK), BF16),
        jax.ShapeDtypeStruct((IDX_HEADS, n), F32),
        jax.ShapeDtypeStruct((n // Q_BLOCK, B_WIDTH, Q_BLOCK), F32),
        jax.ShapeDtypeStruct((n // Q_BLOCK, B_WIDTH, Q_BLOCK), F32),
        jax.ShapeDtypeStruct((n // Q_BLOCK, IDX_DIM, Q_BLOCK), F32),
    ]
    hspec = pl.BlockSpec((A_HEADS, tm, A_DK), head)
    out_specs = [
        hspec, hspec, hspec, hspec, hspec,
        pl.BlockSpec((tm, B_WIDTH), row), pl.BlockSpec((tm, B_WIDTH), row),
        pl.BlockSpec((tm, IDX_DIM), row),
        pl.BlockSpec((B_WIDTH, tm), colT), pl.BlockSpec((B_WIDTH, tm), colT),
        pl.BlockSpec((npage, IDX_DIM, IDX_HEADS * Q_BLOCK), page),
        pl.BlockSpec((IDX_HEADS, tm), colT),
        pl.BlockSpec((npage, B_WIDTH, Q_BLOCK), page),
        pl.BlockSpec((npage, B_WIDTH, Q_BLOCK), page),
        pl.BlockSpec((npage, IDX_DIM, Q_BLOCK), page),
    ]
    return pl.pallas_call(
        _proj_kernel, grid=grid, in_specs=in_specs, out_specs=out_specs, out_shape=out_shape,
        compiler_params=pltpu.CompilerParams(dimension_semantics=("parallel",), vmem_limit_bytes=VMEM_LIMIT),
        name="proj",
    )(x, wn, wt, lb, padl(kn_g), padl(kn_b), kn_g.reshape(IDX_DIM, 1), kn_b.reshape(IDX_DIM, 1),
      cn, sa, sb, ct, st)


def _hgrn_cumsum(g, c):
    if c == LANES:
        ri = lax.broadcasted_iota(I32, (c, c), 0)
        ci = lax.broadcasted_iota(I32, (c, c), 1)
        tri = jnp.where(ci <= ri, 1.0, 0.0).astype(BF16)
        g1 = g.astype(BF16)
        e1 = g - g1.astype(F32)
        g2 = e1.astype(BF16)
        g3 = (e1 - g2.astype(F32)).astype(BF16)
        return _dot(tri, g1) + _dot(tri, g2) + _dot(tri, g3)
    ri = lax.broadcasted_iota(I32, (c, A_DK), 0)
    cum = jnp.zeros((c, A_DK), F32)
    for s in range(c):
        cum = cum + jnp.where(ri >= s, g[s:s + 1], 0.0)
    return cum


def _hgrn_chunks(chains, ng, c):
    r = HGRN_SUB
    nsub = c // r
    rowi = lax.broadcasted_iota(I32, (r, A_DK), 0)
    lane = lax.broadcasted_iota(I32, (r, LANES), 1)
    cums = [_hgrn_cumsum(g, c) for _, _, g, _, _, _ in chains]

    light = []
    for (q, k, _, v, _, st), cum in zip(chains, cums):
        o_state = _dot_nt((q * jnp.exp2(cum)).astype(BF16), st.astype(BF16))
        off = [None]
        for i in range(1, nsub):
            ref = cum[i * r - 1:i * r]
            qt = (q[i * r:(i + 1) * r] * jnp.exp2(cum[i * r:(i + 1) * r] - ref)).astype(BF16)
            kt = (k[:i * r] * jnp.exp2(ref - cum[:i * r])).astype(BF16)
            kt = jnp.concatenate([kt, jnp.zeros((LANES - i * r, A_DK), BF16)], axis=0)
            off.append(_dot_nt(qt, kt))
        vb = v.astype(BF16)
        last = cum[c - 1:c]
        kh = (k * jnp.exp2(last - cum)).astype(BF16)
        if c < LANES:
            zpad = jnp.zeros((LANES - c, A_DK), BF16)
            vb = jnp.concatenate([vb, zpad], axis=0)
            kh = jnp.concatenate([kh, zpad], axis=0)
        st_new = st * jnp.exp2(last) + _dot_tn(vb, kh)
        light.append((o_state, off, vb, st_new))

    rsums = []
    for (q, k, _, _, _, _), cum in zip(chains, cums):
        prods = []
        for i in range(nsub):
            qs, cs = q[i * r:(i + 1) * r], cum[i * r:(i + 1) * r]
            for s in range(r):
                row = i * r + s
                p = (qs * k[row:row + 1]) * jnp.exp2(cs - cum[row:row + 1])
                prods.append(jnp.where(rowi >= s, p, 0.0).astype(BF16))
        if nsub % 2 == 0:
            half = len(prods) // 2
            both = jnp.concatenate([jnp.concatenate(prods[:half], axis=0), jnp.concatenate(prods[half:], axis=0)],
                                   axis=1)
            wr = lax.broadcasted_iota(I32, (2 * A_DK, 2 * LANES), 0) < A_DK
            wc = lax.broadcasted_iota(I32, (2 * A_DK, 2 * LANES), 1) < LANES
            sums = _dot(both, jnp.where(wr == wc, 1.0, 0.0).astype(BF16))
            rsums.append(lambda row, sums=sums, half=half: (
                sums[row * r:(row + 1) * r, :LANES] if row < half
                else sums[(row - half) * r:(row - half + 1) * r, LANES:]))
        else:
            sums = _dot(jnp.concatenate(prods, axis=0), jnp.ones((A_DK, LANES), BF16))
            rsums.append(lambda row, sums=sums: sums[row * r:(row + 1) * r])

    results = []
    for (_, _, _, _, gate, _), (o_state, off, vb, st_new), rsum in zip(chains, light, rsums):
        sc_rows = []
        for i in range(nsub):
            sci = jnp.zeros((r, LANES), F32)
            for s in range(r):
                row = i * r + s
                sci = jnp.where(lane == row, rsum(row), sci)
            sc_rows.append(sci if i == 0 else sci + off[i])
        sc = jnp.concatenate(sc_rows, axis=0).astype(BF16)
        o = o_state + _dot(sc, vb)
        ms = jnp.mean(o * o, axis=-1, keepdims=True)
        results.append((o * lax.rsqrt(ms + LN_EPS) * ng * gate, st_new))
    return results


def _hgrn_prompt_kernel(q_ref, k_ref, g_ref, v_ref, gate_ref, ng_ref, o_ref, sf_ref, st_scr):
    c = HGRN_CHUNK
    nchunk = q_ref.shape[1] // c
    tb = pl.program_id(1)

    @pl.when(tb == 0)
    def _():
        st_scr[...] = jnp.zeros_like(st_scr)

    ng = ng_ref[...]

    def body(j, carry):
        sl = pl.ds(pl.multiple_of(j * c, c), c)
        loaded = [(q_ref[h, sl, :].astype(F32), k_ref[h, sl, :], g_ref[h, sl, :], v_ref[h, sl, :].astype(F32),
                   gate_ref[h, sl, :].astype(F32), st_scr[h]) for h in range(A_HEADS)]
        for h, (o, st_new) in enumerate(_hgrn_chunks(loaded, ng, c)):
            st_scr[h] = st_new
            o_ref[h, sl, :] = o.astype(BF16)
        return carry

    lax.fori_loop(0, nchunk, body, 0)

    @pl.when(tb == pl.num_programs(1) - 1)
    def _():
        for h in range(A_HEADS):
            sf_ref[0, h] = st_scr[h].T


def _hgrn_prompt(hq, hk, hg, hv, hgate, ng, batch, seq, tb):
    n = batch * seq
    nt = seq // tb
    blk = pl.BlockSpec((A_HEADS, tb, A_DK), lambda b, t: (0, b * nt + t, 0))
    return pl.pallas_call(
        _hgrn_prompt_kernel, grid=(batch, nt),
        in_specs=[blk, blk, blk, blk, blk, pl.BlockSpec((1, A_DV), lambda b, t: (0, 0))],
        out_specs=[blk, pl.BlockSpec((1, A_HEADS, A_DK, A_DV), lambda b, t: (b, 0, 0, 0))],
        out_shape=[jax.ShapeDtypeStruct((A_HEADS, n, A_DV), BF16),
                   jax.ShapeDtypeStruct((batch, A_HEADS, A_DK, A_DV), F32)],
        scratch_shapes=[pltpu.VMEM((A_HEADS, A_DV, A_DK), F32)],
        compiler_params=pltpu.CompilerParams(dimension_semantics=("parallel", "arbitrary"),
                                             vmem_limit_bytes=VMEM_LIMIT),
        name="hgrn_prompt",
    )(hq, hk, hg, hv, hgate, ng)


def _hgrn_sample_kernel(q_ref, k_ref, g_ref, v_ref, gate_ref, ng_ref, s0_ref, o_ref, sf_ref, *, t, nb):
    c = HGRN_SUB
    ng = ng_ref[...]
    zpad = jnp.zeros((c - t, A_DK), F32)
    pad = lambda a: jnp.concatenate([a, zpad], axis=0)
    chains = []
    for h in range(A_HEADS):
        q, k, g = q_ref[h].astype(F32), k_ref[h], g_ref[h]
        v, gate = v_ref[h].astype(F32), gate_ref[h].astype(F32)
        for b in range(nb):
            sl = slice(b * t, (b + 1) * t)
            chains.append((pad(q[sl]), pad(k[sl]), pad(g[sl]), pad(v[sl]), pad(gate[sl]), s0_ref[b, h].T))
    results = _hgrn_chunks(chains, ng, c)
    for h in range(A_HEADS):
        for b in range(nb):
            sf_ref[b, h] = results[h * nb + b][1].T
        o_ref[h] = jnp.concatenate([results[h * nb + b][0][:t] for b in range(nb)], axis=0).astype(BF16)


def _hgrn_sample(hq, hk, hg, hv, hgate, ng, s0, t):
    bd = s0.shape[0]
    nb = 16 // t
    blk = pl.BlockSpec((A_HEADS, nb * t, A_DK), lambda i: (0, i, 0))
    sblk = pl.BlockSpec((nb, A_HEADS, A_DK, A_DV), lambda i: (i, 0, 0, 0))
    return pl.pallas_call(
        functools.partial(_hgrn_sample_kernel, t=t, nb=nb), grid=(bd // nb,),
        in_specs=[blk, blk, blk, blk, blk, pl.BlockSpec((1, A_DV), lambda i: (0, 0)), sblk],
        out_specs=[blk, sblk],
        out_shape=[jax.ShapeDtypeStruct((A_HEADS, bd * t, A_DV), BF16),
                   jax.ShapeDtypeStruct((bd, A_HEADS, A_DK, A_DV), F32)],
        compiler_params=pltpu.CompilerParams(dimension_semantics=("parallel",), vmem_limit_bytes=VMEM_LIMIT),
        name="hgrn_sample",
    )(hq, hk, hg, hv, hgate, ng, s0)


def _count_tiles(score_ref, ntiles, pred):
    def body(j, acc):
        ks = pl.multiple_of(j * ATT_TILE, ATT_TILE)
        hit = jnp.where(pred(score_ref[pl.ds(ks, ATT_TILE), :], ks), 1, 0)
        return acc + jnp.sum(hit.reshape(ATT_TILE // 8, 8, Q_BLOCK), axis=0)
    acc = lax.fori_loop(0, ntiles, body, jnp.zeros((8, Q_BLOCK), I32))
    return jnp.sum(acc, axis=0, keepdims=True)


def _kth_largest(count_ge, topk, shape, two_bits=False):
    c0 = count_ge(jnp.zeros(shape, F32))
    ok = c0 >= topk
    thr = jnp.where(ok, 0, INT_MIN)
    nge = jnp.where(ok, c0, 0)

    def place(bit, carry):
        thr, nge = carry
        cand = thr | bit
        cnt = count_ge(_key_to_float(cand))
        ok = cnt >= topk
        return jnp.where(ok, cand, thr), jnp.where(ok, cnt, nge)

    if not two_bits:
        return lax.fori_loop(0, 31, lambda i, c: place(jnp.left_shift(jnp.int32(1), 30 - i), c), (thr, nge))

    def place_two(i, carry):
        thr, nge = carry
        hi = jnp.left_shift(jnp.int32(1), 30 - 2 * i)
        lo = jnp.left_shift(jnp.int32(1), 29 - 2 * i)
        n_hi, n_lo, n_both = (count_ge(_key_to_float(thr | bits)) for bits in (hi, lo, hi | lo))
        ok_hi = n_hi >= topk
        thr1, nge1 = jnp.where(ok_hi, thr | hi, thr), jnp.where(ok_hi, n_hi, nge)
        n2 = jnp.where(ok_hi, n_both, n_lo)
        ok2 = n2 >= topk
        return jnp.where(ok2, thr1 | lo, thr1), jnp.where(ok2, n2, nge1)

    return place(jnp.int32(1), lax.fori_loop(0, 15, place_two, (thr, nge)))


def _dsa_prompt_kernel(ki_ref, qiT_ref, wT_ref, k_ref, qT_ref, vT_ref, gb_ref, o_ref,
                       score_scr, bias_scr, lg_scr, oT_scr, *, topk):
    i = pl.program_id(1)
    ntiles = (i * Q_BLOCK + Q_BLOCK + K_TILE - 1) // K_TILE
    qpos = i * Q_BLOCK + lax.broadcasted_iota(I32, (K_TILE, Q_BLOCK), 1)
    krow = lax.broadcasted_iota(I32, (K_TILE, Q_BLOCK), 0)

    qi = qiT_ref[0]
    wT = wT_ref[...]

    def score_body(j, carry):
        ks = pl.multiple_of(j * K_TILE, K_TILE)
        x = _dot(ki_ref[pl.ds(ks, K_TILE), :], qi)
        sc = jnp.zeros((K_TILE, Q_BLOCK), F32)
        for h in range(IDX_HEADS):
            sc = sc + jnp.maximum(x[:, h * Q_BLOCK:(h + 1) * Q_BLOCK], 0.0) * wT[h:h + 1]
        score_scr[pl.ds(ks, K_TILE), :] = jnp.where(krow + ks <= qpos, sc, -jnp.inf)
        return carry

    lax.fori_loop(0, ntiles, score_body, 0)

    natt = (i * Q_BLOCK + Q_BLOCK + ATT_TILE - 1) // ATT_TILE

    @pl.when(natt * (ATT_TILE // K_TILE) > ntiles)
    def _():
        score_scr[pl.ds(pl.multiple_of(ntiles * K_TILE, K_TILE), K_TILE), :] = jnp.full((K_TILE, Q_BLOCK), -jnp.inf, F32)

    count = functools.partial(_count_tiles, score_scr, natt)
    arow = lax.broadcasted_iota(I32, (ATT_TILE, Q_BLOCK), 0)
    thr, nge = lax.cond(
        (i + 1) * Q_BLOCK > topk,
        lambda: _kth_largest(lambda x: count(lambda tile, ks: tile >= x), topk, (1, Q_BLOCK)),
        lambda: (jnp.full((1, Q_BLOCK), KEY_NEG_INF, I32), jnp.zeros((1, Q_BLOCK), I32)))
    live = thr > KEY_NEG_INF
    thr = _key_to_float(jnp.maximum(thr, KEY_NEG_INF + 1))
    no_cut = jnp.full((1, Q_BLOCK), natt * ATT_TILE, I32)
    has_ties = jnp.max(jnp.where(live & (nge > topk), 1, 0)) > 0

    def tie_cut():
        need = topk - count(lambda tile, ks: tile > thr)
        nbits = max(1, int(math.ceil(math.log2(score_scr.shape[0] + 1))))

        def body(b, pos):
            cand = pos | jnp.left_shift(jnp.int32(1), nbits - 1 - b)
            below = count(lambda tile, ks: (tile == thr) & (arow + ks < cand))
            return jnp.where(below <= need, cand, pos)

        return lax.fori_loop(0, nbits, body, jnp.zeros((1, Q_BLOCK), I32))

    cut = lax.cond(has_ties, tie_cut, lambda: no_cut)

    def bias_body(j, carry):
        ks = pl.multiple_of(j * K_TILE, K_TILE)
        tile = score_scr[pl.ds(ks, K_TILE), :]
        sel = (tile > thr) | ((tile == thr) & (krow + ks < cut))
        bias_scr[pl.ds(ks, K_TILE), :] = jnp.where(sel, 0.0, -jnp.inf)
        return carry

    lax.fori_loop(0, natt * (ATT_TILE // K_TILE), bias_body, 0)

    npair = B_HEADS // 2
    z = jnp.zeros((B_DH, Q_BLOCK), BF16)
    rhs = []
    for p in range(npair):
        qp = qT_ref[p * LANES:(p + 1) * LANES, :]
        rhs.append(jnp.concatenate([jnp.concatenate([qp[:B_DH], z], axis=0),
                                    jnp.concatenate([z, qp[B_DH:]], axis=0)], axis=1))
    fold = lambda a: a.reshape(ATT_TILE // 8, 8, a.shape[1])

    def logit_body(j, mx):
        ks = pl.multiple_of(j * ATT_TILE, ATT_TILE)
        bias = bias_scr[pl.ds(ks, ATT_TILE), :]
        bias2 = jnp.concatenate([bias, bias], axis=1)
        lgs = [_dot(k_ref[pl.ds(ks, ATT_TILE), p * LANES:(p + 1) * LANES], rhs[p]) + bias2 for p in range(npair)]
        lg_scr[pl.ds(ks, ATT_TILE), :] = jnp.concatenate(lgs, axis=1)
        return tuple(jnp.maximum(m, jnp.max(fold(lg), axis=0)) for m, lg in zip(mx, lgs))

    mx = lax.fori_loop(0, natt, logit_body,
                       tuple(jnp.full((8, 2 * Q_BLOCK), -jnp.inf, F32) for _ in range(npair)))
    mx = jnp.concatenate([jnp.max(m, axis=0, keepdims=True) for m in mx], axis=1)
    oT_scr[...] = jnp.zeros_like(oT_scr)

    def value_body(j, den):
        ks = pl.multiple_of(j * ATT_TILE, ATT_TILE)
        new_den, new_acc = [], []
        for h in range(B_HEADS):
            cols = slice(h * Q_BLOCK, (h + 1) * Q_BLOCK)
            rows = slice(h * B_DH, (h + 1) * B_DH)
            e = jnp.exp(lg_scr[pl.ds(ks, ATT_TILE), cols] - mx[:, cols])
            new_den.append(den[h] + jnp.sum(fold(e), axis=0))
            new_acc.append(oT_scr[rows, :] + _dot(vT_ref[rows, pl.ds(ks, ATT_TILE)], e.astype(BF16)))
        for h in range(B_HEADS):
            oT_scr[h * B_DH:(h + 1) * B_DH, :] = new_acc[h]
        return tuple(new_den)

    den = lax.fori_loop(0, natt, value_body, tuple(jnp.zeros((8, Q_BLOCK), F32) for _ in range(B_HEADS)))
    for h in range(B_HEADS):
        rows = slice(h * B_DH, (h + 1) * B_DH)
        oT_scr[rows, :] = oT_scr[rows, :] / jnp.sum(den[h], axis=0, keepdims=True)

    o_ref[...] = (oT_scr[...].T * gb_ref[...].astype(F32)).astype(BF16)


def _dsa_prompt(kibf, qiT2, wT, kbf, qT, vT, gbs, batch, seq):
    n = batch * seq
    nq = seq // Q_BLOCK
    topk = min(TOPK_MAX, seq // 4)
    assert seq % ATT_TILE == 0 and topk <= K_TILE
    return pl.pallas_call(
        functools.partial(_dsa_prompt_kernel, topk=topk), grid=(batch, nq),
        in_specs=[
            pl.BlockSpec((seq, IDX_DIM), lambda b, i: (b, 0)),
            pl.BlockSpec((1, IDX_DIM, IDX_HEADS * Q_BLOCK), lambda b, i: (b * nq + i, 0, 0)),
            pl.BlockSpec((IDX_HEADS, Q_BLOCK), lambda b, i: (0, b * nq + i)),
            pl.BlockSpec((seq, B_WIDTH), lambda b, i: (b, 0)),
            pl.BlockSpec((B_WIDTH, Q_BLOCK), lambda b, i: (0, b * nq + i)),
            pl.BlockSpec((B_WIDTH, seq), lambda b, i: (0, b)),
            pl.BlockSpec((Q_BLOCK, B_WIDTH), lambda b, i: (b * nq + i, 0)),
        ],
        out_specs=pl.BlockSpec((Q_BLOCK, B_WIDTH), lambda b, i: (b * nq + i, 0)),
        out_shape=jax.ShapeDtypeStruct((n, B_WIDTH), BF16),
        scratch_shapes=[
            pltpu.VMEM((seq, Q_BLOCK), F32),
            pltpu.VMEM((seq, Q_BLOCK), F32),
            pltpu.VMEM((seq, B_HEADS * Q_BLOCK), F32),
            pltpu.VMEM((B_WIDTH, Q_BLOCK), F32),
        ],
        compiler_params=pltpu.CompilerParams(dimension_semantics=("parallel", "arbitrary"),
                                             vmem_limit_bytes=VMEM_LIMIT),
        name="dsa_prompt",
    )(kibf, qiT2, wT, kbf, qT, vT, gbs)


def _dsa_sample_kernel(pt_ref, qi_ref, wrep_ref, qbd_ref, kin_ref, kn_ref, vn_ref, gb_ref,
                       cki_hbm, ck_hbm, cv_hbm, o_ref, ki_buf, k_buf, v_buf, score_scr, sem_i, sem_k, sem_v,
                       *, t, npages, topk, depth, group, nseq):
    step = pl.program_id(0)
    nsteps = pl.num_programs(0)
    page = ki_buf.shape[3]
    total = (npages + 1) * page
    rows = nseq * t
    lane = lax.broadcasted_iota(I32, (t, page), 1)
    qrow = lax.broadcasted_iota(I32, (t, page), 0)

    def slot(p):
        return p % depth if isinstance(p, int) else lax.rem(p, depth)

    def ki_copy(seq, p):
        half = lax.rem(seq, 2 * nseq)
        return pltpu.make_async_copy(cki_hbm.at[pt_ref[seq, p], 0], ki_buf.at[half, p], sem_i.at[half * npages + p])

    def k_copy(seq, p):
        return pltpu.make_async_copy(ck_hbm.at[pt_ref[seq, p], 0], k_buf.at[slot(p)], sem_k.at[slot(p)])

    def v_copy(seq, p):
        return pltpu.make_async_copy(cv_hbm.at[pt_ref[seq, p], 0], v_buf.at[slot(p)], sem_v.at[slot(p)])

    def start_ki(seq):
        def body(p, carry):
            ki_copy(seq, p).start()
            return carry
        lax.fori_loop(0, npages, body, 0, unroll=math.gcd(npages, 8))

    @pl.when(step == 0)
    def _():
        for s in range(nseq):
            start_ki(jnp.int32(s))
        for p in range(depth):
            k_copy(0, p).start()
            v_copy(0, p).start()

    @pl.when(step + 1 < nsteps)
    def _():
        for s in range(nseq):
            start_ki((step + 1) * nseq + s)

    def scores(s, kidx_t):
        x = _dot(qi_ref[s], kidx_t)
        xw = jnp.maximum(x, 0.0) * wrep_ref[s]
        sc = xw[0:t]
        for h in range(1, IDX_HEADS):
            sc = sc + xw[h * t:(h + 1) * t]
        return sc

    for s in range(nseq):
        sq = step * nseq + s

        def score_pages(i, carry, s=s, sq=sq):
            for g in range(group):
                ki_copy(sq, i * group + g).wait()
            sc = [scores(s, ki_buf[lax.rem(sq, 2 * nseq), i * group + g].astype(BF16)) for g in range(group)]
            score_scr[s * t:(s + 1) * t, pl.ds(pl.multiple_of(i * group * page, group * page), group * page)] = (
                jnp.concatenate(sc, axis=1))
            return carry

        lax.fori_loop(0, npages // group, score_pages, 0)
        score_scr[s * t:(s + 1) * t, npages * page:] = jnp.where(lane <= qrow, scores(s, kin_ref[s]), -jnp.inf)

    def count(pred):
        hit = jnp.where(pred(score_scr[...]), 1, 0)
        part = hit[:, 0:page]
        for c in range(1, npages + 1):
            part = part + hit[:, c * page:(c + 1) * page]
        return jnp.sum(part, axis=1, keepdims=True)

    thr_all, nge = _kth_largest(lambda x: count(lambda sc: sc >= x), topk, (rows, 1), two_bits=True)
    live = thr_all > KEY_NEG_INF
    thr_all = _key_to_float(jnp.maximum(thr_all, KEY_NEG_INF + 1))
    has_ties = jnp.max(jnp.where(live & (nge > topk), 1, 0)) > 0
    pos_all = lax.broadcasted_iota(I32, (rows, total), 1)

    def tie_cut():
        need = topk - count(lambda sc: sc > thr_all)
        nbits = max(1, int(math.ceil(math.log2(total + 1))))

        def pos_step(j, pos):
            cand = pos | jnp.left_shift(jnp.int32(1), nbits - 1 - j)
            below = count(lambda sc: (sc == thr_all) & (pos_all < cand))
            return jnp.where(below <= need, cand, pos)

        return lax.fori_loop(0, nbits, pos_step, jnp.zeros((rows, 1), I32))

    cut_all = lax.cond(has_ties, tie_cut, lambda: jnp.full((rows, 1), total, I32))

    for s in range(nseq):
        _dsa_sample_attend(s, step * nseq + s, nsteps * nseq, thr_all[s * t:(s + 1) * t], cut_all[s * t:(s + 1) * t],
                           qbd_ref, kn_ref, vn_ref, gb_ref, o_ref, k_buf, v_buf, score_scr, k_copy, v_copy, slot,
                           t=t, npages=npages, depth=depth, group=group, page=page)


def _dsa_sample_attend(s, sq, nseqs, thr, cut, qbd_ref, kn_ref, vn_ref, gb_ref, o_ref, k_buf, v_buf, score_scr,
                       k_copy, v_copy, slot, *, t, npages, depth, group, page):
    lane = lax.broadcasted_iota(I32, (t, page), 1)
    qbd = qbd_ref[s]
    b, nb = sq, nseqs

    def masked_logits(k_pages):
        lgs = []
        for k_t, ks in k_pages:
            tile = score_scr[s * t:(s + 1) * t, pl.ds(ks, page)]
            sel = (tile > thr) | ((tile == thr) & (lane + ks < cut))
            sel = jnp.concatenate([sel.astype(I32)] * B_HEADS, axis=0) > 0
            lgs.append(jnp.where(sel, _dot(qbd, k_t), -jnp.inf))
        return jnp.concatenate(lgs, axis=1)

    def accumulate(lg, v_pages, carry):
        m_old, l_old, acc = carry
        m_new = jnp.maximum(m_old, jnp.max(lg, axis=1, keepdims=True))
        m_safe = jnp.where(m_new == -jnp.inf, 0.0, m_new)
        alpha = jnp.exp(m_old - m_safe)
        e = jnp.exp(lg - m_safe)
        l_new = alpha * l_old + jnp.sum(e, axis=1, keepdims=True)
        acc = alpha * acc
        for g, v_t in enumerate(v_pages):
            acc = acc + _dot_nt(e[:, g * page:(g + 1) * page].astype(BF16), v_t)
        return m_new, l_new, acc

    def group_pages(i):
        return [i * group + g for g in range(group)]

    def k_operands(i):
        return [(k_buf[slot(p)].astype(BF16), pl.multiple_of(p * page, page)) for p in group_pages(i)]

    def v_operands(i):
        return [v_buf[slot(p)].astype(BF16) for p in group_pages(i)]

    def refill(copy, i):
        for p in group_pages(i):
            @pl.when(p + depth < npages)
            def _():
                copy(b, p + depth).start()

            @pl.when((p + depth >= npages) & (b + 1 < nb))
            def _():
                copy(b + 1, p + depth - npages).start()

    ngroups = npages // group
    for p in group_pages(0):
        k_copy(b, p).wait()
    lg_first = masked_logits(k_operands(0))
    refill(k_copy, 0)

    def attend_groups(i, carry):
        lg, state = carry
        for p in group_pages(i + 1):
            k_copy(b, p).wait()
        for p in group_pages(i):
            v_copy(b, p).wait()
        lg_next = masked_logits(k_operands(i + 1))
        state = accumulate(lg, v_operands(i), state)
        refill(k_copy, i + 1)
        refill(v_copy, i)
        return lg_next, state

    nrow = B_HEADS * t
    state = (jnp.full((nrow, 1), -jnp.inf, F32), jnp.zeros((nrow, 1), F32), jnp.zeros((nrow, B_WIDTH), F32))
    lg_last, state = lax.fori_loop(0, ngroups - 1, attend_groups, (lg_first, state))
    for p in group_pages(ngroups - 1):
        v_copy(b, p).wait()
    state = accumulate(lg_last, v_operands(ngroups - 1), state)
    refill(v_copy, ngroups - 1)
    _, den, acc = accumulate(masked_logits([(kn_ref[s], npages * page)]), [vn_ref[s]], state)
    o = acc / den
    col = lax.broadcasted_iota(I32, (t, B_WIDTH), 1)
    out = jnp.zeros((t, B_WIDTH), F32)
    for h in range(B_HEADS):
        out = jnp.where((col >= h * B_DH) & (col < (h + 1) * B_DH), o[h * t:(h + 1) * t], out)
    o_ref[s] = (out * gb_ref[s].astype(F32)).astype(BF16)


def _dsa_sample(page_table, qi_rows, wrep, qbd, kin_t, kn_t, vn_t, gbs, cache_kidx_t, cache_k_t, cache_v_t, t):
    bd, npages = page_table.shape
    page = cache_kidx_t.shape[3]
    total = npages * page + t
    topk = min(TOPK_MAX, total // 4)
    group = max(g for g in range(1, SAMPLE_PAGE_GROUP + 1) if npages % g == 0)
    depth = max(d for d in range(group, min(SAMPLE_DMA_DEPTH, npages) + 1, group) if npages % d == 0)
    nseq = SAMPLE_SEQS_PER_STEP if bd % SAMPLE_SEQS_PER_STEP == 0 else 1
    per_b = lambda b, pt: (b, 0, 0)
    hbm = pl.BlockSpec(memory_space=pl.ANY)
    grid_spec = pltpu.PrefetchScalarGridSpec(
        num_scalar_prefetch=1, grid=(bd // nseq,),
        in_specs=[
            pl.BlockSpec((nseq, IDX_HEADS * t, IDX_DIM), per_b),
            pl.BlockSpec((nseq, IDX_HEADS * t, page), per_b),
            pl.BlockSpec((nseq, B_HEADS * t, B_WIDTH), per_b),
            pl.BlockSpec((nseq, IDX_DIM, page), per_b),
            pl.BlockSpec((nseq, B_WIDTH, page), per_b),
            pl.BlockSpec((nseq, B_WIDTH, page), per_b),
            pl.BlockSpec((nseq, t, B_WIDTH), per_b),
            hbm, hbm, hbm,
        ],
        out_specs=pl.BlockSpec((nseq, t, B_WIDTH), per_b),
        scratch_shapes=[
            pltpu.VMEM((2 * nseq, npages, IDX_DIM, page), F32),
            pltpu.VMEM((depth, B_WIDTH, page), F32),
            pltpu.VMEM((depth, B_WIDTH, page), F32),
            pltpu.VMEM((nseq * t, (npages + 1) * page), F32),
            pltpu.SemaphoreType.DMA((2 * nseq * npages,)),
            pltpu.SemaphoreType.DMA((depth,)),
            pltpu.SemaphoreType.DMA((depth,)),
        ],
    )
    return pl.pallas_call(
        functools.partial(_dsa_sample_kernel, t=t, npages=npages, topk=topk, depth=depth, group=group, nseq=nseq),
        grid_spec=grid_spec,
        out_shape=jax.ShapeDtypeStruct((bd, t, B_WIDTH), BF16),
        compiler_params=pltpu.CompilerParams(dimension_semantics=("arbitrary",), vmem_limit_bytes=VMEM_LIMIT),
        name="dsa_sample",
    )(page_table, qi_rows, wrep, qbd, kin_t, kn_t, vn_t, gbs, cache_kidx_t, cache_k_t, cache_v_t)


def _merge_kernel(x_ref, ma_ref, mb_ref, w_ref, g_ref, b_ref, y_ref, *, alpha):
    mix = jnp.concatenate([ma_ref[h] for h in range(A_HEADS)] + [mb_ref[...]], axis=1)
    y = alpha * x_ref[...] + _dot(mix, w_ref[...])
    mu = jnp.mean(y, axis=-1, keepdims=True)
    d = y - mu
    var = jnp.mean(d * d, axis=-1, keepdims=True)
    y_ref[...] = d * lax.rsqrt(var + LN_EPS) * g_ref[...] + b_ref[...]


def _merge(x, mix_a, mix_b, w_out, ln_g, ln_b, alpha, tm):
    n, dm = x.shape
    return pl.pallas_call(
        functools.partial(_merge_kernel, alpha=alpha), grid=(n // tm,),
        in_specs=[
            pl.BlockSpec((tm, dm), lambda i: (i, 0)),
            pl.BlockSpec((A_HEADS, tm, A_DV), lambda i: (0, i, 0)),
            pl.BlockSpec((tm, B_WIDTH), lambda i: (i, 0)),
            pl.BlockSpec((A_WIDTH + B_WIDTH, dm), lambda i: (0, 0)),
            pl.BlockSpec((1, dm), lambda i: (0, 0)),
            pl.BlockSpec((1, dm), lambda i: (0, 0)),
        ],
        out_specs=pl.BlockSpec((tm, dm), lambda i: (i, 0)),
        out_shape=jax.ShapeDtypeStruct((n, dm), F32),
        compiler_params=pltpu.CompilerParams(dimension_semantics=("parallel",), vmem_limit_bytes=VMEM_LIMIT),
        name="merge",
    )(x, mix_a, mix_b, w_out, ln_g, ln_b)


def _split_weights(w_in_l):
    offs = np.cumsum([0, 512, 512, 512, 512, 512, 512, 512, 512, IDX_HEADS * IDX_DIM, IDX_DIM, IDX_HEADS])
    col = lambda i: w_in_l[:, offs[i]:offs[i + 1]]
    qa, fa, ia, ga, qb, kb, vb, gb, qi, ki, wi = (col(i) for i in range(11))
    pad = jnp.zeros((w_in_l.shape[0], LANES - IDX_DIM), w_in_l.dtype)
    wn = jnp.concatenate([qa, fa, ia, ga, kb, gb, ki, pad], axis=1).astype(BF16)
    wt = jnp.concatenate([qb, kb, vb, qi, wi, ki], axis=1).T.astype(BF16)
    return wn, wt


def _layer(xp, xs, cache_k, cache_v, cache_kidx, s0_sample, page_table, w_in_l, lb_l, norm_g_l, kn_g_l, kn_b_l,
           w_out_l, ln_g_l, ln_b_l, alpha):
    b, l, dm = xp.shape
    bd, t, _ = xs.shape
    npages, page = page_table.shape[1], cache_k.shape[2]
    assert page == Q_BLOCK and l % page == 0
    past = npages * page
    wn, wt = _split_weights(w_in_l)
    lb = lb_l.reshape(1, A_WIDTH)
    ng = norm_g_l.reshape(1, A_DV)
    w_out_b = w_out_l.astype(BF16)
    lng, lnb = ln_g_l.reshape(1, dm), ln_b_l.reshape(1, dm)

    tm = 256
    xp2 = xp.reshape(b * l, dm)
    (hq, hk, hg, hv, hgate, kbf, gbs, kibf, qT, vT, qiT2, wT, k_p, v_p, ki_p) = _project(
        xp2, jnp.arange(l, dtype=I32), wn, wt, lb, kn_g_l, kn_b_l, tm)
    mix_a, s_p = _hgrn_prompt(hq, hk, hg, hv, hgate, ng, b, l, min(l, 1024))
    mix_b = _dsa_prompt(kibf, qiT2, wT, kbf, qT, vT, gbs, b, l)
    y_p = _merge(xp2, mix_a, mix_b, w_out_b, lng, lnb, alpha, 1024).reshape(b, l, dm)

    ns = bd * t
    xs2 = xs.reshape(ns, dm)
    pos_s = past + (jnp.arange(ns, dtype=I32) % t)
    (hq, hk, hg, hv, hgate, kbf, gbs, kibf, qT, vT, qiT2, wT, k_s, v_s, ki_s) = _project(
        xs2, pos_s, wn, wt, lb, kn_g_l, kn_b_l, ns)
    mix_a, s_s = _hgrn_sample(hq, hk, hg, hv, hgate, ng, s0_sample, t)
    qi_nat = qiT2.reshape(ns // Q_BLOCK, IDX_DIM, IDX_HEADS, Q_BLOCK).transpose(0, 3, 2, 1)
    qi_rows = qi_nat.reshape(bd, t, IDX_HEADS, IDX_DIM).transpose(0, 2, 1, 3).reshape(bd, IDX_HEADS * t, IDX_DIM)
    w_rows = wT.T.reshape(bd, t, IDX_HEADS).transpose(0, 2, 1).reshape(bd, IDX_HEADS * t, 1)
    wrep = jnp.broadcast_to(w_rows, (bd, IDX_HEADS * t, page))
    q_nat = qT.T.reshape(bd, 1, t, B_HEADS, B_DH)
    eye = jnp.eye(B_HEADS, dtype=BF16).reshape(1, B_HEADS, 1, B_HEADS, 1)
    qbd = (q_nat * eye).reshape(bd, B_HEADS * t, B_WIDTH)

    def per_seq(pages):
        feat = pages.shape[1]
        a = pages.transpose(1, 0, 2).reshape(feat, bd, t).transpose(1, 0, 2).astype(BF16)
        return jnp.concatenate([a, jnp.zeros((bd, feat, page - t), BF16)], axis=2)

    token_minor = lambda c: jnp.moveaxis(c, 2, -1).reshape(c.shape[0], c.shape[1], -1, page)
    mix_b = _dsa_sample(page_table, qi_rows, wrep, qbd, per_seq(ki_s), per_seq(k_s), per_seq(v_s),
                        gbs.reshape(bd, t, B_WIDTH), token_minor(cache_kidx), token_minor(cache_k),
                        token_minor(cache_v), t).reshape(ns, B_WIDTH)
    y_s = _merge(xs2, mix_a, mix_b, w_out_b, lng, lnb, alpha, ns).reshape(bd, t, dm)

    return (y_p, y_s, k_p, v_p, ki_p, s_p, k_s, v_s, ki_s, s_s)


def kernel(x_prompt, x_sample, cache_k, cache_v, cache_kidx, state_hgrn, page_table, w_in, hgrn_lb_logits,
           hgrn_norm_g, idx_norm_g, idx_norm_b, w_out, ln_g, ln_b):
    depth = w_in.shape[0]
    assert depth == 1, "one layer per step"
    b, l, _ = x_prompt.shape
    bd, t, _ = x_sample.shape
    page = cache_k.shape[2]
    alpha = (2.0 * depth) ** 0.25
    lbs = jnp.cumsum(jax.nn.softmax(hgrn_lb_logits.astype(F32), axis=0), axis=0)[:depth]
    (y_p, y_s, k_p, v_p, ki_p, s_p, k_s, v_s, ki_s, s_s) = _layer(
        x_prompt, x_sample, cache_k, cache_v, cache_kidx, state_hgrn[0], page_table, w_in[0], lbs[0],
        hgrn_norm_g[0], idx_norm_g[0], idx_norm_b[0], w_out[0], ln_g[0], ln_b[0], alpha)
    nat_p = lambda pg, *f: jnp.moveaxis(pg.reshape(b, l // page, 1, *f, page), -1, 3)
    nat_s = lambda pg, *f: pg.transpose(0, 2, 1).reshape(bd, 1, t, *f)
    return (
        y_p, y_s,
        nat_p(k_p, B_HEADS, B_DH), nat_p(v_p, B_HEADS, B_DH), nat_p(ki_p, IDX_DIM),
        s_p[None],
        nat_s(k_s, B_HEADS, B_DH), nat_s(v_s, B_HEADS, B_DH), nat_s(ki_s, IDX_DIM),
        s_s[None],
    )
```

```python
import functools
import math

import jax
import jax.numpy as jnp
import numpy as np
from jax import lax
from jax.experimental import pallas as pl
from jax.experimental.pallas import tpu as pltpu

F32 = jnp.float32
BF16 = jnp.bfloat16
I32 = jnp.int32

A_HEADS = 4
A_DK = 128
A_DV = 128
B_HEADS = 8
B_DH = 64
IDX_HEADS = 16
IDX_DIM = 64
TOPK_MAX = 256
ROPE_THETA = 500000.0
ROT = 16
ROT_HALF = ROT // 2
LN_EPS = 1e-5
Q_BLOCK = 128
K_TILE = 512
ATT_TILE = 512
HGRN_CHUNK = 128
HGRN_SUB = 16
LANES = 128
VMEM_LIMIT = 56 * 1024 * 1024
SAMPLE_DMA_DEPTH = 32
SAMPLE_PAGE_GROUP = 8
SAMPLE_SEQS_PER_STEP = 1

A_WIDTH = A_HEADS * A_DV
B_WIDTH = B_HEADS * B_DH
NAT_WIDTH = 6 * 512 + LANES
TR_ROWS = 3 * 512 + IDX_HEADS * IDX_DIM + IDX_HEADS + IDX_DIM
INT_MIN = -(2 ** 31)
KEY_NEG_INF = -0x7F800000


def _dot(a, b):
    return jnp.dot(a, b, preferred_element_type=F32)


def _dot_nt(a, b):
    return lax.dot_general(a, b, (((1,), (1,)), ((), ())), preferred_element_type=F32)


def _dot_tn(a, b):
    return lax.dot_general(a, b, (((0,), (0,)), ((), ())), preferred_element_type=F32)


def _silu(x):
    return x * jax.nn.sigmoid(x)


def _key_to_float(key):
    m = key >> 31
    mag = (key ^ m) - m
    return pltpu.bitcast(mag | (m & INT_MIN), F32)


def _proj_kernel(x_ref, wn_ref, wt_ref, lb_ref, kng_ref, knb_ref, kngc_ref, knbc_ref, cn_ref, sa_ref, sb_ref,
                 ct_ref, st_ref,
                 hq_ref, hk_ref, hg_ref, hv_ref, hgate_ref, kbf_ref, gb_ref, kibf_ref,
                 qT_ref, vT_ref, qiT_ref, wT_ref, kTp_ref, vTp_ref, kiTp_ref):
    tm = x_ref.shape[0]
    xb = x_ref[...].astype(BF16)

    def nat(col, width=512):
        return _dot(xb, wn_ref[:, col:col + width])

    lb = lb_ref[...]
    qa = nat(0)
    hq = _silu(qa)
    fa = nat(512)
    hg = jnp.log2(lb + (1.0 - lb) * jax.nn.sigmoid(fa))
    hk = (1.0 - lb) * jax.nn.sigmoid(-fa)
    ia = nat(1024)
    hgate = _silu(nat(1536))
    for h in range(A_HEADS):
        sl = slice(h * A_DK, (h + 1) * A_DK)
        hq_ref[h] = hq[:, sl].astype(BF16)
        hk_ref[h] = hk[:, sl]
        hg_ref[h] = hg[:, sl]
        hv_ref[h] = ia[:, sl].astype(BF16)
        hgate_ref[h] = hgate[:, sl].astype(BF16)

    cn, sa, sb = cn_ref[...], sa_ref[...], sb_ref[...]

    def rope_nat(xc):
        return xc * cn + pltpu.roll(xc, LANES - ROT_HALF, 1) * sa + pltpu.roll(xc, ROT_HALF, 1) * sb

    kb = nat(2048)
    kbf_ref[...] = jnp.concatenate(
        [rope_nat(kb[:, c * LANES:(c + 1) * LANES]) for c in range(B_WIDTH // LANES)], axis=1).astype(BF16)
    gb_ref[...] = _silu(nat(2560)).astype(BF16)

    kic = nat(3072, LANES)
    lane = lax.broadcasted_iota(I32, (tm, LANES), 1)
    inb = lane < IDX_DIM
    mu = jnp.sum(jnp.where(inb, kic, 0.0), axis=-1, keepdims=True) * (1.0 / IDX_DIM)
    d = jnp.where(inb, kic - mu, 0.0)
    var = jnp.sum(d * d, axis=-1, keepdims=True) * (1.0 / IDX_DIM)
    kin = d * lax.rsqrt(var + LN_EPS) * kng_ref[...] + knb_ref[...]
    kibf_ref[...] = rope_nat(kin)[:, :IDX_DIM].astype(BF16)

    ct, st = ct_ref[...], st_ref[...]
    npage = tm // Q_BLOCK

    def tr(row, height):
        return _dot_nt(wt_ref[row:row + height, :], xb)

    def rope_tr(blk):
        x1, x2 = blk[0:ROT_HALF], blk[ROT_HALF:ROT]
        return jnp.concatenate([x1 * ct - x2 * st, x1 * st + x2 * ct, blk[ROT:]], axis=0)

    qbT = tr(0, B_WIDTH)
    for h in range(B_HEADS):
        blk = rope_tr(qbT[h * B_DH:(h + 1) * B_DH])
        qT_ref[h * B_DH:(h + 1) * B_DH, :] = (blk * (B_DH ** -0.5)).astype(BF16)
    kbT = tr(B_WIDTH, B_WIDTH)
    for h in range(B_HEADS):
        blk = rope_tr(kbT[h * B_DH:(h + 1) * B_DH])
        for pg in range(npage):
            kTp_ref[pg, h * B_DH:(h + 1) * B_DH, :] = blk[:, pg * Q_BLOCK:(pg + 1) * Q_BLOCK]
    vbT = tr(2 * B_WIDTH, B_WIDTH)
    vT_ref[...] = vbT.astype(BF16)
    for pg in range(npage):
        vTp_ref[pg] = vbT[:, pg * Q_BLOCK:(pg + 1) * Q_BLOCK]
    row = 3 * B_WIDTH
    qiT = tr(row, IDX_HEADS * IDX_DIM)
    for h in range(IDX_HEADS):
        blk = rope_tr(qiT[h * IDX_DIM:(h + 1) * IDX_DIM]).astype(BF16)
        for pg in range(npage):
            qiT_ref[pg, :, h * Q_BLOCK:(h + 1) * Q_BLOCK] = blk[:, pg * Q_BLOCK:(pg + 1) * Q_BLOCK]
    row += IDX_HEADS * IDX_DIM
    wT_ref[...] = tr(row, IDX_HEADS) * (IDX_HEADS ** -0.5 * IDX_DIM ** -0.5)
    row += IDX_HEADS
    kiT = tr(row, IDX_DIM)
    muT = jnp.mean(kiT, axis=0, keepdims=True)
    dT = kiT - muT
    varT = jnp.mean(dT * dT, axis=0, keepdims=True)
    kiT = rope_tr(dT * lax.rsqrt(varT + LN_EPS) * kngc_ref[...] + knbc_ref[...])
    for pg in range(npage):
        kiTp_ref[pg] = kiT[:, pg * Q_BLOCK:(pg + 1) * Q_BLOCK]


def _rope_tables(pos):
    inv = ROPE_THETA ** (-jnp.arange(ROT_HALF, dtype=F32) / ROT_HALF)
    ang = pos.astype(F32)[:, None] * inv[None, :]
    cos, sin = jnp.cos(ang), jnp.sin(ang)
    p = pos.shape[0]
    one = jnp.ones((p, B_DH - ROT), F32)
    zero8 = jnp.zeros((p, ROT_HALF), F32)
    zero = jnp.zeros((p, B_DH - ROT), F32)
    cn = jnp.concatenate([cos, cos, one], axis=1)
    sa = jnp.concatenate([-sin, zero8, zero], axis=1)
    sb = jnp.concatenate([zero8, sin, zero], axis=1)
    tile2 = lambda a: jnp.concatenate([a, a], axis=1)
    return tile2(cn), tile2(sa), tile2(sb), cos.T, sin.T


def _project(x, pos, wn, wt, lb, kn_g, kn_b, tm):
    n, dm = x.shape
    p = pos.shape[0]
    nper = p // tm
    cn, sa, sb, ct, st = _rope_tables(pos)
    padl = lambda a: jnp.concatenate([a, jnp.zeros((LANES - IDX_DIM,), a.dtype)]).reshape(1, LANES)
    grid = (n // tm,)
    row = lambda i: (i, 0)
    full = lambda i: (0, 0)
    head = lambda i: (0, i, 0)
    page = lambda i: (i, 0, 0)
    per = lambda i: (i % nper, 0)
    perT = lambda i: (0, i % nper)
    colT = lambda i: (0, i)
    in_specs = [
        pl.BlockSpec((tm, dm), row),
        pl.BlockSpec((dm, NAT_WIDTH), full, pipeline_mode=pl.Buffered(1)),
        pl.BlockSpec((TR_ROWS, dm), full, pipeline_mode=pl.Buffered(1)),
        pl.BlockSpec((1, A_WIDTH), full),
        pl.BlockSpec((1, LANES), full),
        pl.BlockSpec((1, LANES), full),
        pl.BlockSpec((IDX_DIM, 1), full),
        pl.BlockSpec((IDX_DIM, 1), full),
        pl.BlockSpec((tm, LANES), per),
        pl.BlockSpec((tm, LANES), per),
        pl.BlockSpec((tm, LANES), per),
        pl.BlockSpec((ROT_HALF, tm), perT),
        pl.BlockSpec((ROT_HALF, tm), perT),
    ]
    npage = tm // Q_BLOCK
    hshape = lambda dt: jax.ShapeDtypeStruct((A_HEADS, n, A_DK), dt)
    out_shape = [
        hshape(BF16), hshape(F32), hshape(F32), hshape(BF16), hshape(BF16),
        jax.ShapeDtypeStruct((n, B_WIDTH), BF16), jax.ShapeDtypeStruct((n, B_WIDTH), BF16),
        jax.ShapeDtypeStruct((n, IDX_DIM), BF16),
        jax.ShapeDtypeStruct((B_WIDTH, n), BF16), jax.ShapeDtypeStruct((B_WIDTH, n), BF16),
        jax.ShapeDtypeStruct((n // Q_BLOCK, IDX_DIM, IDX_HEADS * Q_BLOCK), BF16),
        jax.ShapeDtypeStruct((IDX_HEADS, n), F32),
        jax.ShapeDtypeStruct((n // Q_BLOCK, B_WIDTH, Q_BLOCK), F32),
        jax.ShapeDtypeStruct((n // Q_BLOCK, B_WIDTH, Q_BLOCK), F32),
        jax.ShapeDtypeStruct((n // Q_BLOCK, IDX_DIM, Q_BLOCK), F32),
    ]
    hspec = pl.BlockSpec((A_HEADS, tm, A_DK), head)
    out_specs = [
        hspec, hspec, hspec, hspec, hspec,
        pl.BlockSpec((tm, B_WIDTH), row), pl.BlockSpec((tm, B_WIDTH), row),
        pl.BlockSpec((tm, IDX_DIM), row),
        pl.BlockSpec((B_WIDTH, tm), colT), pl.BlockSpec((B_WIDTH, tm), colT),
        pl.BlockSpec((npage, IDX_DIM, IDX_HEADS * Q_BLOCK), page),
        pl.BlockSpec((IDX_HEADS, tm), colT),
        pl.BlockSpec((npage, B_WIDTH, Q_BLOCK), page),
        pl.BlockSpec((npage, B_WIDTH, Q_BLOCK), page),
        pl.BlockSpec((npage, IDX_DIM, Q_BLOCK), page),
    ]
    return pl.pallas_call(
        _proj_kernel, grid=grid, in_specs=in_specs, out_specs=out_specs, out_shape=out_shape,
        compiler_params=pltpu.CompilerParams(dimension_semantics=("parallel",), vmem_limit_bytes=VMEM_LIMIT),
        name="proj",
    )(x, wn, wt, lb, padl(kn_g), padl(kn_b), kn_g.reshape(IDX_DIM, 1), kn_b.reshape(IDX_DIM, 1),
      cn, sa, sb, ct, st)


def _hgrn_cumsum(g, c):
    if c == LANES:
        ri = lax.broadcasted_iota(I32, (c, c), 0)
        ci = lax.broadcasted_iota(I32, (c, c), 1)
        tri = jnp.where(ci <= ri, 1.0, 0.0).astype(BF16)
        g1 = g.astype(BF16)
        e1 = g - g1.astype(F32)
        g2 = e1.astype(BF16)
        g3 = (e1 - g2.astype(F32)).astype(BF16)
        return _dot(tri, g1) + _dot(tri, g2) + _dot(tri, g3)
    ri = lax.broadcasted_iota(I32, (c, A_DK), 0)
    cum = jnp.zeros((c, A_DK), F32)
    for s in range(c):
        cum = cum + jnp.where(ri >= s, g[s:s + 1], 0.0)
    return cum


def _hgrn_chunks(chains, ng, c):
    r = HGRN_SUB
    nsub = c // r
    rowi = lax.broadcasted_iota(I32, (r, A_DK), 0)
    lane = lax.broadcasted_iota(I32, (r, LANES), 1)
    cums = [_hgrn_cumsum(g, c) for _, _, g, _, _, _ in chains]

    light = []
    for (q, k, _, v, _, st), cum in zip(chains, cums):
        o_state = _dot_nt((q * jnp.exp2(cum)).astype(BF16), st.astype(BF16))
        off = [None]
        for i in range(1, nsub):
            ref = cum[i * r - 1:i * r]
            qt = (q[i * r:(i + 1) * r] * jnp.exp2(cum[i * r:(i + 1) * r] - ref)).astype(BF16)
            kt = (k[:i * r] * jnp.exp2(ref - cum[:i * r])).astype(BF16)
            kt = jnp.concatenate([kt, jnp.zeros((LANES - i * r, A_DK), BF16)], axis=0)
            off.append(_dot_nt(qt, kt))
        vb = v.astype(BF16)
        last = cum[c - 1:c]
        kh = (k * jnp.exp2(last - cum)).astype(BF16)
        if c < LANES:
            zpad = jnp.zeros((LANES - c, A_DK), BF16)
            vb = jnp.concatenate([vb, zpad], axis=0)
            kh = jnp.concatenate([kh, zpad], axis=0)
        st_new = st * jnp.exp2(last) + _dot_tn(vb, kh)
        light.append((o_state, off, vb, st_new))

    rsums = []
    for (q, k, _, _, _, _), cum in zip(chains, cums):
        prods = []
        for i in range(nsub):
            qs, cs = q[i * r:(i + 1) * r], cum[i * r:(i + 1) * r]
            for s in range(r):
                row = i * r + s
                p = (qs * k[row:row + 1]) * jnp.exp2(cs - cum[row:row + 1])
                prods.append(jnp.where(rowi >= s, p, 0.0).astype(BF16))
        if nsub % 2 == 0:
            half = len(prods) // 2
            both = jnp.concatenate([jnp.concatenate(prods[:half], axis=0), jnp.concatenate(prods[half:], axis=0)],
                                   axis=1)
            wr = lax.broadcasted_iota(I32, (2 * A_DK, 2 * LANES), 0) < A_DK
            wc = lax.broadcasted_iota(I32, (2 * A_DK, 2 * LANES), 1) < LANES
            sums = _dot(both, jnp.where(wr == wc, 1.0, 0.0).astype(BF16))
            rsums.append(lambda row, sums=sums, half=half: (
                sums[row * r:(row + 1) * r, :LANES] if row < half
                else sums[(row - half) * r:(row - half + 1) * r, LANES:]))
        else:
            sums = _dot(jnp.concatenate(prods, axis=0), jnp.ones((A_DK, LANES), BF16))
            rsums.append(lambda row, sums=sums: sums[row * r:(row + 1) * r])

    results = []
    for (_, _, _, _, gate, _), (o_state, off, vb, st_new), rsum in zip(chains, light, rsums):
        sc_rows = []
        for i in range(nsub):
            sci = jnp.zeros((r, LANES), F32)
            for s in range(r):
                row = i * r + s
                sci = jnp.where(lane == row, rsum(row), sci)
            sc_rows.append(sci if i == 0 else sci + off[i])
        sc = jnp.concatenate(sc_rows, axis=0).astype(BF16)
        o = o_state + _dot(sc, vb)
        ms = jnp.mean(o * o, axis=-1, keepdims=True)
        results.append((o * lax.rsqrt(ms + LN_EPS) * ng * gate, st_new))
    return results


def _hgrn_prompt_kernel(q_ref, k_ref, g_ref, v_ref, gate_ref, ng_ref, o_ref, sf_ref, st_scr):
    c = HGRN_CHUNK
    nchunk = q_ref.shape[1] // c
    tb = pl.program_id(1)

    @pl.when(tb == 0)
    def _():
        st_scr[...] = jnp.zeros_like(st_scr)

    ng = ng_ref[...]

    def body(j, carry):
        sl = pl.ds(pl.multiple_of(j * c, c), c)
        loaded = [(q_ref[h, sl, :].astype(F32), k_ref[h, sl, :], g_ref[h, sl, :], v_ref[h, sl, :].astype(F32),
                   gate_ref[h, sl, :].astype(F32), st_scr[h]) for h in range(A_HEADS)]
        for h, (o, st_new) in enumerate(_hgrn_chunks(loaded, ng, c)):
            st_scr[h] = st_new
            o_ref[h, sl, :] = o.astype(BF16)
        return carry

    lax.fori_loop(0, nchunk, body, 0)

    @pl.when(tb == pl.num_programs(1) - 1)
    def _():
        for h in range(A_HEADS):
            sf_ref[0, h] = st_scr[h].T


def _hgrn_prompt(hq, hk, hg, hv, hgate, ng, batch, seq, tb):
    n = batch * seq
    nt = seq // tb
    blk = pl.BlockSpec((A_HEADS, tb, A_DK), lambda b, t: (0, b * nt + t, 0))
    return pl.pallas_call(
        _hgrn_prompt_kernel, grid=(batch, nt),
        in_specs=[blk, blk, blk, blk, blk, pl.BlockSpec((1, A_DV), lambda b, t: (0, 0))],
        out_specs=[blk, pl.BlockSpec((1, A_HEADS, A_DK, A_DV), lambda b, t: (b, 0, 0, 0))],
        out_shape=[jax.ShapeDtypeStruct((A_HEADS, n, A_DV), BF16),
                   jax.ShapeDtypeStruct((batch, A_HEADS, A_DK, A_DV), F32)],
        scratch_shapes=[pltpu.VMEM((A_HEADS, A_DV, A_DK), F32)],
        compiler_params=pltpu.CompilerParams(dimension_semantics=("parallel", "arbitrary"),
                                             vmem_limit_bytes=VMEM_LIMIT),
        name="hgrn_prompt",
    )(hq, hk, hg, hv, hgate, ng)


def _hgrn_sample_kernel(q_ref, k_ref, g_ref, v_ref, gate_ref, ng_ref, s0_ref, o_ref, sf_ref, *, t, nb):
    c = HGRN_SUB
    ng = ng_ref[...]
    zpad = jnp.zeros((c - t, A_DK), F32)
    pad = lambda a: jnp.concatenate([a, zpad], axis=0)
    chains = []
    for h in range(A_HEADS):
        q, k, g = q_ref[h].astype(F32), k_ref[h], g_ref[h]
        v, gate = v_ref[h].astype(F32), gate_ref[h].astype(F32)
        for b in range(nb):
            sl = slice(b * t, (b + 1) * t)
            chains.append((pad(q[sl]), pad(k[sl]), pad(g[sl]), pad(v[sl]), pad(gate[sl]), s0_ref[b, h].T))
    results = _hgrn_chunks(chains, ng, c)
    for h in range(A_HEADS):
        for b in range(nb):
            sf_ref[b, h] = results[h * nb + b][1].T
        o_ref[h] = jnp.concatenate([results[h * nb + b][0][:t] for b in range(nb)], axis=0).astype(BF16)


def _hgrn_sample(hq, hk, hg, hv, hgate, ng, s0, t):
    bd = s0.shape[0]
    nb = 16 // t
    blk = pl.BlockSpec((A_HEADS, nb * t, A_DK), lambda i: (0, i, 0))
    sblk = pl.BlockSpec((nb, A_HEADS, A_DK, A_DV), lambda i: (i, 0, 0, 0))
    return pl.pallas_call(
        functools.partial(_hgrn_sample_kernel, t=t, nb=nb), grid=(bd // nb,),
        in_specs=[blk, blk, blk, blk, blk, pl.BlockSpec((1, A_DV), lambda i: (0, 0)), sblk],
        out_specs=[blk, sblk],
        out_shape=[jax.ShapeDtypeStruct((A_HEADS, bd * t, A_DV), BF16),
                   jax.ShapeDtypeStruct((bd, A_HEADS, A_DK, A_DV), F32)],
        compiler_params=pltpu.CompilerParams(dimension_semantics=("parallel",), vmem_limit_bytes=VMEM_LIMIT),
        name="hgrn_sample",
    )(hq, hk, hg, hv, hgate, ng, s0)


def _count_tiles(score_ref, ntiles, pred):
    def body(j, acc):
        ks = pl.multiple_of(j * ATT_TILE, ATT_TILE)
        hit = jnp.where(pred(score_ref[pl.ds(ks, ATT_TILE), :], ks), 1, 0)
        return acc + jnp.sum(hit.reshape(ATT_TILE // 8, 8, Q_BLOCK), axis=0)
    acc = lax.fori_loop(0, ntiles, body, jnp.zeros((8, Q_BLOCK), I32))
    return jnp.sum(acc, axis=0, keepdims=True)


def _kth_largest(count_ge, topk, shape, two_bits=False):
    c0 = count_ge(jnp.zeros(shape, F32))
    ok = c0 >= topk
    thr = jnp.where(ok, 0, INT_MIN)
    nge = jnp.where(ok, c0, 0)

    def place(bit, carry):
        thr, nge = carry
        cand = thr | bit
        cnt = count_ge(_key_to_float(cand))
        ok = cnt >= topk
        return jnp.where(ok, cand, thr), jnp.where(ok, cnt, nge)

    if not two_bits:
        return lax.fori_loop(0, 31, lambda i, c: place(jnp.left_shift(jnp.int32(1), 30 - i), c), (thr, nge))

    def place_two(i, carry):
        thr, nge = carry
        hi = jnp.left_shift(jnp.int32(1), 30 - 2 * i)
        lo = jnp.left_shift(jnp.int32(1), 29 - 2 * i)
        n_hi, n_lo, n_both = (count_ge(_key_to_float(thr | bits)) for bits in (hi, lo, hi | lo))
        ok_hi = n_hi >= topk
        thr1, nge1 = jnp.where(ok_hi, thr | hi, thr), jnp.where(ok_hi, n_hi, nge)
        n2 = jnp.where(ok_hi, n_both, n_lo)
        ok2 = n2 >= topk
        return jnp.where(ok2, thr1 | lo, thr1), jnp.where(ok2, n2, nge1)

    return place(jnp.int32(1), lax.fori_loop(0, 15, place_two, (thr, nge)))


def _dsa_prompt_kernel(ki_ref, qiT_ref, wT_ref, k_ref, qT_ref, vT_ref, gb_ref, o_ref,
                       score_scr, bias_scr, lg_scr, oT_scr, *, topk):
    i = pl.program_id(1)
    ntiles = (i * Q_BLOCK + Q_BLOCK + K_TILE - 1) // K_TILE
    qpos = i * Q_BLOCK + lax.broadcasted_iota(I32, (K_TILE, Q_BLOCK), 1)
    krow = lax.broadcasted_iota(I32, (K_TILE, Q_BLOCK), 0)

    qi = qiT_ref[0]
    wT = wT_ref[...]

    def score_body(j, carry):
        ks = pl.multiple_of(j * K_TILE, K_TILE)
        x = _dot(ki_ref[pl.ds(ks, K_TILE), :], qi)
        sc = jnp.zeros((K_TILE, Q_BLOCK), F32)
        for h in range(IDX_HEADS):
            sc = sc + jnp.maximum(x[:, h * Q_BLOCK:(h + 1) * Q_BLOCK], 0.0) * wT[h:h + 1]
        score_scr[pl.ds(ks, K_TILE), :] = jnp.where(krow + ks <= qpos, sc, -jnp.inf)
        return carry

    lax.fori_loop(0, ntiles, score_body, 0)

    natt = (i * Q_BLOCK + Q_BLOCK + ATT_TILE - 1) // ATT_TILE

    @pl.when(natt * (ATT_TILE // K_TILE) > ntiles)
    def _():
        score_scr[pl.ds(pl.multiple_of(ntiles * K_TILE, K_TILE), K_TILE), :] = jnp.full((K_TILE, Q_BLOCK), -jnp.inf, F32)

    count = functools.partial(_count_tiles, score_scr, natt)
    arow = lax.broadcasted_iota(I32, (ATT_TILE, Q_BLOCK), 0)
    thr, nge = lax.cond(
        (i + 1) * Q_BLOCK > topk,
        lambda: _kth_largest(lambda x: count(lambda tile, ks: tile >= x), topk, (1, Q_BLOCK)),
        lambda: (jnp.full((1, Q_BLOCK), KEY_NEG_INF, I32), jnp.zeros((1, Q_BLOCK), I32)))
    live = thr > KEY_NEG_INF
    thr = _key_to_float(jnp.maximum(thr, KEY_NEG_INF + 1))
    no_cut = jnp.full((1, Q_BLOCK), natt * ATT_TILE, I32)
    has_ties = jnp.max(jnp.where(live & (nge > topk), 1, 0)) > 0

    def tie_cut():
        need = topk - count(lambda tile, ks: tile > thr)
        nbits = max(1, int(math.ceil(math.log2(score_scr.shape[0] + 1))))

        def body(b, pos):
            cand = pos | jnp.left_shift(jnp.int32(1), nbits - 1 - b)
            below = count(lambda tile, ks: (tile == thr) & (arow + ks < cand))
            return jnp.where(below <= need, cand, pos)

        return lax.fori_loop(0, nbits, body, jnp.zeros((1, Q_BLOCK), I32))

    cut = lax.cond(has_ties, tie_cut, lambda: no_cut)

    def bias_body(j, carry):
        ks = pl.multiple_of(j * K_TILE, K_TILE)
        tile = score_scr[pl.ds(ks, K_TILE), :]
        sel = (tile > thr) | ((tile == thr) & (krow + ks < cut))
        bias_scr[pl.ds(ks, K_TILE), :] = jnp.where(sel, 0.0, -jnp.inf)
        return carry

    lax.fori_loop(0, natt * (ATT_TILE // K_TILE), bias_body, 0)

    npair = B_HEADS // 2
    z = jnp.zeros((B_DH, Q_BLOCK), BF16)
    rhs = []
    for p in range(npair):
        qp = qT_ref[p * LANES:(p + 1) * LANES, :]
        rhs.append(jnp.concatenate([jnp.concatenate([qp[:B_DH], z], axis=0),
                                    jnp.concatenate([z, qp[B_DH:]], axis=0)], axis=1))
    fold = lambda a: a.reshape(ATT_TILE // 8, 8, a.shape[1])

    def logit_body(j, mx):
        ks = pl.multiple_of(j * ATT_TILE, ATT_TILE)
        bias = bias_scr[pl.ds(ks, ATT_TILE), :]
        bias2 = jnp.concatenate([bias, bias], axis=1)
        lgs = [_dot(k_ref[pl.ds(ks, ATT_TILE), p * LANES:(p + 1) * LANES], rhs[p]) + bias2 for p in range(npair)]
        lg_scr[pl.ds(ks, ATT_TILE), :] = jnp.concatenate(lgs, axis=1)
        return tuple(jnp.maximum(m, jnp.max(fold(lg), axis=0)) for m, lg in zip(mx, lgs))

    mx = lax.fori_loop(0, natt, logit_body,
                       tuple(jnp.full((8, 2 * Q_BLOCK), -jnp.inf, F32) for _ in range(npair)))
    mx = jnp.concatenate([jnp.max(m, axis=0, keepdims=True) for m in mx], axis=1)
    oT_scr[...] = jnp.zeros_like(oT_scr)

    def value_body(j, den):
        ks = pl.multiple_of(j * ATT_TILE, ATT_TILE)
        new_den, new_acc = [], []
        for h in range(B_HEADS):
            cols = slice(h * Q_BLOCK, (h + 1) * Q_BLOCK)
            rows = slice(h * B_DH, (h + 1) * B_DH)
            e = jnp.exp(lg_scr[pl.ds(ks, ATT_TILE), cols] - mx[:, cols])
            new_den.append(den[h] + jnp.sum(fold(e), axis=0))
            new_acc.append(oT_scr[rows, :] + _dot(vT_ref[rows, pl.ds(ks, ATT_TILE)], e.astype(BF16)))
        for h in range(B_HEADS):
            oT_scr[h * B_DH:(h + 1) * B_DH, :] = new_acc[h]
        return tuple(new_den)

    den = lax.fori_loop(0, natt, value_body, tuple(jnp.zeros((8, Q_BLOCK), F32) for _ in range(B_HEADS)))
    for h in range(B_HEADS):
        rows = slice(h * B_DH, (h + 1) * B_DH)
        oT_scr[rows, :] = oT_scr[rows, :] / jnp.sum(den[h], axis=0, keepdims=True)

    o_ref[...] = (oT_scr[...].T * gb_ref[...].astype(F32)).astype(BF16)


def _dsa_prompt(kibf, qiT2, wT, kbf, qT, vT, gbs, batch, seq):
    n = batch * seq
    nq = seq // Q_BLOCK
    topk = min(TOPK_MAX, seq // 4)
    assert seq % ATT_TILE == 0 and topk <= K_TILE
    return pl.pallas_call(
        functools.partial(_dsa_prompt_kernel, topk=topk), grid=(batch, nq),
        in_specs=[
            pl.BlockSpec((seq, IDX_DIM), lambda b, i: (b, 0)),
            pl.BlockSpec((1, IDX_DIM, IDX_HEADS * Q_BLOCK), lambda b, i: (b * nq + i, 0, 0)),
            pl.BlockSpec((IDX_HEADS, Q_BLOCK), lambda b, i: (0, b * nq + i)),
            pl.BlockSpec((seq, B_WIDTH), lambda b, i: (b, 0)),
            pl.BlockSpec((B_WIDTH, Q_BLOCK), lambda b, i: (0, b * nq + i)),
            pl.BlockSpec((B_WIDTH, seq), lambda b, i: (0, b)),
            pl.BlockSpec((Q_BLOCK, B_WIDTH), lambda b, i: (b * nq + i, 0)),
        ],
        out_specs=pl.BlockSpec((Q_BLOCK, B_WIDTH), lambda b, i: (b * nq + i, 0)),
        out_shape=jax.ShapeDtypeStruct((n, B_WIDTH), BF16),
        scratch_shapes=[
            pltpu.VMEM((seq, Q_BLOCK), F32),
            pltpu.VMEM((seq, Q_BLOCK), F32),
            pltpu.VMEM((seq, B_HEADS * Q_BLOCK), F32),
            pltpu.VMEM((B_WIDTH, Q_BLOCK), F32),
        ],
        compiler_params=pltpu.CompilerParams(dimension_semantics=("parallel", "arbitrary"),
                                             vmem_limit_bytes=VMEM_LIMIT),
        name="dsa_prompt",
    )(kibf, qiT2, wT, kbf, qT, vT, gbs)


def _dsa_sample_kernel(pt_ref, qi_ref, wrep_ref, qbd_ref, kin_ref, kn_ref, vn_ref, gb_ref,
                       cki_hbm, ck_hbm, cv_hbm, o_ref, ki_buf, k_buf, v_buf, score_scr, sem_i, sem_k, sem_v,
                       *, t, npages, topk, depth, group, nseq):
    step = pl.program_id(0)
    nsteps = pl.num_programs(0)
    page = ki_buf.shape[3]
    total = (npages + 1) * page
    rows = nseq * t
    lane = lax.broadcasted_iota(I32, (t, page), 1)
    qrow = lax.broadcasted_iota(I32, (t, page), 0)

    def slot(p):
        return p % depth if isinstance(p, int) else lax.rem(p, depth)

    def ki_copy(seq, p):
        half = lax.rem(seq, 2 * nseq)
        return pltpu.make_async_copy(cki_hbm.at[pt_ref[seq, p], 0], ki_buf.at[half, p], sem_i.at[half * npages + p])

    def k_copy(seq, p):
        return pltpu.make_async_copy(ck_hbm.at[pt_ref[seq, p], 0], k_buf.at[slot(p)], sem_k.at[slot(p)])

    def v_copy(seq, p):
        return pltpu.make_async_copy(cv_hbm.at[pt_ref[seq, p], 0], v_buf.at[slot(p)], sem_v.at[slot(p)])

    def start_ki(seq):
        def body(p, carry):
            ki_copy(seq, p).start()
            return carry
        lax.fori_loop(0, npages, body, 0, unroll=math.gcd(npages, 8))

    @pl.when(step == 0)
    def _():
        for s in range(nseq):
            start_ki(jnp.int32(s))
        for p in range(depth):
            k_copy(0, p).start()
            v_copy(0, p).start()

    @pl.when(step + 1 < nsteps)
    def _():
        for s in range(nseq):
            start_ki((step + 1) * nseq + s)

    def scores(s, kidx_t):
        x = _dot(qi_ref[s], kidx_t)
        xw = jnp.maximum(x, 0.0) * wrep_ref[s]
        sc = xw[0:t]
        for h in range(1, IDX_HEADS):
            sc = sc + xw[h * t:(h + 1) * t]
        return sc

    for s in range(nseq):
        sq = step * nseq + s

        def score_pages(i, carry, s=s, sq=sq):
            for g in range(group):
                ki_copy(sq, i * group + g).wait()
            sc = [scores(s, ki_buf[lax.rem(sq, 2 * nseq), i * group + g].astype(BF16)) for g in range(group)]
            score_scr[s * t:(s + 1) * t, pl.ds(pl.multiple_of(i * group * page, group * page), group * page)] = (
                jnp.concatenate(sc, axis=1))
            return carry

        lax.fori_loop(0, npages // group, score_pages, 0)
        score_scr[s * t:(s + 1) * t, npages * page:] = jnp.where(lane <= qrow, scores(s, kin_ref[s]), -jnp.inf)

    def count(pred):
        hit = jnp.where(pred(score_scr[...]), 1, 0)
        part = hit[:, 0:page]
        for c in range(1, npages + 1):
            part = part + hit[:, c * page:(c + 1) * page]
        return jnp.sum(part, axis=1, keepdims=True)

    thr_all, nge = _kth_largest(lambda x: count(lambda sc: sc >= x), topk, (rows, 1), two_bits=True)
    live = thr_all > KEY_NEG_INF
    thr_all = _key_to_float(jnp.maximum(thr_all, KEY_NEG_INF + 1))
    has_ties = jnp.max(jnp.where(live & (nge > topk), 1, 0)) > 0
    pos_all = lax.broadcasted_iota(I32, (rows, total), 1)

    def tie_cut():
        need = topk - count(lambda sc: sc > thr_all)
        nbits = max(1, int(math.ceil(math.log2(total + 1))))

        def pos_step(j, pos):
            cand = pos | jnp.left_shift(jnp.int32(1), nbits - 1 - j)
            below = count(lambda sc: (sc == thr_all) & (pos_all < cand))
            return jnp.where(below <= need, cand, pos)

        return lax.fori_loop(0, nbits, pos_step, jnp.zeros((rows, 1), I32))

    cut_all = lax.cond(has_ties, tie_cut, lambda: jnp.full((rows, 1), total, I32))

    for s in range(nseq):
        _dsa_sample_attend(s, step * nseq + s, nsteps * nseq, thr_all[s * t:(s + 1) * t], cut_all[s * t:(s + 1) * t],
                           qbd_ref, kn_ref, vn_ref, gb_ref, o_ref, k_buf, v_buf, score_scr, k_copy, v_copy, slot,
                           t=t, npages=npages, depth=depth, group=group, page=page)


def _dsa_sample_attend(s, sq, nseqs, thr, cut, qbd_ref, kn_ref, vn_ref, gb_ref, o_ref, k_buf, v_buf, score_scr,
                       k_copy, v_copy, slot, *, t, npages, depth, group, page):
    lane = lax.broadcasted_iota(I32, (t, page), 1)
    qbd = qbd_ref[s]
    b, nb = sq, nseqs

    def masked_logits(k_pages):
        lgs = []
        for k_t, ks in k_pages:
            tile = score_scr[s * t:(s + 1) * t, pl.ds(ks, page)]
            sel = (tile > thr) | ((tile == thr) & (lane + ks < cut))
            sel = jnp.concatenate([sel.astype(I32)] * B_HEADS, axis=0) > 0
            lgs.append(jnp.where(sel, _dot(qbd, k_t), -jnp.inf))
        return jnp.concatenate(lgs, axis=1)

    def accumulate(lg, v_pages, carry):
        m_old, l_old, acc = carry
        m_new = jnp.maximum(m_old, jnp.max(lg, axis=1, keepdims=True))
        m_safe = jnp.where(m_new == -jnp.inf, 0.0, m_new)
        alpha = jnp.exp(m_old - m_safe)
        e = jnp.exp(lg - m_safe)
        l_new = alpha * l_old + jnp.sum(e, axis=1, keepdims=True)
        acc = alpha * acc
        for g, v_t in enumerate(v_pages):
            acc = acc + _dot_nt(e[:, g * page:(g + 1) * page].astype(BF16), v_t)
        return m_new, l_new, acc

    def group_pages(i):
        return [i * group + g for g in range(group)]

    def k_operands(i):
        return [(k_buf[slot(p)].astype(BF16), pl.multiple_of(p * page, page)) for p in group_pages(i)]

    def v_operands(i):
        return [v_buf[slot(p)].astype(BF16) for p in group_pages(i)]

    def refill(copy, i):
        for p in group_pages(i):
            @pl.when(p + depth < npages)
            def _():
                copy(b, p + depth).start()

            @pl.when((p + depth >= npages) & (b + 1 < nb))
            def _():
                copy(b + 1, p + depth - npages).start()

    ngroups = npages // group
    for p in group_pages(0):
        k_copy(b, p).wait()
    lg_first = masked_logits(k_operands(0))
    refill(k_copy, 0)

    def attend_groups(i, carry):
        lg, state = carry
        for p in group_pages(i + 1):
            k_copy(b, p).wait()
        for p in group_pages(i):
            v_copy(b, p).wait()
        lg_next = masked_logits(k_operands(i + 1))
        state = accumulate(lg, v_operands(i), state)
        refill(k_copy, i + 1)
        refill(v_copy, i)
        return lg_next, state

    nrow = B_HEADS * t
    state = (jnp.full((nrow, 1), -jnp.inf, F32), jnp.zeros((nrow, 1), F32), jnp.zeros((nrow, B_WIDTH), F32))
    lg_last, state = lax.fori_loop(0, ngroups - 1, attend_groups, (lg_first, state))
    for p in group_pages(ngroups - 1):
        v_copy(b, p).wait()
    state = accumulate(lg_last, v_operands(ngroups - 1), state)
    refill(v_copy, ngroups - 1)
    _, den, acc = accumulate(masked_logits([(kn_ref[s], npages * page)]), [vn_ref[s]], state)
    o = acc / den
    col = lax.broadcasted_iota(I32, (t, B_WIDTH), 1)
    out = jnp.zeros((t, B_WIDTH), F32)
    for h in range(B_HEADS):
        out = jnp.where((col >= h * B_DH) & (col < (h + 1) * B_DH), o[h * t:(h + 1) * t], out)
    o_ref[s] = (out * gb_ref[s].astype(F32)).astype(BF16)


def _dsa_sample(page_table, qi_rows, wrep, qbd, kin_t, kn_t, vn_t, gbs, cache_kidx_t, cache_k_t, cache_v_t, t):
    bd, npages = page_table.shape
    page = cache_kidx_t.shape[3]
    total = npages * page + t
    topk = min(TOPK_MAX, total // 4)
    group = max(g for g in range(1, SAMPLE_PAGE_GROUP + 1) if npages % g == 0)
    depth = max(d for d in range(group, min(SAMPLE_DMA_DEPTH, npages) + 1, group) if npages % d == 0)
    nseq = SAMPLE_SEQS_PER_STEP if bd % SAMPLE_SEQS_PER_STEP == 0 else 1
    per_b = lambda b, pt: (b, 0, 0)
    hbm = pl.BlockSpec(memory_space=pl.ANY)
    grid_spec = pltpu.PrefetchScalarGridSpec(
        num_scalar_prefetch=1, grid=(bd // nseq,),
        in_specs=[
            pl.BlockSpec((nseq, IDX_HEADS * t, IDX_DIM), per_b),
            pl.BlockSpec((nseq, IDX_HEADS * t, page), per_b),
            pl.BlockSpec((nseq, B_HEADS * t, B_WIDTH), per_b),
            pl.BlockSpec((nseq, IDX_DIM, page), per_b),
            pl.BlockSpec((nseq, B_WIDTH, page), per_b),
            pl.BlockSpec((nseq, B_WIDTH, page), per_b),
            pl.BlockSpec((nseq, t, B_WIDTH), per_b),
            hbm, hbm, hbm,
        ],
        out_specs=pl.BlockSpec((nseq, t, B_WIDTH), per_b),
        scratch_shapes=[
            pltpu.VMEM((2 * nseq, npages, IDX_DIM, page), F32),
            pltpu.VMEM((depth, B_WIDTH, page), F32),
            pltpu.VMEM((depth, B_WIDTH, page), F32),
            pltpu.VMEM((nseq * t, (npages + 1) * page), F32),
            pltpu.SemaphoreType.DMA((2 * nseq * npages,)),
            pltpu.SemaphoreType.DMA((depth,)),
            pltpu.SemaphoreType.DMA((depth,)),
        ],
    )
    return pl.pallas_call(
        functools.partial(_dsa_sample_kernel, t=t, npages=npages, topk=topk, depth=depth, group=group, nseq=nseq),
        grid_spec=grid_spec,
        out_shape=jax.ShapeDtypeStruct((bd, t, B_WIDTH), BF16),
        compiler_params=pltpu.CompilerParams(dimension_semantics=("arbitrary",), vmem_limit_bytes=VMEM_LIMIT),
        name="dsa_sample",
    )(page_table, qi_rows, wrep, qbd, kin_t, kn_t, vn_t, gbs, cache_kidx_t, cache_k_t, cache_v_t)


def _merge_kernel(x_ref, ma_ref, mb_ref, w_ref, g_ref, b_ref, y_ref, *, alpha):
    mix = jnp.concatenate([ma_ref[h] for h in range(A_HEADS)] + [mb_ref[...]], axis=1)
    y = alpha * x_ref[...] + _dot(mix, w_ref[...])
    mu = jnp.mean(y, axis=-1, keepdims=True)
    d = y - mu
    var = jnp.mean(d * d, axis=-1, keepdims=True)
    y_ref[...] = d * lax.rsqrt(var + LN_EPS) * g_ref[...] + b_ref[...]


def _merge(x, mix_a, mix_b, w_out, ln_g, ln_b, alpha, tm):
    n, dm = x.shape
    return pl.pallas_call(
        functools.partial(_merge_kernel, alpha=alpha), grid=(n // tm,),
        in_specs=[
            pl.BlockSpec((tm, dm), lambda i: (i, 0)),
            pl.BlockSpec((A_HEADS, tm, A_DV), lambda i: (0, i, 0)),
            pl.BlockSpec((tm, B_WIDTH), lambda i: (i, 0)),
            pl.BlockSpec((A_WIDTH + B_WIDTH, dm), lambda i: (0, 0)),
            pl.BlockSpec((1, dm), lambda i: (0, 0)),
            pl.BlockSpec((1, dm), lambda i: (0, 0)),
        ],
        out_specs=pl.BlockSpec((tm, dm), lambda i: (i, 0)),
        out_shape=jax.ShapeDtypeStruct((n, dm), F32),
        compiler_params=pltpu.CompilerParams(dimension_semantics=("parallel",), vmem_limit_bytes=VMEM_LIMIT),
        name="merge",
    )(x, mix_a, mix_b, w_out, ln_g, ln_b)


def _split_weights(w_in_l):
    offs = np.cumsum([0, 512, 512, 512, 512, 512, 512, 512, 512, IDX_HEADS * IDX_DIM, IDX_DIM, IDX_HEADS])
    col = lambda i: w_in_l[:, offs[i]:offs[i + 1]]
    qa, fa, ia, ga, qb, kb, vb, gb, qi, ki, wi = (col(i) for i in range(11))
    pad = jnp.zeros((w_in_l.shape[0], LANES - IDX_DIM), w_in_l.dtype)
    wn = jnp.concatenate([qa, fa, ia, ga, kb, gb, ki, pad], axis=1).astype(BF16)
    wt = jnp.concatenate([qb, kb, vb, qi, wi, ki], axis=1).T.astype(BF16)
    return wn, wt


def _layer(xp, xs, cache_k, cache_v, cache_kidx, s0_sample, page_table, w_in_l, lb_l, norm_g_l, kn_g_l, kn_b_l,
           w_out_l, ln_g_l, ln_b_l, alpha):
    b, l, dm = xp.shape
    bd, t, _ = xs.shape
    npages, page = page_table.shape[1], cache_k.shape[2]
    assert page == Q_BLOCK and l % page == 0
    past = npages * page
    wn, wt = _split_weights(w_in_l)
    lb = lb_l.reshape(1, A_WIDTH)
    ng = norm_g_l.reshape(1, A_DV)
    w_out_b = w_out_l.astype(BF16)
    lng, lnb = ln_g_l.reshape(1, dm), ln_b_l.reshape(1, dm)

    tm = 256
    xp2 = xp.reshape(b * l, dm)
    (hq, hk, hg, hv, hgate, kbf, gbs, kibf, qT, vT, qiT2, wT, k_p, v_p, ki_p) = _project(
        xp2, jnp.arange(l, dtype=I32), wn, wt, lb, kn_g_l, kn_b_l, tm)
    mix_a, s_p = _hgrn_prompt(hq, hk, hg, hv, hgate, ng, b, l, min(l, 1024))
    mix_b = _dsa_prompt(kibf, qiT2, wT, kbf, qT, vT, gbs, b, l)
    y_p = _merge(xp2, mix_a, mix_b, w_out_b, lng, lnb, alpha, 1024).reshape(b, l, dm)

    ns = bd * t
    xs2 = xs.reshape(ns, dm)
    pos_s = past + (jnp.arange(ns, dtype=I32) % t)
    (hq, hk, hg, hv, hgate, kbf, gbs, kibf, qT, vT, qiT2, wT, k_s, v_s, ki_s) = _project(
        xs2, pos_s, wn, wt, lb, kn_g_l, kn_b_l, ns)
    mix_a, s_s = _hgrn_sample(hq, hk, hg, hv, hgate, ng, s0_sample, t)
    qi_nat = qiT2.reshape(ns // Q_BLOCK, IDX_DIM, IDX_HEADS, Q_BLOCK).transpose(0, 3, 2, 1)
    qi_rows = qi_nat.reshape(bd, t, IDX_HEADS, IDX_DIM).transpose(0, 2, 1, 3).reshape(bd, IDX_HEADS * t, IDX_DIM)
    w_rows = wT.T.reshape(bd, t, IDX_HEADS).transpose(0, 2, 1).reshape(bd, IDX_HEADS * t, 1)
    wrep = jnp.broadcast_to(w_rows, (bd, IDX_HEADS * t, page))
    q_nat = qT.T.reshape(bd, 1, t, B_HEADS, B_DH)
    eye = jnp.eye(B_HEADS, dtype=BF16).reshape(1, B_HEADS, 1, B_HEADS, 1)
    qbd = (q_nat * eye).reshape(bd, B_HEADS * t, B_WIDTH)

    def per_seq(pages):
        feat = pages.shape[1]
        a = pages.transpose(1, 0, 2).reshape(feat, bd, t).transpose(1, 0, 2).astype(BF16)
        return jnp.concatenate([a, jnp.zeros((bd, feat, page - t), BF16)], axis=2)

    token_minor = lambda c: jnp.moveaxis(c, 2, -1).reshape(c.shape[0], c.shape[1], -1, page)
    mix_b = _dsa_sample(page_table, qi_rows, wrep, qbd, per_seq(ki_s), per_seq(k_s), per_seq(v_s),
                        gbs.reshape(bd, t, B_WIDTH), token_minor(cache_kidx), token_minor(cache_k),
                        token_minor(cache_v), t).reshape(ns, B_WIDTH)
    y_s = _merge(xs2, mix_a, mix_b, w_out_b, lng, lnb, alpha, ns).reshape(bd, t, dm)

    return (y_p, y_s, k_p, v_p, ki_p, s_p, k_s, v_s, ki_s, s_s)


def kernel(x_prompt, x_sample, cache_k, cache_v, cache_kidx, state_hgrn, page_table, w_in, hgrn_lb_logits,
           hgrn_norm_g, idx_norm_g, idx_norm_b, w_out, ln_g, ln_b):
    depth = w_in.shape[0]
    assert depth == 1, "one layer per step"
    b, l, _ = x_prompt.shape
    bd, t, _ = x_sample.shape
    page = cache_k.shape[2]
    alpha = (2.0 * depth) ** 0.25
    lbs = jnp.cumsum(jax.nn.softmax(hgrn_lb_logits.astype(F32), axis=0), axis=0)[:depth]
    (y_p, y_s, k_p, v_p, ki_p, s_p, k_s, v_s, ki_s, s_s) = _layer(
        x_prompt, x_sample, cache_k, cache_v, cache_kidx, state_hgrn[0], page_table, w_in[0], lbs[0],
        hgrn_norm_g[0], idx_norm_g[0], idx_norm_b[0], w_out[0], ln_g[0], ln_b[0], alpha)
    nat_p = lambda pg, *f: jnp.moveaxis(pg.reshape(b, l // page, 1, *f, page), -1, 3)
    nat_s = lambda pg, *f: pg.transpose(0, 2, 1).reshape(bd, 1, t, *f)
    return (
        y_p, y_s,
        nat_p(k_p, B_HEADS, B_DH), nat_p(v_p, B_HEADS, B_DH), nat_p(ki_p, IDX_DIM),
        s_p[None],
        nat_s(k_s, B_HEADS, B_DH), nat_s(v_s, B_HEADS, B_DH), nat_s(ki_s, IDX_DIM),
        s_s[None],
    )
```
